```python
import math
import jax
import jax.numpy as jnp
from jax import lax
import numpy as np

D_MODEL = 1024
BATCH = 8
SEQ = 2048
DEPTH = 1
DEC_BATCH = 128
DEC_SEQ = 8
PAST_LEN = 2048
PAGE_SIZE = 128

SB_HEADS = 8
SB_HEAD_DIM = 64
SB_WIDTH = SB_HEADS * SB_HEAD_DIM
Q_BLOCK = 128
SB_BIAS_INIT = -6.0
SSM_INNER = D_MODEL
SSM_HEAD_DIM = 64
SSM_HEADS = SSM_INNER // SSM_HEAD_DIM
SSM_GROUPS = 2
SSM_STATE = 128
CONV_WIDTH = 4
SSM_CHUNK = 128
CONV_DIM = SSM_INNER + 2 * SSM_GROUPS * SSM_STATE
PEER_HEADS = 8
PEER_KEYS = 128
PEER_EXPERTS = PEER_KEYS * PEER_KEYS
PEER_KEY_DIM = 256
PEER_TOPK = 16
PEER_TOKEN_BLOCK = 128
EPS = 1e-6
IN_SPLIT_SIZES = (SB_WIDTH, SB_WIDTH, SB_WIDTH, SSM_INNER, CONV_DIM, SSM_HEADS, D_MODEL, D_MODEL)
IN_PROJ_COLS = 3 * SB_WIDTH + SSM_INNER + CONV_DIM + SSM_HEADS + 2 * D_MODEL

kernel_name = 'hybrid_stickbreak_ssd_peer_decode_step'


def _split_points(sizes):
    pts, acc = [], 0
    for s in sizes[:-1]:
        acc += s
        pts.append(acc)
    return pts


def rms_norm(x, w):
    xf = x.astype(jnp.float32)
    y = xf * lax.rsqrt(jnp.mean(xf * xf, axis=-1, keepdims=True) + EPS)
    return (y * w.astype(jnp.float32)).astype(x.dtype)


def stick_breaking_block(q, k, v, q_pos, k_pos, sb_bias):
    z = jnp.einsum('bqhd,bkhd->bhqk', q.astype(jnp.float32), k.astype(jnp.float32)) * (SB_HEAD_DIM ** -0.5)
    z = z + sb_bias.astype(jnp.float32)[None, :, None, None]
    mask = k_pos[None, :] < q_pos[:, None]
    log_keep = jnp.where(mask, jax.nn.log_sigmoid(-z), 0.0)
    after = lax.cumsum(log_keep, axis=3, reverse=True) - log_keep
    w = jnp.where(mask, jnp.exp(jax.nn.log_sigmoid(z) + after), 0.0)
    return jnp.einsum('bhqk,bkhd->bqhd', w, v.astype(jnp.float32)).astype(q.dtype)


def stick_breaking(q, k, v, q_offset, k_pos, sb_bias):
    b, tq, h, d = q.shape
    q_pos = q_offset + jnp.arange(tq)
    if tq > Q_BLOCK and tq % Q_BLOCK == 0:
        nb = tq // Q_BLOCK
        qb = q.reshape(b, nb, Q_BLOCK, h, d).transpose(1, 0, 2, 3, 4)
        pb = q_pos.reshape(nb, Q_BLOCK)
        ob = lax.map(lambda a: stick_breaking_block(a[0], k, v, a[1], k_pos, sb_bias), (qb, pb))
        return ob.transpose(1, 0, 2, 3, 4).reshape(b, tq, h, d)
    return stick_breaking_block(q, k, v, q_pos, k_pos, sb_bias)


def causal_conv(xbc, buf, conv_w, conv_b):
    L = xbc.shape[1]
    xp = jnp.concatenate([buf.astype(xbc.dtype), xbc], axis=1)
    y = sum(xp[:, i:i + L] * conv_w[i] for i in range(CONV_WIDTH)) + conv_b
    return jax.nn.silu(y), xp[:, -(CONV_WIDTH - 1):]


def ssd(x, dt, A, Bm, Cm, h0):
    b, L = x.shape[:2]
    G, R, P, N = SSM_GROUPS, SSM_HEADS // SSM_GROUPS, SSM_HEAD_DIM, SSM_STATE
    Q = min(SSM_CHUNK, L)
    pad = (-L) % Q
    if pad:
        pz = lambda a: jnp.pad(a, [(0, 0), (0, pad)] + [(0, 0)] * (a.ndim - 2))
        x, dt, Bm, Cm = pz(x), pz(dt), pz(Bm), pz(Cm)
    nc = (L + pad) // Q
    xs = x.reshape(b, nc, Q, G, R, P).astype(jnp.float32)
    dts = dt.reshape(b, nc, Q, G, R)
    Bs = Bm.reshape(b, nc, Q, G, N).astype(jnp.float32)
    Cs = Cm.reshape(b, nc, Q, G, N).astype(jnp.float32)
    Acs = jnp.cumsum(dts * A.reshape(G, R), axis=2)
    xdt = xs * dts[..., None]
    causal = jnp.tril(jnp.ones((Q, Q), bool))[:, :, None, None]
    seg = Acs[:, :, :, None] - Acs[:, :, None, :]
    decay = jnp.exp(jnp.where(causal, seg, -jnp.inf))
    cb = jnp.einsum('bcign,bcjgn->bcijg', Cs, Bs)
    y_diag = jnp.einsum('bcijg,bcijgr,bcjgrp->bcigrp', cb, decay, xdt)
    decay_end = jnp.exp(Acs[:, :, -1:] - Acs)
    chunk_states = jnp.einsum('bcjgn,bcjgr,bcjgrp->bcgrpn', Bs, decay_end, xdt)
    chunk_decay = jnp.exp(Acs[:, :, -1])

    def step(hc, inp):
        s, dcy = inp
        return hc * dcy[..., None, None] + s, hc

    h_final, h_starts = lax.scan(step, h0.astype(jnp.float32).reshape(b, G, R, P, N),
                                 (chunk_states.transpose(1, 0, 2, 3, 4, 5), chunk_decay.transpose(1, 0, 2, 3)))
    h_starts = h_starts.transpose(1, 0, 2, 3, 4, 5)
    y_off = jnp.einsum('bcign,bcigr,bcgrpn->bcigrp', Cs, jnp.exp(Acs), h_starts)
    y = (y_diag + y_off).reshape(b, nc * Q, SSM_HEADS, P)[:, :L]
    return y, h_final.reshape(b, SSM_HEADS, P, N)


def token_mixer(h, past_k, past_v, conv_buf, ssm_h0, pos_offset, w_in, sb_bias, conv_w, conv_b, dt_bias, a_log,
                d_skip, ssm_norm_w, w_br_attn, w_br_ssm, w_out):
    b, L, _ = h.shape
    proj = h @ w_in
    q, k, v, z, xbc, dt, ga, gs = jnp.split(proj, _split_points(IN_SPLIT_SIZES), axis=-1)
    q = q.reshape(b, L, SB_HEADS, SB_HEAD_DIM)
    k = k.reshape(b, L, SB_HEADS, SB_HEAD_DIM)
    v = v.reshape(b, L, SB_HEADS, SB_HEAD_DIM)
    if past_k is None:
        k_all, v_all = k, v
    else:
        k_all = jnp.concatenate([past_k.astype(k.dtype), k], axis=1)
        v_all = jnp.concatenate([past_v.astype(v.dtype), v], axis=1)
    k_pos = jnp.arange(k_all.shape[1])
    o_attn = stick_breaking(q, k_all, v_all, pos_offset, k_pos, sb_bias).reshape(b, L, SB_WIDTH)
    xbc_act, new_buf = causal_conv(xbc, conv_buf, conv_w, conv_b)
    xs, Bm, Cm = jnp.split(xbc_act, [SSM_INNER, SSM_INNER + SSM_GROUPS * SSM_STATE], axis=-1)
    dtp = jax.nn.softplus(dt.astype(jnp.float32) + dt_bias.astype(jnp.float32))
    A = -jnp.exp(a_log.astype(jnp.float32))
    xh = xs.reshape(b, L, SSM_HEADS, SSM_HEAD_DIM)
    y, h_new = ssd(xh, dtp, A, Bm.reshape(b, L, SSM_GROUPS, SSM_STATE),
                   Cm.reshape(b, L, SSM_GROUPS, SSM_STATE), ssm_h0)
    y = y + d_skip.astype(jnp.float32)[:, None] * xh.astype(jnp.float32)
    y = y.reshape(b, L, SSM_INNER) * jax.nn.silu(z.astype(jnp.float32))
    gsz = SSM_INNER // SSM_GROUPS
    y = rms_norm(y.reshape(b, L, SSM_GROUPS, gsz), ssm_norm_w.reshape(SSM_GROUPS, gsz))
    y = y.reshape(b, L, SSM_INNER).astype(h.dtype)
    merged = jax.nn.sigmoid(ga) * (o_attn @ w_br_attn) + jax.nn.sigmoid(gs) * (y @ w_br_ssm)
    return merged @ w_out, k, v, new_buf, h_new


def peer(h, peer_w_query, peer_keys1, peer_keys2, peer_u, peer_v):
    b, L, D = h.shape
    T = b * L
    nblk = -(-T // PEER_TOKEN_BLOCK)
    xt = jnp.pad(h.reshape(T, D), ((0, nblk * PEER_TOKEN_BLOCK - T), (0, 0)))

    def block(xb):
        qh = (xb @ peer_w_query).reshape(-1, PEER_HEADS, 2, PEER_KEY_DIM // 2).astype(jnp.float32)
        s1 = jnp.einsum('thd,kd->thk', qh[:, :, 0], peer_keys1.astype(jnp.float32))
        s2 = jnp.einsum('thd,kd->thk', qh[:, :, 1], peer_keys2.astype(jnp.float32))
        v1, i1 = lax.top_k(s1, PEER_TOPK)
        v2, i2 = lax.top_k(s2, PEER_TOPK)
        cand = (v1[..., :, None] + v2[..., None, :]).reshape(-1, PEER_HEADS, PEER_TOPK * PEER_TOPK)
        vals, flat = lax.top_k(cand, PEER_TOPK)
        e = (jnp.take_along_axis(i1, flat // PEER_TOPK, axis=-1) * PEER_KEYS
             + jnp.take_along_axis(i2, flat % PEER_TOPK, axis=-1))
        g = jax.nn.softmax(vals, axis=-1)
        act = jax.nn.gelu(jnp.einsum('thkd,td->thk', peer_u[e], xb).astype(jnp.float32))
        return jnp.einsum('thk,thkd->td', (g * act).astype(xb.dtype), peer_v[e])

    out = lax.map(block, xt.reshape(nblk, PEER_TOKEN_BLOCK, D))
    return out.reshape(-1, D)[:T].reshape(b, L, D)


def layer(x, c, past_k, past_v, conv_buf, ssm_h0, pos_offset, norm1_w, norm2_w, w_ada, b_ada, w_in, sb_bias,
          conv_w, conv_b, dt_bias, a_log, d_skip, ssm_norm_w, w_br_attn, w_br_ssm, w_out,
          peer_w_query, peer_keys1, peer_keys2, peer_u, peer_v):
    mod = (jax.nn.silu(c) @ w_ada + b_ada)[:, None, :]
    sh1, sc1, g1, sh2, sc2, g2 = jnp.split(mod, 6, axis=-1)
    h = rms_norm(x, norm1_w) * (1 + sc1) + sh1
    mix, k_new, v_new, buf_new, h_new = token_mixer(h, past_k, past_v, conv_buf, ssm_h0, pos_offset, w_in, sb_bias,
                                                    conv_w, conv_b, dt_bias, a_log, d_skip, ssm_norm_w,
                                                    w_br_attn, w_br_ssm, w_out)
    x = x + g1 * mix
    h = rms_norm(x, norm2_w) * (1 + sc2) + sh2
    x = x + g2 * peer(h, peer_w_query, peer_keys1, peer_keys2, peer_u, peer_v)
    return x, k_new, v_new, buf_new, h_new


def setup_inputs(seed: int = 0) -> dict:
    key = jax.random.key(seed)
    ks = jax.random.split(key, 32)
    f32 = jnp.float32

    def nrm(k, shape, scale):
        return jax.random.normal(k, shape, f32) * scale

    n_pages = PAST_LEN // PAGE_SIZE
    n_used = DEC_BATCH * n_pages
    n_pool = (5 * n_used + 3) // 4
    page_table = jax.random.permutation(ks[0], n_pool)[:n_used].reshape(DEC_BATCH, n_pages).astype(jnp.int32)
    dt0 = jnp.exp(jax.random.uniform(ks[1], (DEPTH, SSM_HEADS), f32, math.log(1e-3), math.log(1e-1)))
    dt_bias = dt0 + jnp.log(-jnp.expm1(-dt0))
    a_log = jnp.log(jax.random.uniform(ks[2], (DEPTH, SSM_HEADS), f32, 1.0, 16.0))
    return {
        'x_prompt': nrm(ks[3], (BATCH, SEQ, D_MODEL), 1.0),
        'x_sample': nrm(ks[4], (DEC_BATCH, DEC_SEQ, D_MODEL), 1.0),
        'cache_k': nrm(ks[5], (DEPTH, n_pool, PAGE_SIZE, SB_HEADS, SB_HEAD_DIM), 1.0),
        'cache_v': nrm(ks[6], (DEPTH, n_pool, PAGE_SIZE, SB_HEADS, SB_HEAD_DIM), 1.0),
        'state_conv': nrm(ks[7], (DEPTH, DEC_BATCH, CONV_WIDTH - 1, CONV_DIM), 1.0),
        'state_ssm': nrm(ks[8], (DEPTH, DEC_BATCH, SSM_HEADS, SSM_HEAD_DIM, SSM_STATE), 0.5),
        'page_table': page_table,
        'c_prompt': nrm(ks[9], (BATCH, D_MODEL), 1.0),
        'c_sample': nrm(ks[10], (DEC_BATCH, D_MODEL), 1.0),
        'norm1_w': 1.0 + nrm(ks[11], (DEPTH, D_MODEL), 0.02),
        'norm2_w': 1.0 + nrm(ks[12], (DEPTH, D_MODEL), 0.02),
        'w_ada': nrm(ks[13], (DEPTH, D_MODEL, 6 * D_MODEL), 0.5 * D_MODEL ** -0.5),
        'b_ada': nrm(ks[14], (DEPTH, 6 * D_MODEL), 0.02),
        'w_in': nrm(ks[15], (DEPTH, D_MODEL, IN_PROJ_COLS), D_MODEL ** -0.5),
        'sb_bias': SB_BIAS_INIT + nrm(ks[29], (DEPTH, SB_HEADS), 0.1),
        'conv_w': nrm(ks[16], (DEPTH, CONV_WIDTH, CONV_DIM), CONV_WIDTH ** -0.5),
        'conv_b': nrm(ks[17], (DEPTH, CONV_DIM), 0.02),
        'dt_bias': dt_bias,
        'a_log': a_log,
        'd_skip': 1.0 + nrm(ks[18], (DEPTH, SSM_HEADS), 0.02),
        'ssm_norm_w': 1.0 + nrm(ks[19], (DEPTH, SSM_INNER), 0.02),
        'w_br_attn': nrm(ks[20], (DEPTH, SB_WIDTH, D_MODEL), SB_WIDTH ** -0.5),
        'w_br_ssm': nrm(ks[21], (DEPTH, SSM_INNER, D_MODEL), SSM_INNER ** -0.5),
        'w_out': nrm(ks[22], (DEPTH, D_MODEL, D_MODEL), D_MODEL ** -0.5),
        'peer_w_query': nrm(ks[23], (DEPTH, D_MODEL, PEER_HEADS * PEER_KEY_DIM), D_MODEL ** -0.5),
        'peer_keys1': nrm(ks[24], (DEPTH, PEER_KEYS, PEER_KEY_DIM // 2), (PEER_KEY_DIM // 2) ** -0.5),
        'peer_keys2': nrm(ks[25], (DEPTH, PEER_KEYS, PEER_KEY_DIM // 2), (PEER_KEY_DIM // 2) ** -0.5),
        'peer_u': nrm(ks[26], (DEPTH, PEER_EXPERTS, D_MODEL), D_MODEL ** -0.5),
        'peer_v': nrm(ks[27], (DEPTH, PEER_EXPERTS, D_MODEL), 0.5),
        'norm_f_w': 1.0 + nrm(ks[28], (D_MODEL,), 0.02),
    }


def reference(x_prompt, x_sample, cache_k, cache_v, state_conv, state_ssm, page_table, c_prompt, c_sample,
              norm1_w, norm2_w, w_ada, b_ada, w_in, sb_bias, conv_w, conv_b, dt_bias, a_log, d_skip, ssm_norm_w,
              w_br_attn, w_br_ssm, w_out, peer_w_query, peer_keys1, peer_keys2, peer_u, peer_v, norm_f_w):
    n_seq, n_pages = page_table.shape
    past_len = n_pages * PAGE_SIZE
    xp, xs = x_prompt, x_sample
    kp_l, vp_l, bp_l, hp_l, ks_l, vs_l, bs_l, hs_l = [], [], [], [], [], [], [], []
    for d in range(DEPTH):
        lw = (norm1_w[d], norm2_w[d], w_ada[d], b_ada[d], w_in[d], sb_bias[d], conv_w[d], conv_b[d], dt_bias[d],
              a_log[d], d_skip[d], ssm_norm_w[d], w_br_attn[d], w_br_ssm[d], w_out[d], peer_w_query[d],
              peer_keys1[d], peer_keys2[d], peer_u[d], peer_v[d])
        buf0 = jnp.zeros((xp.shape[0], CONV_WIDTH - 1, CONV_DIM), xp.dtype)
        h00 = jnp.zeros((xp.shape[0], SSM_HEADS, SSM_HEAD_DIM, SSM_STATE), jnp.float32)
        xp, kp, vp, bp, hp = layer(xp, c_prompt, None, None, buf0, h00, 0, *lw)
        past_k = cache_k[d][page_table].reshape(n_seq, past_len, SB_HEADS, SB_HEAD_DIM)
        past_v = cache_v[d][page_table].reshape(n_seq, past_len, SB_HEADS, SB_HEAD_DIM)
        xs, ksn, vsn, bsn, hsn = layer(xs, c_sample, past_k, past_v, state_conv[d], state_ssm[d], past_len, *lw)
        kp_l.append(kp); vp_l.append(vp); bp_l.append(bp); hp_l.append(hp)
        ks_l.append(ksn); vs_l.append(vsn); bs_l.append(bsn); hs_l.append(hsn)
    y_prompt = rms_norm(xp, norm_f_w)
    y_sample = rms_norm(xs, norm_f_w)
    return (y_prompt, y_sample, jnp.stack(kp_l), jnp.stack(vp_l), jnp.stack(bp_l), jnp.stack(hp_l),
            jnp.stack(ks_l), jnp.stack(vs_l), jnp.stack(bs_l), jnp.stack(hs_l))
```

```python
import functools
import math

import jax
import jax.numpy as jnp
from jax import lax
from jax.experimental import pallas as pl
from jax.experimental.pallas import tpu as pltpu

F32 = jnp.float32
BF16 = jnp.bfloat16

D_MODEL = 1024
EPS = 1e-6
SB_HEADS = 8
SB_HEAD_DIM = 64
SB_WIDTH = SB_HEADS * SB_HEAD_DIM
PAGE = 128
SSM_HEADS = 16
SSM_P = 64
SSM_GROUPS = 2
SSM_N = 128
SSM_INNER = SSM_HEADS * SSM_P
SSM_PAIRS = SSM_HEADS // 2
CONV_W = 4
CONV_DIM = SSM_INNER + 2 * SSM_GROUPS * SSM_N
SSM_CHUNK = 128
PEER_HEADS = 8
PEER_KEYS = 128
PEER_HALF = 128
PEER_TOPK = 16

LANES = 128
SUBLANES = 8
VMEM_LIMIT = 56 * 1024 * 1024

_PROJ_GROUPS = (("q", SB_WIDTH), ("k", SB_WIDTH), ("v", SB_WIDTH), ("z", SSM_INNER), ("xbc", CONV_DIM),
                ("dt", LANES), ("ga", D_MODEL), ("gs", D_MODEL))
_PROJ_COLS = sum(w for _, w in _PROJ_GROUPS)


def _params(*sem):
    return pltpu.CompilerParams(dimension_semantics=sem, vmem_limit_bytes=VMEM_LIMIT)


def _dot(a, b):
    return jnp.dot(a, b, preferred_element_type=F32)


def _dot_nt(a, b):
    return lax.dot_general(a, b, (((1,), (1,)), ((), ())), preferred_element_type=F32)


def _split3(x):
    hi = x.astype(BF16)
    r = x - hi.astype(F32)
    mid = r.astype(BF16)
    lo = (r - mid.astype(F32)).astype(BF16)
    return hi, mid, lo


def _dot3_l(x, m):
    hi, mid, lo = _split3(x)
    return _dot(hi, m) + _dot(mid, m) + _dot(lo, m)


def _dot3_r(m, x):
    hi, mid, lo = _split3(x)
    return _dot(m, hi) + _dot(m, mid) + _dot(m, lo)


def _softplus(x):
    return jnp.maximum(x, 0.0) + jnp.log1p(jnp.exp(-jnp.abs(x)))


def _silu(x):
    return x * jax.nn.sigmoid(x)


def _adaln_kernel(c_ref, w_ref, b_ref, o_ref):
    s = _silu(c_ref[...]).astype(BF16)
    o_ref[...] = _dot(s, w_ref[...].astype(BF16)) + b_ref[...]


def _adaln(c_all, w_ada, b_ada):
    m, n = c_all.shape[0], w_ada.shape[1]
    tn = 1024
    return pl.pallas_call(
        _adaln_kernel,
        grid=(n // tn,),
        in_specs=[pl.BlockSpec((m, D_MODEL), lambda j: (0, 0)),
                  pl.BlockSpec((D_MODEL, tn), lambda j: (0, j)),
                  pl.BlockSpec((1, tn), lambda j: (0, j))],
        out_specs=pl.BlockSpec((m, tn), lambda j: (0, j)),
        out_shape=jax.ShapeDtypeStruct((m, n), F32),
        compiler_params=_params("arbitrary"),
        name="adaln",
    )(c_all, w_ada, b_ada.reshape(1, n))


def _inproj_kernel(x_ref, sh_ref, sc_ref, nw_ref, w_ref, q_ref, k_ref, v_ref, kb_ref, vb_ref, z_ref, xbc_ref,
                   dt_ref, ga_ref, gs_ref):
    x = x_ref[...]
    g, r, d = x.shape
    y = x * lax.rsqrt(jnp.mean(x * x, axis=-1, keepdims=True) + EPS) * nw_ref[...]
    h = y * (1.0 + sc_ref[...]) + sh_ref[...]
    hb = h.reshape(g * r, d).astype(BF16)
    outs = {"q": q_ref, "k": k_ref, "v": v_ref, "z": z_ref, "xbc": xbc_ref, "dt": dt_ref, "ga": ga_ref, "gs": gs_ref}
    col = 0
    for name, width in _PROJ_GROUPS:
        res = _dot(hb, w_ref[:, col:col + width])
        col += width
        if name == "q":
            res = res * (SB_HEAD_DIM ** -0.5)
        outs[name][...] = res.astype(outs[name].dtype)
        if name == "k":
            kb_ref[...] = res.astype(BF16)
        if name == "v":
            vb_ref[...] = res.astype(BF16)


def _inproj(x3, mod3, norm_w, w_packed, groups_per_block, blocks_per_mod):
    ng, r, d = x3.shape
    gpb = groups_per_block
    nblk = ng // gpb
    tm = gpb * r
    t = ng * r
    if gpb == 1:
        mod_map = lambda col: (lambda i: (i // blocks_per_mod, 0, col))
    else:
        mod_map = lambda col: (lambda i: (i, 0, col))
    widths = dict(_PROJ_GROUPS)
    out_shapes = [
        jax.ShapeDtypeStruct((t, widths["q"]), F32), jax.ShapeDtypeStruct((t, widths["k"]), F32),
        jax.ShapeDtypeStruct((t, widths["v"]), F32), jax.ShapeDtypeStruct((t, widths["k"]), BF16),
        jax.ShapeDtypeStruct((t, widths["v"]), BF16), jax.ShapeDtypeStruct((t, widths["z"]), F32),
        jax.ShapeDtypeStruct((t, widths["xbc"]), F32), jax.ShapeDtypeStruct((t, widths["dt"]), F32),
        jax.ShapeDtypeStruct((t, widths["ga"]), F32), jax.ShapeDtypeStruct((t, widths["gs"]), F32)]
    out_specs = [pl.BlockSpec((tm, s.shape[1]), lambda i: (i, 0)) for s in out_shapes]
    return pl.pallas_call(
        _inproj_kernel,
        grid=(nblk,),
        in_specs=[pl.BlockSpec((gpb, r, d), lambda i: (i, 0, 0)),
                  pl.BlockSpec((gpb, 1, d), mod_map(0)),
                  pl.BlockSpec((gpb, 1, d), mod_map(1)),
                  pl.BlockSpec((1, 1, d), lambda i: (0, 0, 0)),
                  pl.BlockSpec((d, _PROJ_COLS), lambda i: (0, 0))],
        out_specs=out_specs,
        out_shape=out_shapes,
        compiler_params=_params("arbitrary"),
        name="inproj",
    )(x3, mod3, mod3, norm_w.reshape(1, 1, d), w_packed)


ATT_TQ = 256


def _stick_tile(z, mask, tri, cum):
    sp = jnp.log1p(jnp.exp(-jnp.abs(z)))
    lsz = jnp.minimum(z, 0.0) - sp
    lk = jnp.where(mask, -jnp.maximum(z, 0.0) - sp, 0.0)
    hi = lk.astype(BF16)
    lo = (lk - hi.astype(F32)).astype(BF16)
    incl = _dot(hi, tri) + _dot(lo, tri)
    after = incl - lk + cum
    w = jnp.where(mask, jnp.exp(lsz + after), 0.0)
    return w, cum + incl[:, 0:1]


def _attn_prompt_kernel(bias_ref, q_ref, k_ref, v_ref, tri_ref, o_ref):
    hp = pl.program_id(1)
    qi = pl.program_id(2)
    tq = ATT_TQ
    q = q_ref[...].astype(BF16)
    lane = lax.broadcasted_iota(jnp.int32, (tq, LANES), 1)
    rel = lax.broadcasted_iota(jnp.int32, (tq, tq), 0) - lax.broadcasted_iota(jnp.int32, (tq, tq), 1)
    tri = tri_ref[...]
    accs = []
    for hh in range(2):
        qh = jnp.where((lane // SB_HEAD_DIM) == hh, q, jnp.zeros_like(q))
        bias = bias_ref[hp * 2 + hh]

        def body(step, carry, qh=qh, bias=bias):
            acc, cum = carry
            off = pl.multiple_of((qi - step) * tq, tq)
            kblk = k_ref[pl.ds(off, tq), :]
            vblk = v_ref[pl.ds(off, tq), :]
            z = _dot_nt(qh, kblk) + bias
            mask = (rel + step * tq) > 0
            w, cum = _stick_tile(z, mask, tri, cum)
            return acc + _dot(w.astype(BF16), vblk), cum

        acc, _ = lax.fori_loop(0, qi + 1, body, (jnp.zeros((tq, LANES), F32), jnp.zeros((tq, 1), F32)))
        accs.append(acc)
    o_ref[...] = jnp.where(lane < SB_HEAD_DIM, accs[0], accs[1]).astype(o_ref.dtype)


def _tri_incl(n):
    r = lax.broadcasted_iota(jnp.int32, (n, n), 0)
    c = lax.broadcasted_iota(jnp.int32, (n, n), 1)
    return (r >= c).astype(BF16)


def _attn_prompt(q, kb, vb, sb_bias, batch, seq):
    tq = ATT_TQ
    nq = seq // tq
    return pl.pallas_call(
        _attn_prompt_kernel,
        grid=(batch, SB_HEADS // 2, nq),
        in_specs=[pl.BlockSpec(memory_space=pltpu.SMEM),
                  pl.BlockSpec((tq, LANES), lambda b, h, i: (b * nq + i, h)),
                  pl.BlockSpec((seq, LANES), lambda b, h, i: (b, h)),
                  pl.BlockSpec((seq, LANES), lambda b, h, i: (b, h)),
                  pl.BlockSpec((tq, tq), lambda b, h, i: (0, 0))],
        out_specs=pl.BlockSpec((tq, LANES), lambda b, h, i: (b * nq + i, h)),
        out_shape=jax.ShapeDtypeStruct((batch * seq, SB_WIDTH), BF16),
        compiler_params=_params("arbitrary", "arbitrary", "arbitrary"),
        name="attn_prompt",
    )(sb_bias, q, kb, vb, _tri_incl(tq))


def _attn_sample_kernel(n_pages, dec_seq, pt_ref, bias_ref, q_ref, kn_ref, vn_ref, tri_ref, *refs):
    k_pages = refs[:n_pages]
    v_pages = refs[n_pages:2 * n_pages]
    o_ref = refs[2 * n_pages]
    rows = SB_HEADS * dec_seq
    q = q_ref[...]
    qt = jnp.concatenate([q] * SB_HEADS, axis=0)
    row_head = lax.broadcasted_iota(jnp.int32, (rows, SB_WIDTH), 0) // dec_seq
    lane_head = lax.broadcasted_iota(jnp.int32, (rows, SB_WIDTH), 1) // SB_HEAD_DIM
    qbd = jnp.where(row_head == lane_head, qt, 0.0).astype(BF16)
    bias = bias_ref[...]
    tri = tri_ref[...]
    r_i = lax.broadcasted_iota(jnp.int32, (rows, PAGE), 0) % dec_seq
    c_i = lax.broadcasted_iota(jnp.int32, (rows, PAGE), 1)
    all_valid = c_i >= 0
    new_valid = c_i < r_i

    def pad_rows(a):
        return jnp.concatenate([a, jnp.zeros((PAGE - dec_seq, SB_WIDTH), a.dtype)], axis=0)

    acc = jnp.zeros((rows, SB_WIDTH), F32)
    cum = jnp.zeros((rows, 1), F32)
    blocks = [(lambda: pad_rows(kn_ref[...]), lambda: pad_rows(vn_ref[...]), new_valid)]
    for p in range(n_pages - 1, -1, -1):
        blocks.append((lambda p=p: k_pages[p][0], lambda p=p: v_pages[p][0], all_valid))
    for load_k, load_v, mask in blocks:
        kblk = load_k()
        vblk = load_v()
        z = _dot_nt(qbd, kblk.astype(BF16)) + bias
        w, cum = _stick_tile(z, mask, tri, cum)
        acc = acc + _dot(w.astype(BF16), vblk.astype(BF16))
    out = jnp.zeros((dec_seq, SB_WIDTH), F32)
    lane_head8 = lax.broadcasted_iota(jnp.int32, (dec_seq, SB_WIDTH), 1) // SB_HEAD_DIM
    for h in range(SB_HEADS):
        out = out + jnp.where(lane_head8 == h, acc[h * dec_seq:(h + 1) * dec_seq, :], 0.0)
    o_ref[...] = out.astype(o_ref.dtype)


def _attn_sample(q, k_new, v_new, cache_k, cache_v, page_table, sb_bias, dec_seq):
    n_seq, n_pages = page_table.shape
    rows = SB_HEADS * dec_seq
    bias_rows = jnp.broadcast_to(jnp.repeat(sb_bias, dec_seq)[:, None], (rows, PAGE))

    def page_spec(p):
        return pl.BlockSpec((1, PAGE, SB_WIDTH), lambda s, pt: (pt[s, p], 0, 0))

    tok_spec = pl.BlockSpec((dec_seq, SB_WIDTH), lambda s, pt: (s, 0))
    grid_spec = pltpu.PrefetchScalarGridSpec(
        num_scalar_prefetch=1,
        grid=(n_seq,),
        in_specs=[pl.BlockSpec((rows, PAGE), lambda s, pt: (0, 0)), tok_spec, tok_spec, tok_spec,
                  pl.BlockSpec((PAGE, PAGE), lambda s, pt: (0, 0))]
                 + [page_spec(p) for p in range(n_pages)] + [page_spec(p) for p in range(n_pages)],
        out_specs=tok_spec,
    )
    return pl.pallas_call(
        functools.partial(_attn_sample_kernel, n_pages, dec_seq),
        grid_spec=grid_spec,
        out_shape=jax.ShapeDtypeStruct((n_seq * dec_seq, SB_WIDTH), F32),
        compiler_params=_params("arbitrary"),
        name="attn_sample",
    )(page_table, bias_rows, q, k_new, v_new, _tri_incl(PAGE), *([cache_k] * n_pages), *([cache_v] * n_pages))


def _expand_matrix(reps):
    r = lax.broadcasted_iota(jnp.int32, (LANES, SSM_HEADS * reps), 0)
    c = lax.broadcasted_iota(jnp.int32, (LANES, SSM_HEADS * reps), 1)
    return (c // reps == r).astype(BF16)


def _ssd_core(act, dt_raw, z, valid, tri, e1, e2, dtb, a_row, dsk, nw, state_in, c_masks):
    xs = act[:, :SSM_INNER]
    bm = act[:, SSM_INNER:SSM_INNER + SSM_GROUPS * SSM_N]
    cm = act[:, SSM_INNER + SSM_GROUPS * SSM_N:]
    dt = _softplus(dt_raw + dtb)
    dta = dt * a_row
    acs = _dot3_r(tri, dta)
    acs_t = acs.T
    acs_e2 = _dot3_l(acs, e2)
    dt_e = _dot3_l(dt, e1)
    eacs_e = _dot3_l(jnp.exp(acs), e1)
    n_slots = len(state_in)
    dend_e = []
    cdec_e = []
    for s in range(n_slots):
        if c_masks[s] is None:
            last = acs[SSM_CHUNK - 1:SSM_CHUNK, :]
        else:
            last = jnp.min(jnp.where(c_masks[s], acs, jnp.inf), axis=0, keepdims=True)
        de = jnp.exp(last - acs)
        if c_masks[s] is not None:
            de = jnp.where(c_masks[s], de, 0.0)
        dend_e.append(_dot3_l(de, e1))
        cdec_e.append(_dot3_l(jnp.broadcast_to(jnp.exp(last), (SUBLANES, LANES)), e1)[0:1, :])
    bmb = [bm[:, g * SSM_N:(g + 1) * SSM_N] for g in range(SSM_GROUPS)]
    cmb = [cm[:, g * SSM_N:(g + 1) * SSM_N].astype(BF16) for g in range(SSM_GROUPS)]
    cb = [_dot_nt(cmb[g], bmb[g].astype(BF16)) for g in range(SSM_GROUPS)]
    bt = [bmb[g].T.astype(BF16) for g in range(SSM_GROUPS)]
    lane = lax.broadcasted_iota(jnp.int32, (SSM_CHUNK, LANES), 1)
    y_pairs = []
    new_states = [[None] * SSM_PAIRS for _ in range(n_slots)]
    for k in range(SSM_PAIRS):
        g = (2 * k) // (SSM_HEADS // SSM_GROUPS)
        sl = slice(k * LANES, (k + 1) * LANES)
        xdt = xs[:, sl] * dt_e[:, sl]
        xdt_b = xdt.astype(BF16)
        parts = []
        for hh in range(2):
            h = 2 * k + hh
            seg = acs_e2[:, h * LANES:(h + 1) * LANES] - acs_t[h:h + 1, :]
            decay = jnp.where(valid, jnp.exp(jnp.where(valid, seg, 0.0)), 0.0)
            parts.append(_dot((cb[g] * decay).astype(BF16), xdt_b))
        y = jnp.where(lane < SSM_P, parts[0], parts[1])
        y_off = None
        for s in range(n_slots):
            st = state_in[s][k]
            c_s = cmb[g] if c_masks[s] is None else jnp.where(c_masks[s], cmb[g], jnp.zeros_like(cmb[g]))
            contrib = _dot(c_s, st.astype(BF16))
            y_off = contrib if y_off is None else y_off + contrib
            xdd = (xdt * dend_e[s][:, sl]).astype(BF16)
            new_states[s][k] = st * cdec_e[s][:, sl] + _dot(bt[g], xdd)
        y = y + y_off * eacs_e[:, sl] + dsk[:, sl] * xs[:, sl]
        y_pairs.append(y)
    y = jnp.concatenate(y_pairs, axis=1) * _silu(z)
    gsz = SSM_INNER // SSM_GROUPS
    outs = []
    for g in range(SSM_GROUPS):
        yg = y[:, g * gsz:(g + 1) * gsz]
        outs.append(yg * lax.rsqrt(jnp.mean(yg * yg, axis=-1, keepdims=True) + EPS) * nw[:, g * gsz:(g + 1) * gsz])
    return jnp.concatenate(outs, axis=1), new_states


def _ssd_prompt_kernel(xbc_ref, z_ref, dt_ref, cw_ref, cbias_ref, dtb_ref, a_ref, dsk_ref, nw_ref, tri_ref, e1_ref,
                       e2_ref, y_ref, tail_ref, st_ref, xs_scr, st_scr):
    c = pl.program_id(1)

    @pl.when(c == 0)
    def _():
        xs_scr[0:SUBLANES, :] = jnp.zeros((SUBLANES, CONV_DIM), F32)
        st_scr[...] = jnp.zeros_like(st_scr)

    xs_scr[SUBLANES:SUBLANES + SSM_CHUNK, :] = xbc_ref[...]
    conv = cbias_ref[...]
    for i in range(CONV_W):
        off = SUBLANES - (CONV_W - 1) + i
        conv = conv + xs_scr[off:off + SSM_CHUNK, :] * cw_ref[i:i + 1, :]
    tail = xs_scr[SSM_CHUNK:SSM_CHUNK + SUBLANES, :]
    xs_scr[0:SUBLANES, :] = tail
    tail_ref[0] = tail
    act = _silu(conv)
    r = lax.broadcasted_iota(jnp.int32, (SSM_CHUNK, SSM_CHUNK), 0)
    cc = lax.broadcasted_iota(jnp.int32, (SSM_CHUNK, SSM_CHUNK), 1)
    valid = r >= cc
    state_in = [[st_scr[k] for k in range(SSM_PAIRS)]]
    y, new_states = _ssd_core(act, dt_ref[...], z_ref[...], valid, tri_ref[...], e1_ref[...], e2_ref[...],
                              dtb_ref[...], a_ref[...], dsk_ref[...], nw_ref[...], state_in, [None])
    for k in range(SSM_PAIRS):
        st_scr[k] = new_states[0][k]
        st_ref[0, k] = new_states[0][k]
    y_ref[...] = y.astype(y_ref.dtype)


def _ssd_consts(conv_w, conv_b, dt_bias, a_log, d_skip, ssm_norm_w):
    pad = LANES - SSM_HEADS
    cw = jnp.concatenate([conv_w, jnp.zeros((SUBLANES - CONV_W, CONV_DIM), F32)], axis=0)
    dtb = jnp.pad(dt_bias, (0, pad)).reshape(1, LANES)
    a_row = jnp.pad(-jnp.exp(a_log), (0, pad)).reshape(1, LANES)
    dsk = jnp.repeat(d_skip, SSM_P).reshape(1, SSM_INNER)
    return cw, conv_b.reshape(1, CONV_DIM), dtb, a_row, dsk, ssm_norm_w.reshape(1, SSM_INNER)


def _const_spec(shape, ngrid):
    zeros = (0,) * len(shape)
    if ngrid == 1:
        return pl.BlockSpec(shape, lambda i: zeros)
    return pl.BlockSpec(shape, lambda i, j: zeros)


def _ssd_prompt(xbc, z, dt, consts, batch, seq):
    nc = seq // SSM_CHUNK
    cw, cbias, dtb, a_row, dsk, nw = consts
    tok = lambda w: pl.BlockSpec((SSM_CHUNK, w), lambda b, c: (b * nc + c, 0))
    tri = _tri_incl(SSM_CHUNK)
    return pl.pallas_call(
        _ssd_prompt_kernel,
        grid=(batch, nc),
        in_specs=[tok(CONV_DIM), tok(SSM_INNER), tok(LANES),
                  _const_spec(cw.shape, 2), _const_spec(cbias.shape, 2), _const_spec(dtb.shape, 2),
                  _const_spec(a_row.shape, 2), _const_spec(dsk.shape, 2), _const_spec(nw.shape, 2),
                  _const_spec((SSM_CHUNK, SSM_CHUNK), 2), _const_spec((LANES, SSM_INNER), 2),
                  _const_spec((LANES, SSM_HEADS * LANES), 2)],
        out_specs=[tok(SSM_INNER),
                   pl.BlockSpec((1, SUBLANES, CONV_DIM), lambda b, c: (b, 0, 0)),
                   pl.BlockSpec((1, SSM_PAIRS, SSM_N, LANES), lambda b, c: (b, 0, 0, 0))],
        out_shape=[jax.ShapeDtypeStruct((batch * seq, SSM_INNER), BF16),
                   jax.ShapeDtypeStruct((batch, SUBLANES, CONV_DIM), F32),
                   jax.ShapeDtypeStruct((batch, SSM_PAIRS, SSM_N, LANES), F32)],
        scratch_shapes=[pltpu.VMEM((SSM_CHUNK + SUBLANES, CONV_DIM), F32),
                        pltpu.VMEM((SSM_PAIRS, SSM_N, LANES), F32)],
        compiler_params=_params("arbitrary", "arbitrary"),
        name="ssd_prompt",
    )(xbc, z, dt, cw, cbias, dtb, a_row, dsk, nw, tri, _expand_matrix(SSM_P), _expand_matrix(LANES))


def _ssd_sample_kernel(dec_seq, xbc_ref, prev_ref, z_ref, dt_ref, st_in_ref, cw_ref, cbias_ref, dtb_ref, a_ref,
                       dsk_ref, nw_ref, tri_ref, e1_ref, e2_ref, y_ref, st_ref):
    n_slots = SSM_CHUNK // dec_seq
    cur = xbc_ref[...]
    prev = prev_ref[...]
    pos = lax.broadcasted_iota(jnp.int32, (SSM_CHUNK, CONV_DIM), 0) % dec_seq
    conv = cbias_ref[...] + cur * cw_ref[CONV_W - 1:CONV_W, :]
    for k in range(1, CONV_W):
        shifted = jnp.where(pos < k, pltpu.roll(prev, SSM_CHUNK - dec_seq + k, axis=0), pltpu.roll(cur, k, axis=0))
        conv = conv + shifted * cw_ref[CONV_W - 1 - k:CONV_W - k, :]
    act = _silu(conv)
    r = lax.broadcasted_iota(jnp.int32, (SSM_CHUNK, SSM_CHUNK), 0)
    cc = lax.broadcasted_iota(jnp.int32, (SSM_CHUNK, SSM_CHUNK), 1)
    valid = (r >= cc) & ((r // dec_seq) == (cc // dec_seq))
    row_seq = lax.broadcasted_iota(jnp.int32, (SSM_CHUNK, LANES), 0) // dec_seq
    c_masks = [row_seq == s for s in range(n_slots)]
    state_in = [[st_in_ref[s, k] for k in range(SSM_PAIRS)] for s in range(n_slots)]
    y, new_states = _ssd_core(act, dt_ref[...], z_ref[...], valid, tri_ref[...], e1_ref[...], e2_ref[...],
                              dtb_ref[...], a_ref[...], dsk_ref[...], nw_ref[...], state_in, c_masks)
    for s in range(n_slots):
        for k in range(SSM_PAIRS):
            st_ref[s, k] = new_states[s][k]
    y_ref[...] = y.astype(y_ref.dtype)


def _ssd_sample(xbc, prev, z, dt, state_t, consts, dec_seq):
    n_tok = xbc.shape[0]
    n_slots = SSM_CHUNK // dec_seq
    nblk = n_tok // SSM_CHUNK
    cw, cbias, dtb, a_row, dsk, nw = consts
    tok = lambda w: pl.BlockSpec((SSM_CHUNK, w), lambda i: (i, 0))
    st_spec = pl.BlockSpec((n_slots, SSM_PAIRS, SSM_N, LANES), lambda i: (i, 0, 0, 0))
    r = lax.broadcasted_iota(jnp.int32, (SSM_CHUNK, SSM_CHUNK), 0)
    c = lax.broadcasted_iota(jnp.int32, (SSM_CHUNK, SSM_CHUNK), 1)
    tri = ((r >= c) & ((r // dec_seq) == (c // dec_seq))).astype(BF16)
    return pl.pallas_call(
        functools.partial(_ssd_sample_kernel, dec_seq),
        grid=(nblk,),
        in_specs=[tok(CONV_DIM), tok(CONV_DIM), tok(SSM_INNER), tok(LANES), st_spec,
                  _const_spec(cw.shape, 1), _const_spec(cbias.shape, 1), _const_spec(dtb.shape, 1),
                  _const_spec(a_row.shape, 1), _const_spec(dsk.shape, 1), _const_spec(nw.shape, 1),
                  _const_spec((SSM_CHUNK, SSM_CHUNK), 1), _const_spec((LANES, SSM_INNER), 1),
                  _const_spec((LANES, SSM_HEADS * LANES), 1)],
        out_specs=[tok(SSM_INNER), st_spec],
        out_shape=[jax.ShapeDtypeStruct((n_tok, SSM_INNER), BF16),
                   jax.ShapeDtypeStruct(state_t.shape, F32)],
        compiler_params=_params("arbitrary"),
        name="ssd_sample",
    )(xbc, prev, z, dt, state_t, cw, cbias, dtb, a_row, dsk, nw, tri, _expand_matrix(SSM_P), _expand_matrix(LANES))


def _merge_kernel(o_ref, y_ref, ga_ref, gs_ref, x_ref, g1_ref, sh_ref, sc_ref, nw_ref, wa_ref, ws_ref, wo_ref,
                  x1_ref, h2_ref):
    merged = (jax.nn.sigmoid(ga_ref[...]) * _dot(o_ref[...].astype(BF16), wa_ref[...])
              + jax.nn.sigmoid(gs_ref[...]) * _dot(y_ref[...], ws_ref[...]))
    mix = _dot(merged.astype(BF16), wo_ref[...])
    x = x_ref[...]
    g, r, d = x.shape
    x1 = x + g1_ref[...] * mix.reshape(g, r, d)
    x1_ref[...] = x1
    y = x1 * lax.rsqrt(jnp.mean(x1 * x1, axis=-1, keepdims=True) + EPS) * nw_ref[...]
    h2 = y * (1.0 + sc_ref[...]) + sh_ref[...]
    h2_ref[...] = h2.reshape(g * r, d).astype(h2_ref.dtype)


def _mod_map(gpb, blocks_per_mod, col):
    if gpb == 1:
        return lambda i: (i // blocks_per_mod, 0, col)
    return lambda i: (i, 0, col)


def _merge(o_attn, y_ssm, ga, gs, x3, mod3, norm2_w, wa, ws, wo, groups_per_block, blocks_per_mod):
    ng, r, d = x3.shape
    gpb = groups_per_block
    nblk = ng // gpb
    tm = gpb * r
    t = ng * r
    tok = lambda w: pl.BlockSpec((tm, w), lambda i: (i, 0))
    xspec = pl.BlockSpec((gpb, r, d), lambda i: (i, 0, 0))
    mspec = lambda col: pl.BlockSpec((gpb, 1, d), _mod_map(gpb, blocks_per_mod, col))
    return pl.pallas_call(
        _merge_kernel,
        grid=(nblk,),
        in_specs=[tok(SB_WIDTH), tok(SSM_INNER), tok(d), tok(d), xspec, mspec(2), mspec(3), mspec(4),
                  pl.BlockSpec((1, 1, d), lambda i: (0, 0, 0)),
                  _const_spec(wa.shape, 1), _const_spec(ws.shape, 1), _const_spec(wo.shape, 1)],
        out_specs=[xspec, tok(d)],
        out_shape=[jax.ShapeDtypeStruct((ng, r, d), F32), jax.ShapeDtypeStruct((t, d), BF16)],
        compiler_params=_params("arbitrary"),
        name="merge",
    )(o_attn, y_ssm, ga, gs, x3, mod3, mod3, mod3, norm2_w.reshape(1, 1, d), wa, ws, wo)


def _batcher_pairs(n):
    pairs = []
    p = 1
    while p < n:
        k = p
        while k >= 1:
            for j in range(k % p, n - k, 2 * k):
                for i in range(min(k, n - j - k)):
                    if (i + j) // (2 * p) == (i + j + k) // (2 * p):
                        pairs.append((i + j, i + j + k))
            k //= 2
        p *= 2
    return pairs


_SORT16 = _batcher_pairs(PEER_TOPK)


def _sort_desc(vals):
    vals = list(vals)
    for i, j in _SORT16:
        hi = jnp.maximum(vals[i], vals[j])
        lo = jnp.minimum(vals[i], vals[j])
        vals[i], vals[j] = hi, lo
    return vals


def _merge_top(a, b, sort_result=True):
    n = PEER_TOPK
    t = [jnp.maximum(a[k], b[n - 1 - k]) for k in range(n)]
    if not sort_result:
        return t
    d = n // 2
    while d >= 1:
        for k in range(n):
            if (k // d) % 2 == 0:
                hi = jnp.maximum(t[k], t[k + d])
                lo = jnp.minimum(t[k], t[k + d])
                t[k], t[k + d] = hi, lo
        d //= 2
    return t


def _top16_desc(vals, need_sorted=True):
    groups = [_sort_desc(vals[i:i + PEER_TOPK]) for i in range(0, len(vals), PEER_TOPK)]
    while len(groups) > 1:
        nxt = []
        for i in range(0, len(groups) - 1, 2):
            last = len(groups) == 2
            nxt.append(_merge_top(groups[i], groups[i + 1], sort_result=need_sorted or not last))
        if len(groups) % 2:
            nxt.append(groups[-1])
        groups = nxt
    return groups[0]


ROUTE_TM = 256


def _route_kernel(h_ref, wq_ref, k1_ref, k2_ref, th_ref, p_ref, s2t_ref, r_ref, s1_scr, s2_scr):
    half = PEER_HEADS * PEER_HALF
    qt = _dot_nt(wq_ref[...], h_ref[...])
    s1 = _dot(k1_ref[...], qt[:half].astype(BF16))
    s2 = _dot(k2_ref[...], qt[half:].astype(BF16))
    for c in range(ROUTE_TM // LANES):
        s1_scr[c] = s1[:, c * LANES:(c + 1) * LANES]
        s2_scr[c] = s2[:, c * LANES:(c + 1) * LANES]
    staircase = [(a, b) for a in range(PEER_TOPK) for b in range(PEER_TOPK) if (a + 1) * (b + 1) <= PEER_TOPK]
    neg_inf = jnp.full((SUBLANES, LANES), -jnp.inf, F32)
    pos_inf = jnp.full((SUBLANES, LANES), jnp.inf, F32)
    for c in range(ROUTE_TM // LANES):
        tl = pl.ds(c * LANES, LANES)
        row = lambda scr, i, c=c: scr[c, i * SUBLANES:(i + 1) * SUBLANES, :]
        top1 = _top16_desc([row(s1_scr, i) for i in range(PEER_KEYS)])
        top2 = _top16_desc([row(s2_scr, i) for i in range(PEER_KEYS)])
        cand = {(a, b): top1[a] + top2[b] for a, b in staircase}
        cvals = [cand[ab] for ab in staircase]
        cvals += [neg_inf] * (-len(cvals) % PEER_TOPK)
        tau = _top16_desc(cvals, need_sorted=False)
        thr = tau[0]
        for v in tau[1:]:
            thr = jnp.minimum(thr, v)
        cmax = cand[(0, 0)]
        zsum = jnp.zeros((SUBLANES, LANES), F32)
        theta2 = [pos_inf] * PEER_TOPK
        for a, b in staircase:
            ok = cand[(a, b)] >= thr
            zsum = zsum + jnp.where(ok, jnp.exp(cand[(a, b)] - cmax), 0.0)
            theta2[a] = jnp.minimum(theta2[a], jnp.where(ok, top2[b], jnp.inf))
        inv_z = 1.0 / zsum
        for i in range(PEER_KEYS):
            s = row(s1_scr, i)
            th = pos_inf
            for a in range(PEER_TOPK):
                th = jnp.where(s == top1[a], theta2[a], th)
            th_ref[i, :, tl] = th
            p_ref[i, :, tl] = jnp.exp(s - top1[0]) * inv_z
        for h in range(PEER_HEADS):
            s2h = s2_scr[c, pl.ds(h, PEER_KEYS, stride=SUBLANES), :]
            s2t_ref[h, :, tl] = s2h
            r_ref[h, :, tl] = jnp.exp(s2h - top2[0][h:h + 1, :])


def _route(h2, wq_t, kbd1, kbd2):
    t = h2.shape[0]
    tm = ROUTE_TM
    half = PEER_HEADS * PEER_HALF
    return pl.pallas_call(
        _route_kernel,
        grid=(t // tm,),
        in_specs=[pl.BlockSpec((tm, D_MODEL), lambda i: (i, 0)),
                  _const_spec(wq_t.shape, 1), _const_spec(kbd1.shape, 1), _const_spec(kbd2.shape, 1)],
        out_specs=[pl.BlockSpec((PEER_KEYS, PEER_HEADS, tm), lambda i: (0, 0, i)),
                   pl.BlockSpec((PEER_KEYS, PEER_HEADS, tm), lambda i: (0, 0, i)),
                   pl.BlockSpec((PEER_HEADS, PEER_KEYS, tm), lambda i: (0, 0, i)),
                   pl.BlockSpec((PEER_HEADS, PEER_KEYS, tm), lambda i: (0, 0, i))],
        out_shape=[jax.ShapeDtypeStruct((PEER_KEYS, PEER_HEADS, t), F32),
                   jax.ShapeDtypeStruct((PEER_KEYS, PEER_HEADS, t), F32),
                   jax.ShapeDtypeStruct((PEER_HEADS, PEER_KEYS, t), F32),
                   jax.ShapeDtypeStruct((PEER_HEADS, PEER_KEYS, t), F32)],
        scratch_shapes=[pltpu.VMEM((tm // LANES, half, LANES), F32), pltpu.VMEM((tm // LANES, half, LANES), F32)],
        compiler_params=_params("arbitrary"),
        name="peer_route",
    )(h2, wq_t, kbd1, kbd2)


PEER_TM = 1024
PEER_KB = 4


def _gelu_tanh(x):
    return 0.5 * x * (1.0 + jnp.tanh(math.sqrt(2.0 / math.pi) * (x + 0.044715 * (x * x * x))))


def _peer_kernel(h_ref, u_ref, vt_ref, th_ref, p_ref, s2t_ref, r_ref, o_ref, acc_ref):
    j = pl.program_id(1)

    @pl.when(j == 0)
    def _():
        acc_ref[...] = jnp.zeros_like(acc_ref)

    act = _gelu_tanh(_dot_nt(u_ref[...], h_ref[...]))
    parts = []
    for ii in range(PEER_KB):
        w = None
        for h in range(PEER_HEADS):
            sel = jnp.where(s2t_ref[h] >= th_ref[ii, h:h + 1, :], r_ref[h], 0.0) * p_ref[ii, h:h + 1, :]
            w = sel if w is None else w + sel
        parts.append((w * act[ii * PEER_KEYS:(ii + 1) * PEER_KEYS, :]).astype(BF16))
    a = jnp.concatenate(parts, axis=0)
    acc_ref[...] += _dot(vt_ref[...], a)

    @pl.when(j == pl.num_programs(1) - 1)
    def _():
        o_ref[...] = acc_ref[...].T


def _peer_dense(h2, u_b, vt_b, th, p, s2t, r):
    t = h2.shape[0]
    tm = min(PEER_TM, t)
    eb = PEER_KB * PEER_KEYS
    nj = PEER_KEYS // PEER_KB
    return pl.pallas_call(
        _peer_kernel,
        grid=(t // tm, nj),
        in_specs=[pl.BlockSpec((tm, D_MODEL), lambda i, j: (i, 0)),
                  pl.BlockSpec((eb, D_MODEL), lambda i, j: (j, 0)),
                  pl.BlockSpec((D_MODEL, eb), lambda i, j: (0, j)),
                  pl.BlockSpec((PEER_KB, PEER_HEADS, tm), lambda i, j: (j, 0, i)),
                  pl.BlockSpec((PEER_KB, PEER_HEADS, tm), lambda i, j: (j, 0, i)),
                  pl.BlockSpec((PEER_HEADS, PEER_KEYS, tm), lambda i, j: (0, 0, i)),
                  pl.BlockSpec((PEER_HEADS, PEER_KEYS, tm), lambda i, j: (0, 0, i))],
        out_specs=pl.BlockSpec((tm, D_MODEL), lambda i, j: (i, 0)),
        out_shape=jax.ShapeDtypeStruct((t, D_MODEL), F32),
        scratch_shapes=[pltpu.VMEM((D_MODEL, tm), F32)],
        compiler_params=_params("arbitrary", "arbitrary"),
        name="peer_dense",
    )(h2, u_b, vt_b, th, p, s2t, r)


def _final_kernel(x_ref, pe_ref, g2_ref, nw_ref, y_ref):
    x1 = x_ref[...]
    g, r, d = x1.shape
    x2 = x1 + g2_ref[...] * pe_ref[...].reshape(g, r, d)
    y_ref[...] = x2 * lax.rsqrt(jnp.mean(x2 * x2, axis=-1, keepdims=True) + EPS) * nw_ref[...]


def _final(x1_3, peer_out, mod3, norm_f_w, groups_per_block, blocks_per_mod):
    ng, r, d = x1_3.shape
    gpb = groups_per_block
    nblk = ng // gpb
    tm = gpb * r
    xspec = pl.BlockSpec((gpb, r, d), lambda i: (i, 0, 0))
    return pl.pallas_call(
        _final_kernel,
        grid=(nblk,),
        in_specs=[xspec, pl.BlockSpec((tm, d), lambda i: (i, 0)),
                  pl.BlockSpec((gpb, 1, d), _mod_map(gpb, blocks_per_mod, 5)),
                  pl.BlockSpec((1, 1, d), lambda i: (0, 0, 0))],
        out_specs=xspec,
        out_shape=jax.ShapeDtypeStruct((ng, r, d), F32),
        compiler_params=_params("arbitrary"),
        name="final_norm",
    )(x1_3, peer_out, mod3, norm_f_w.reshape(1, 1, d))


def _pack_w_in(w_in):
    q_end = 3 * SB_WIDTH + SSM_INNER + CONV_DIM
    dt_w = jnp.pad(w_in[:, q_end:q_end + SSM_HEADS], ((0, 0), (0, LANES - SSM_HEADS)))
    return jnp.concatenate([w_in[:, :q_end], dt_w, w_in[:, q_end + SSM_HEADS:]], axis=1).astype(BF16)


def _pack_peer(peer_w_query, peer_keys1, peer_keys2):
    d = peer_w_query.shape[0]
    wq = peer_w_query.reshape(d, PEER_HEADS, 2, PEER_HALF)
    wq_t = jnp.transpose(wq, (2, 1, 3, 0)).reshape(2 * PEER_HEADS * PEER_HALF, d).astype(BF16)
    eye = jnp.eye(PEER_HEADS, dtype=F32)

    def block_diag(keys):
        return jnp.einsum("kd,hg->khgd", keys, eye).reshape(PEER_KEYS * PEER_HEADS, PEER_HEADS * PEER_HALF).astype(BF16)

    return wq_t, block_diag(peer_keys1), block_diag(peer_keys2)


def _layer_tail(o_attn, y_ssm, proj, x3, mod3, lw, gpb, bpm):
    x1_3, h2 = _merge(o_attn, y_ssm, proj["ga"], proj["gs"], x3, mod3, lw["norm2_w"], lw["wa"], lw["ws"], lw["wo"],
                      gpb, bpm)
    th, p, s2t, r = _route(h2, lw["wq_t"], lw["kbd1"], lw["kbd2"])
    peer_out = _peer_dense(h2, lw["u_b"], lw["vt_b"], th, p, s2t, r)
    return _final(x1_3, peer_out, mod3, lw["norm_f_w"], gpb, bpm)


def _proj_dict(outs):
    names = ("q", "k", "v", "kb", "vb", "z", "xbc", "dt", "ga", "gs")
    return dict(zip(names, outs))


INPROJ_TM = 256


def kernel(x_prompt, x_sample, cache_k, cache_v, state_conv, state_ssm, page_table, c_prompt, c_sample, norm1_w,
           norm2_w, w_ada, b_ada, w_in, sb_bias, conv_w, conv_b, dt_bias, a_log, d_skip, ssm_norm_w, w_br_attn,
           w_br_ssm, w_out, peer_w_query, peer_keys1, peer_keys2, peer_u, peer_v, norm_f_w):
    batch, seq, d = x_prompt.shape
    n_seq, dec_seq, _ = x_sample.shape
    assert w_ada.shape[0] == 1, "single layer"
    lw = {"norm2_w": norm2_w[0], "wa": w_br_attn[0].astype(BF16), "ws": w_br_ssm[0].astype(BF16),
          "wo": w_out[0].astype(BF16), "norm_f_w": norm_f_w,
          "u_b": peer_u[0].astype(BF16), "vt_b": peer_v[0].T.astype(BF16)}
    lw["wq_t"], lw["kbd1"], lw["kbd2"] = _pack_peer(peer_w_query[0], peer_keys1[0], peer_keys2[0])
    w_packed = _pack_w_in(w_in[0])
    consts = _ssd_consts(conv_w[0], conv_b[0], dt_bias[0], a_log[0], d_skip[0], ssm_norm_w[0])

    mod = _adaln(jnp.concatenate([c_prompt, c_sample], axis=0), w_ada[0], b_ada[0])
    mod_p = mod[:batch].reshape(batch, 1, 6 * d)
    mod_s = mod[batch:].reshape(n_seq, 1, 6 * d)

    tm = INPROJ_TM
    bpm = seq // tm
    xp3 = x_prompt.reshape(batch * bpm, tm, d)
    pp = _proj_dict(_inproj(xp3, mod_p, norm1_w[0], w_packed, 1, bpm))
    o_attn_p = _attn_prompt(pp["q"], pp["kb"], pp["vb"], sb_bias[0], batch, seq)
    y_ssm_p, tail_p, st_p = _ssd_prompt(pp["xbc"], pp["z"], pp["dt"], consts, batch, seq)
    y_prompt = _layer_tail(o_attn_p, y_ssm_p, pp, xp3, mod_p, lw, 1, bpm).reshape(batch, seq, d)

    gpb = tm // dec_seq
    ps = _proj_dict(_inproj(x_sample, mod_s, norm1_w[0], w_packed, gpb, 1))
    n_pool = cache_k.shape[1]
    o_attn_s = _attn_sample(ps["q"], ps["k"], ps["v"], cache_k[0].reshape(n_pool, PAGE, SB_WIDTH),
                            cache_v[0].reshape(n_pool, PAGE, SB_WIDTH), page_table, sb_bias[0], dec_seq)
    prev = jnp.pad(state_conv[0], ((0, 0), (dec_seq - (CONV_W - 1), 0), (0, 0))).reshape(n_seq * dec_seq, CONV_DIM)
    st_in = state_ssm[0].reshape(n_seq, SSM_PAIRS, 2 * SSM_P, SSM_N).transpose(0, 1, 3, 2)
    y_ssm_s, st_s = _ssd_sample(ps["xbc"], prev, ps["z"], ps["dt"], st_in, consts, dec_seq)
    y_sample = _layer_tail(o_attn_s, y_ssm_s, ps, x_sample, mod_s, lw, gpb, 1)

    def state_out(st):
        b = st.shape[0]
        return st.transpose(0, 1, 3, 2).reshape(1, b, SSM_HEADS, SSM_P, SSM_N)

    k_prompt = pp["k"].reshape(1, batch, seq, SB_HEADS, SB_HEAD_DIM)
    v_prompt = pp["v"].reshape(1, batch, seq, SB_HEADS, SB_HEAD_DIM)
    conv_prompt = tail_p[:, SUBLANES - (CONV_W - 1):, :][None]
    k_sample = ps["k"].reshape(1, n_seq, dec_seq, SB_HEADS, SB_HEAD_DIM)
    v_sample = ps["v"].reshape(1, n_seq, dec_seq, SB_HEADS, SB_HEAD_DIM)
    conv_sample = ps["xbc"].reshape(n_seq, dec_seq, CONV_DIM)[:, dec_seq - (CONV_W - 1):, :][None]
    return (y_prompt, y_sample, k_prompt, v_prompt, conv_prompt, state_out(st_p), k_sample, v_sample, conv_sample,
            state_out(st_s))
```

```python
import functools
import math

import jax
import jax.numpy as jnp
from jax import lax
from jax.experimental import pallas as pl
from jax.experimental.pallas import tpu as pltpu

F32 = jnp.float32
BF16 = jnp.bfloat16

D_MODEL = 1024
EPS = 1e-6
SB_HEADS = 8
SB_HEAD_DIM = 64
SB_WIDTH = SB_HEADS * SB_HEAD_DIM
PAGE = 128
SSM_HEADS = 16
SSM_P = 64
SSM_GROUPS = 2
SSM_N = 128
SSM_INNER = SSM_HEADS * SSM_P
SSM_PAIRS = SSM_HEADS // 2
CONV_W = 4
CONV_DIM = SSM_INNER + 2 * SSM_GROUPS * SSM_N
SSM_CHUNK = 128
PEER_HEADS = 8
PEER_KEYS = 128
PEER_HALF = 128
PEER_TOPK = 16

LANES = 128
SUBLANES = 8
VMEM_LIMIT = 56 * 1024 * 1024

_PROJ_GROUPS = (("q", SB_WIDTH), ("k", SB_WIDTH), ("v", SB_WIDTH), ("z", SSM_INNER), ("xbc", CONV_DIM),
                ("dt", LANES), ("ga", D_MODEL), ("gs", D_MODEL))
_PROJ_COLS = sum(w for _, w in _PROJ_GROUPS)


def _params(*sem):
    return pltpu.CompilerParams(dimension_semantics=sem, vmem_limit_bytes=VMEM_LIMIT)


def _dot(a, b):
    return jnp.dot(a, b, preferred_element_type=F32)


def _dot_nt(a, b):
    return lax.dot_general(a, b, (((1,), (1,)), ((), ())), preferred_element_type=F32)


def _split3(x):
    hi = x.astype(BF16)
    r = x - hi.astype(F32)
    mid = r.astype(BF16)
    lo = (r - mid.astype(F32)).astype(BF16)
    return hi, mid, lo


def _dot3_l(x, m):
    hi, mid, lo = _split3(x)
    return _dot(hi, m) + _dot(mid, m) + _dot(lo, m)


def _dot3_r(m, x):
    hi, mid, lo = _split3(x)
    return _dot(m, hi) + _dot(m, mid) + _dot(m, lo)


def _softplus(x):
    return jnp.maximum(x, 0.0) + jnp.log1p(jnp.exp(-jnp.abs(x)))


def _silu(x):
    return x * jax.nn.sigmoid(x)


def _adaln_kernel(c_ref, w_ref, b_ref, o_ref):
    s = _silu(c_ref[...]).astype(BF16)
    o_ref[...] = _dot(s, w_ref[...].astype(BF16)) + b_ref[...]


def _adaln(c_all, w_ada, b_ada):
    m, n = c_all.shape[0], w_ada.shape[1]
    tn = 1024
    return pl.pallas_call(
        _adaln_kernel,
        grid=(n // tn,),
        in_specs=[pl.BlockSpec((m, D_MODEL), lambda j: (0, 0)),
                  pl.BlockSpec((D_MODEL, tn), lambda j: (0, j)),
                  pl.BlockSpec((1, tn), lambda j: (0, j))],
        out_specs=pl.BlockSpec((m, tn), lambda j: (0, j)),
        out_shape=jax.ShapeDtypeStruct((m, n), F32),
        compiler_params=_params("arbitrary"),
        name="adaln",
    )(c_all, w_ada, b_ada.reshape(1, n))


def _inproj_kernel(x_ref, sh_ref, sc_ref, nw_ref, w_ref, q_ref, k_ref, v_ref, kb_ref, vb_ref, z_ref, xbc_ref,
                   dt_ref, ga_ref, gs_ref):
    x = x_ref[...]
    g, r, d = x.shape
    y = x * lax.rsqrt(jnp.mean(x * x, axis=-1, keepdims=True) + EPS) * nw_ref[...]
    h = y * (1.0 + sc_ref[...]) + sh_ref[...]
    hb = h.reshape(g * r, d).astype(BF16)
    outs = {"q": q_ref, "k": k_ref, "v": v_ref, "z": z_ref, "xbc": xbc_ref, "dt": dt_ref, "ga": ga_ref, "gs": gs_ref}
    col = 0
    for name, width in _PROJ_GROUPS:
        res = _dot(hb, w_ref[:, col:col + width])
        col += width
        if name == "q":
            res = res * (SB_HEAD_DIM ** -0.5)
        outs[name][...] = res.astype(outs[name].dtype)
        if name == "k":
            kb_ref[...] = res.astype(BF16)
        if name == "v":
            vb_ref[...] = res.astype(BF16)


def _inproj(x3, mod3, norm_w, w_packed, groups_per_block, blocks_per_mod):
    ng, r, d = x3.shape
    gpb = groups_per_block
    nblk = ng // gpb
    tm = gpb * r
    t = ng * r
    if gpb == 1:
        mod_map = lambda col: (lambda i: (i // blocks_per_mod, 0, col))
    else:
        mod_map = lambda col: (lambda i: (i, 0, col))
    widths = dict(_PROJ_GROUPS)
    out_shapes = [
        jax.ShapeDtypeStruct((t, widths["q"]), F32), jax.ShapeDtypeStruct((t, widths["k"]), F32),
        jax.ShapeDtypeStruct((t, widths["v"]), F32), jax.ShapeDtypeStruct((t, widths["k"]), BF16),
        jax.ShapeDtypeStruct((t, widths["v"]), BF16), jax.ShapeDtypeStruct((t, widths["z"]), F32),
        jax.ShapeDtypeStruct((t, widths["xbc"]), F32), jax.ShapeDtypeStruct((t, widths["dt"]), F32),
        jax.ShapeDtypeStruct((t, widths["ga"]), F32), jax.ShapeDtypeStruct((t, widths["gs"]), F32)]
    out_specs = [pl.BlockSpec((tm, s.shape[1]), lambda i: (i, 0)) for s in out_shapes]
    return pl.pallas_call(
        _inproj_kernel,
        grid=(nblk,),
        in_specs=[pl.BlockSpec((gpb, r, d), lambda i: (i, 0, 0)),
                  pl.BlockSpec((gpb, 1, d), mod_map(0)),
                  pl.BlockSpec((gpb, 1, d), mod_map(1)),
                  pl.BlockSpec((1, 1, d), lambda i: (0, 0, 0)),
                  pl.BlockSpec((d, _PROJ_COLS), lambda i: (0, 0))],
        out_specs=out_specs,
        out_shape=out_shapes,
        compiler_params=_params("arbitrary"),
        name="inproj",
    )(x3, mod3, mod3, norm_w.reshape(1, 1, d), w_packed)


ATT_TQ = 256
ATT_HEADS = 4


def _stick_tile(z, mask, tri, cum):
    sp = jnp.log(1.0 + jnp.exp(-jnp.abs(z)))
    lsz = jnp.minimum(z, 0.0) - sp
    lk = jnp.where(mask, -jnp.maximum(z, 0.0) - sp, 0.0)
    hi = lk.astype(BF16)
    lo = (lk - hi.astype(F32)).astype(BF16)
    incl = _dot(hi, tri) + _dot(lo, tri)
    after = incl - lk + cum
    w = jnp.where(mask, jnp.exp(lsz + after), 0.0)
    return w, cum + incl[:, 0:1]


def _attn_prompt_kernel(bias_ref, q_ref, k_ref, v_ref, tri_ref, o_ref):
    hg = pl.program_id(1)
    qi = pl.program_id(2)
    tq = ATT_TQ
    width = ATT_HEADS * SB_HEAD_DIM
    q = q_ref[...].astype(BF16)
    lane_head = lax.broadcasted_iota(jnp.int32, (tq, width), 1) // SB_HEAD_DIM
    rel = lax.broadcasted_iota(jnp.int32, (tq, tq), 0) - lax.broadcasted_iota(jnp.int32, (tq, tq), 1)
    tri = tri_ref[...]
    qhs = [jnp.where(lane_head == hh, q, jnp.zeros_like(q)) for hh in range(ATT_HEADS)]
    biases = [bias_ref[hg * ATT_HEADS + hh] for hh in range(ATT_HEADS)]

    def body(step, carry):
        off = pl.multiple_of((qi - step) * tq, tq)
        kblk = k_ref[pl.ds(off, tq), :]
        vblk = v_ref[pl.ds(off, tq), :]
        mask = (rel + step * tq) > 0
        out = []
        for hh in range(ATT_HEADS):
            acc, cum = carry[hh]
            z = _dot_nt(qhs[hh], kblk) + biases[hh]
            w, cum = _stick_tile(z, mask, tri, cum)
            out.append((acc + _dot(w.astype(BF16), vblk), cum))
        return tuple(out)

    init = tuple((jnp.zeros((tq, width), F32), jnp.zeros((tq, 1), F32)) for _ in range(ATT_HEADS))
    res = lax.fori_loop(0, qi + 1, body, init)
    o = res[0][0]
    for hh in range(1, ATT_HEADS):
        o = jnp.where(lane_head == hh, res[hh][0], o)
    o_ref[...] = o.astype(o_ref.dtype)


def _tri_incl(n):
    r = lax.broadcasted_iota(jnp.int32, (n, n), 0)
    c = lax.broadcasted_iota(jnp.int32, (n, n), 1)
    return (r >= c).astype(BF16)


def _attn_prompt(q, kb, vb, sb_bias, batch, seq):
    tq = ATT_TQ
    nq = seq // tq
    width = ATT_HEADS * SB_HEAD_DIM
    return pl.pallas_call(
        _attn_prompt_kernel,
        grid=(batch, SB_HEADS // ATT_HEADS, nq),
        in_specs=[pl.BlockSpec(memory_space=pltpu.SMEM),
                  pl.BlockSpec((tq, width), lambda b, h, i: (b * nq + i, h)),
                  pl.BlockSpec((seq, width), lambda b, h, i: (b, h)),
                  pl.BlockSpec((seq, width), lambda b, h, i: (b, h)),
                  pl.BlockSpec((tq, tq), lambda b, h, i: (0, 0))],
        out_specs=pl.BlockSpec((tq, width), lambda b, h, i: (b * nq + i, h)),
        out_shape=jax.ShapeDtypeStruct((batch * seq, SB_WIDTH), BF16),
        compiler_params=_params("arbitrary", "arbitrary", "arbitrary"),
        name="attn_prompt",
    )(sb_bias, q, kb, vb, _tri_incl(tq))


def _attn_sample_kernel(n_pages, dec_seq, pt_ref, bias_ref, q_ref, kn_ref, vn_ref, tri_ref, *refs):
    k_pages = refs[:n_pages]
    v_pages = refs[n_pages:2 * n_pages]
    o_ref = refs[2 * n_pages]
    rows = SB_HEADS * dec_seq
    q = q_ref[...]
    qt = jnp.concatenate([q] * SB_HEADS, axis=0)
    row_head = lax.broadcasted_iota(jnp.int32, (rows, SB_WIDTH), 0) // dec_seq
    lane_head = lax.broadcasted_iota(jnp.int32, (rows, SB_WIDTH), 1) // SB_HEAD_DIM
    qbd = jnp.where(row_head == lane_head, qt, 0.0).astype(BF16)
    bias = bias_ref[...]
    tri = tri_ref[...]
    r_i = lax.broadcasted_iota(jnp.int32, (rows, PAGE), 0) % dec_seq
    c_i = lax.broadcasted_iota(jnp.int32, (rows, PAGE), 1)
    all_valid = c_i >= 0
    new_valid = c_i < r_i

    def pad_rows(a):
        return jnp.concatenate([a, jnp.zeros((PAGE - dec_seq, SB_WIDTH), a.dtype)], axis=0)

    acc = jnp.zeros((rows, SB_WIDTH), F32)
    cum = jnp.zeros((rows, 1), F32)
    blocks = [(lambda: pad_rows(kn_ref[...]), lambda: pad_rows(vn_ref[...]), new_valid)]
    for p in range(n_pages - 1, -1, -1):
        blocks.append((lambda p=p: k_pages[p][0], lambda p=p: v_pages[p][0], all_valid))
    for load_k, load_v, mask in blocks:
        kblk = load_k()
        vblk = load_v()
        z = _dot_nt(qbd, kblk.astype(BF16)) + bias
        w, cum = _stick_tile(z, mask, tri, cum)
        acc = acc + _dot(w.astype(BF16), vblk.astype(BF16))
    out = jnp.zeros((dec_seq, SB_WIDTH), F32)
    lane_head8 = lax.broadcasted_iota(jnp.int32, (dec_seq, SB_WIDTH), 1) // SB_HEAD_DIM
    for h in range(SB_HEADS):
        out = out + jnp.where(lane_head8 == h, acc[h * dec_seq:(h + 1) * dec_seq, :], 0.0)
    o_ref[...] = out.astype(o_ref.dtype)


def _attn_sample(q, k_new, v_new, cache_k, cache_v, page_table, sb_bias, dec_seq):
    n_seq, n_pages = page_table.shape
    rows = SB_HEADS * dec_seq
    bias_rows = jnp.broadcast_to(jnp.repeat(sb_bias, dec_seq)[:, None], (rows, PAGE))

    def page_spec(p):
        return pl.BlockSpec((1, PAGE, SB_WIDTH), lambda s, pt: (pt[s, p], 0, 0))

    tok_spec = pl.BlockSpec((dec_seq, SB_WIDTH), lambda s, pt: (s, 0))
    grid_spec = pltpu.PrefetchScalarGridSpec(
        num_scalar_prefetch=1,
        grid=(n_seq,),
        in_specs=[pl.BlockSpec((rows, PAGE), lambda s, pt: (0, 0)), tok_spec, tok_spec, tok_spec,
                  pl.BlockSpec((PAGE, PAGE), lambda s, pt: (0, 0))]
                 + [page_spec(p) for p in range(n_pages)] + [page_spec(p) for p in range(n_pages)],
        out_specs=tok_spec,
    )
    return pl.pallas_call(
        functools.partial(_attn_sample_kernel, n_pages, dec_seq),
        grid_spec=grid_spec,
        out_shape=jax.ShapeDtypeStruct((n_seq * dec_seq, SB_WIDTH), F32),
        compiler_params=_params("arbitrary"),
        name="attn_sample",
    )(page_table, bias_rows, q, k_new, v_new, _tri_incl(PAGE), *([cache_k] * n_pages), *([cache_v] * n_pages))


def _expand_matrix(reps):
    r = lax.broadcasted_iota(jnp.int32, (LANES, SSM_HEADS * reps), 0)
    c = lax.broadcasted_iota(jnp.int32, (LANES, SSM_HEADS * reps), 1)
    return (c // reps == r).astype(BF16)


def _ssd_core(act, dt_raw, z, valid, tri, e1, e2, dtb, a_row, dsk, nw, state_in, c_masks):
    xs = act[:, :SSM_INNER]
    bm = act[:, SSM_INNER:SSM_INNER + SSM_GROUPS * SSM_N]
    cm = act[:, SSM_INNER + SSM_GROUPS * SSM_N:]
    dt = _softplus(dt_raw + dtb)
    dta = dt * a_row
    acs = _dot3_r(tri, dta)
    acs_t = acs.T
    acs_e2 = _dot3_l(acs, e2)
    dt_e = _dot3_l(dt, e1)
    eacs_e = _dot3_l(jnp.exp(acs), e1)
    n_slots = len(state_in)
    dend_e = []
    cdec_e = []
    for s in range(n_slots):
        if c_masks[s] is None:
            last = acs[SSM_CHUNK - 1:SSM_CHUNK, :]
        else:
            last = jnp.min(jnp.where(c_masks[s], acs, jnp.inf), axis=0, keepdims=True)
        de = jnp.exp(last - acs)
        if c_masks[s] is not None:
            de = jnp.where(c_masks[s], de, 0.0)
        dend_e.append(_dot3_l(de, e1))
        cdec_e.append(_dot3_l(jnp.broadcast_to(jnp.exp(last), (SUBLANES, LANES)), e1)[0:1, :])
    bmb = [bm[:, g * SSM_N:(g + 1) * SSM_N] for g in range(SSM_GROUPS)]
    cmb = [cm[:, g * SSM_N:(g + 1) * SSM_N].astype(BF16) for g in range(SSM_GROUPS)]
    cb = [_dot_nt(cmb[g], bmb[g].astype(BF16)) for g in range(SSM_GROUPS)]
    bt = [bmb[g].T.astype(BF16) for g in range(SSM_GROUPS)]
    lane = lax.broadcasted_iota(jnp.int32, (SSM_CHUNK, LANES), 1)
    y_pairs = []
    new_states = [[None] * SSM_PAIRS for _ in range(n_slots)]
    for k in range(SSM_PAIRS):
        g = (2 * k) // (SSM_HEADS // SSM_GROUPS)
        sl = slice(k * LANES, (k + 1) * LANES)
        xdt = xs[:, sl] * dt_e[:, sl]
        xdt_b = xdt.astype(BF16)
        parts = []
        for hh in range(2):
            h = 2 * k + hh
            seg = acs_e2[:, h * LANES:(h + 1) * LANES] - acs_t[h:h + 1, :]
            decay = jnp.where(valid, jnp.exp(jnp.where(valid, seg, 0.0)), 0.0)
            parts.append(_dot((cb[g] * decay).astype(BF16), xdt_b))
        y = jnp.where(lane < SSM_P, parts[0], parts[1])
        y_off = None
        for s in range(n_slots):
            st = state_in[s][k]
            c_s = cmb[g] if c_masks[s] is None else jnp.where(c_masks[s], cmb[g], jnp.zeros_like(cmb[g]))
            contrib = _dot(c_s, st.astype(BF16))
            y_off = contrib if y_off is None else y_off + contrib
            xdd = (xdt * dend_e[s][:, sl]).astype(BF16)
            new_states[s][k] = st * cdec_e[s][:, sl] + _dot(bt[g], xdd)
        y = y + y_off * eacs_e[:, sl] + dsk[:, sl] * xs[:, sl]
        y_pairs.append(y)
    y = jnp.concatenate(y_pairs, axis=1) * _silu(z)
    gsz = SSM_INNER // SSM_GROUPS
    outs = []
    for g in range(SSM_GROUPS):
        yg = y[:, g * gsz:(g + 1) * gsz]
        outs.append(yg * lax.rsqrt(jnp.mean(yg * yg, axis=-1, keepdims=True) + EPS) * nw[:, g * gsz:(g + 1) * gsz])
    return jnp.concatenate(outs, axis=1), new_states


def _ssd_prompt_kernel(xbc_ref, z_ref, dt_ref, cw_ref, cbias_ref, dtb_ref, a_ref, dsk_ref, nw_ref, tri_ref, e1_ref,
                       e2_ref, y_ref, tail_ref, st_ref, xs_scr, st_scr):
    c = pl.program_id(1)

    @pl.when(c == 0)
    def _():
        xs_scr[0:SUBLANES, :] = jnp.zeros((SUBLANES, CONV_DIM), F32)
        st_scr[...] = jnp.zeros_like(st_scr)

    xs_scr[SUBLANES:SUBLANES + SSM_CHUNK, :] = xbc_ref[...]
    conv = cbias_ref[...]
    for i in range(CONV_W):
        off = SUBLANES - (CONV_W - 1) + i
        conv = conv + xs_scr[off:off + SSM_CHUNK, :] * cw_ref[i:i + 1, :]
    tail = xs_scr[SSM_CHUNK:SSM_CHUNK + SUBLANES, :]
    xs_scr[0:SUBLANES, :] = tail
    tail_ref[0] = tail
    act = _silu(conv)
    r = lax.broadcasted_iota(jnp.int32, (SSM_CHUNK, SSM_CHUNK), 0)
    cc = lax.broadcasted_iota(jnp.int32, (SSM_CHUNK, SSM_CHUNK), 1)
    valid = r >= cc
    state_in = [[st_scr[k] for k in range(SSM_PAIRS)]]
    y, new_states = _ssd_core(act, dt_ref[...], z_ref[...], valid, tri_ref[...], e1_ref[...], e2_ref[...],
                              dtb_ref[...], a_ref[...], dsk_ref[...], nw_ref[...], state_in, [None])
    for k in range(SSM_PAIRS):
        st_scr[k] = new_states[0][k]
        st_ref[0, k] = new_states[0][k]
    y_ref[...] = y.astype(y_ref.dtype)


def _ssd_consts(conv_w, conv_b, dt_bias, a_log, d_skip, ssm_norm_w):
    pad = LANES - SSM_HEADS
    cw = jnp.concatenate([conv_w, jnp.zeros((SUBLANES - CONV_W, CONV_DIM), F32)], axis=0)
    dtb = jnp.pad(dt_bias, (0, pad)).reshape(1, LANES)
    a_row = jnp.pad(-jnp.exp(a_log), (0, pad)).reshape(1, LANES)
    dsk = jnp.repeat(d_skip, SSM_P).reshape(1, SSM_INNER)
    return cw, conv_b.reshape(1, CONV_DIM), dtb, a_row, dsk, ssm_norm_w.reshape(1, SSM_INNER)


def _const_spec(shape, ngrid):
    zeros = (0,) * len(shape)
    if ngrid == 1:
        return pl.BlockSpec(shape, lambda i: zeros)
    return pl.BlockSpec(shape, lambda i, j: zeros)


def _ssd_prompt(xbc, z, dt, consts, batch, seq):
    nc = seq // SSM_CHUNK
    cw, cbias, dtb, a_row, dsk, nw = consts
    tok = lambda w: pl.BlockSpec((SSM_CHUNK, w), lambda b, c: (b * nc + c, 0))
    tri = _tri_incl(SSM_CHUNK)
    return pl.pallas_call(
        _ssd_prompt_kernel,
        grid=(batch, nc),
        in_specs=[tok(CONV_DIM), tok(SSM_INNER), tok(LANES),
                  _const_spec(cw.shape, 2), _const_spec(cbias.shape, 2), _const_spec(dtb.shape, 2),
                  _const_spec(a_row.shape, 2), _const_spec(dsk.shape, 2), _const_spec(nw.shape, 2),
                  _const_spec((SSM_CHUNK, SSM_CHUNK), 2), _const_spec((LANES, SSM_INNER), 2),
                  _const_spec((LANES, SSM_HEADS * LANES), 2)],
        out_specs=[tok(SSM_INNER),
                   pl.BlockSpec((1, SUBLANES, CONV_DIM), lambda b, c: (b, 0, 0)),
                   pl.BlockSpec((1, SSM_PAIRS, SSM_N, LANES), lambda b, c: (b, 0, 0, 0))],
        out_shape=[jax.ShapeDtypeStruct((batch * seq, SSM_INNER), BF16),
                   jax.ShapeDtypeStruct((batch, SUBLANES, CONV_DIM), F32),
                   jax.ShapeDtypeStruct((batch, SSM_PAIRS, SSM_N, LANES), F32)],
        scratch_shapes=[pltpu.VMEM((SSM_CHUNK + SUBLANES, CONV_DIM), F32),
                        pltpu.VMEM((SSM_PAIRS, SSM_N, LANES), F32)],
        compiler_params=_params("arbitrary", "arbitrary"),
        name="ssd_prompt",
    )(xbc, z, dt, cw, cbias, dtb, a_row, dsk, nw, tri, _expand_matrix(SSM_P), _expand_matrix(LANES))


def _ssd_sample_kernel(dec_seq, xbc_ref, prev_ref, z_ref, dt_ref, st_in_ref, cw_ref, cbias_ref, dtb_ref, a_ref,
                       dsk_ref, nw_ref, tri_ref, e1_ref, e2_ref, y_ref, st_ref):
    n_slots = SSM_CHUNK // dec_seq
    cur = xbc_ref[...]
    prev = prev_ref[...]
    pos = lax.broadcasted_iota(jnp.int32, (SSM_CHUNK, CONV_DIM), 0) % dec_seq
    conv = cbias_ref[...] + cur * cw_ref[CONV_W - 1:CONV_W, :]
    for k in range(1, CONV_W):
        shifted = jnp.where(pos < k, pltpu.roll(prev, SSM_CHUNK - dec_seq + k, axis=0), pltpu.roll(cur, k, axis=0))
        conv = conv + shifted * cw_ref[CONV_W - 1 - k:CONV_W - k, :]
    act = _silu(conv)
    r = lax.broadcasted_iota(jnp.int32, (SSM_CHUNK, SSM_CHUNK), 0)
    cc = lax.broadcasted_iota(jnp.int32, (SSM_CHUNK, SSM_CHUNK), 1)
    valid = (r >= cc) & ((r // dec_seq) == (cc // dec_seq))
    row_seq = lax.broadcasted_iota(jnp.int32, (SSM_CHUNK, LANES), 0) // dec_seq
    c_masks = [row_seq == s for s in range(n_slots)]
    state_in = [[st_in_ref[s, k] for k in range(SSM_PAIRS)] for s in range(n_slots)]
    y, new_states = _ssd_core(act, dt_ref[...], z_ref[...], valid, tri_ref[...], e1_ref[...], e2_ref[...],
                              dtb_ref[...], a_ref[...], dsk_ref[...], nw_ref[...], state_in, c_masks)
    for s in range(n_slots):
        for k in range(SSM_PAIRS):
            st_ref[s, k] = new_states[s][k]
    y_ref[...] = y.astype(y_ref.dtype)


def _ssd_sample(xbc, prev, z, dt, state_t, consts, dec_seq):
    n_tok = xbc.shape[0]
    n_slots = SSM_CHUNK // dec_seq
    nblk = n_tok // SSM_CHUNK
    cw, cbias, dtb, a_row, dsk, nw = consts
    tok = lambda w: pl.BlockSpec((SSM_CHUNK, w), lambda i: (i, 0))
    st_spec = pl.BlockSpec((n_slots, SSM_PAIRS, SSM_N, LANES), lambda i: (i, 0, 0, 0))
    r = lax.broadcasted_iota(jnp.int32, (SSM_CHUNK, SSM_CHUNK), 0)
    c = lax.broadcasted_iota(jnp.int32, (SSM_CHUNK, SSM_CHUNK), 1)
    tri = ((r >= c) & ((r // dec_seq) == (c // dec_seq))).astype(BF16)
    return pl.pallas_call(
        functools.partial(_ssd_sample_kernel, dec_seq),
        grid=(nblk,),
        in_specs=[tok(CONV_DIM), tok(CONV_DIM), tok(SSM_INNER), tok(LANES), st_spec,
                  _const_spec(cw.shape, 1), _const_spec(cbias.shape, 1), _const_spec(dtb.shape, 1),
                  _const_spec(a_row.shape, 1), _const_spec(dsk.shape, 1), _const_spec(nw.shape, 1),
                  _const_spec((SSM_CHUNK, SSM_CHUNK), 1), _const_spec((LANES, SSM_INNER), 1),
                  _const_spec((LANES, SSM_HEADS * LANES), 1)],
        out_specs=[tok(SSM_INNER), st_spec],
        out_shape=[jax.ShapeDtypeStruct((n_tok, SSM_INNER), BF16),
                   jax.ShapeDtypeStruct(state_t.shape, F32)],
        compiler_params=_params("arbitrary"),
        name="ssd_sample",
    )(xbc, prev, z, dt, state_t, cw, cbias, dtb, a_row, dsk, nw, tri, _expand_matrix(SSM_P), _expand_matrix(LANES))


def _merge_kernel(o_ref, y_ref, ga_ref, gs_ref, x_ref, g1_ref, sh_ref, sc_ref, nw_ref, wa_ref, ws_ref, wo_ref,
                  x1_ref, h2_ref):
    merged = (jax.nn.sigmoid(ga_ref[...]) * _dot(o_ref[...].astype(BF16), wa_ref[...])
              + jax.nn.sigmoid(gs_ref[...]) * _dot(y_ref[...], ws_ref[...]))
    mix = _dot(merged.astype(BF16), wo_ref[...])
    x = x_ref[...]
    g, r, d = x.shape
    x1 = x + g1_ref[...] * mix.reshape(g, r, d)
    x1_ref[...] = x1
    y = x1 * lax.rsqrt(jnp.mean(x1 * x1, axis=-1, keepdims=True) + EPS) * nw_ref[...]
    h2 = y * (1.0 + sc_ref[...]) + sh_ref[...]
    h2_ref[...] = h2.reshape(g * r, d).astype(h2_ref.dtype)


def _mod_map(gpb, blocks_per_mod, col):
    if gpb == 1:
        return lambda i: (i // blocks_per_mod, 0, col)
    return lambda i: (i, 0, col)


def _merge(o_attn, y_ssm, ga, gs, x3, mod3, norm2_w, wa, ws, wo, groups_per_block, blocks_per_mod):
    ng, r, d = x3.shape
    gpb = groups_per_block
    nblk = ng // gpb
    tm = gpb * r
    t = ng * r
    tok = lambda w: pl.BlockSpec((tm, w), lambda i: (i, 0))
    xspec = pl.BlockSpec((gpb, r, d), lambda i: (i, 0, 0))
    mspec = lambda col: pl.BlockSpec((gpb, 1, d), _mod_map(gpb, blocks_per_mod, col))
    return pl.pallas_call(
        _merge_kernel,
        grid=(nblk,),
        in_specs=[tok(SB_WIDTH), tok(SSM_INNER), tok(d), tok(d), xspec, mspec(2), mspec(3), mspec(4),
                  pl.BlockSpec((1, 1, d), lambda i: (0, 0, 0)),
                  _const_spec(wa.shape, 1), _const_spec(ws.shape, 1), _const_spec(wo.shape, 1)],
        out_specs=[xspec, tok(d)],
        out_shape=[jax.ShapeDtypeStruct((ng, r, d), F32), jax.ShapeDtypeStruct((t, d), BF16)],
        compiler_params=_params("arbitrary"),
        name="merge",
    )(o_attn, y_ssm, ga, gs, x3, mod3, mod3, mod3, norm2_w.reshape(1, 1, d), wa, ws, wo)


def _batcher_pairs(n):
    pairs = []
    p = 1
    while p < n:
        k = p
        while k >= 1:
            for j in range(k % p, n - k, 2 * k):
                for i in range(min(k, n - j - k)):
                    if (i + j) // (2 * p) == (i + j + k) // (2 * p):
                        pairs.append((i + j, i + j + k))
            k //= 2
        p *= 2
    return pairs


_SORT16 = _batcher_pairs(PEER_TOPK)


def _sort_desc(vals):
    vals = list(vals)
    for i, j in _SORT16:
        hi = jnp.maximum(vals[i], vals[j])
        lo = jnp.minimum(vals[i], vals[j])
        vals[i], vals[j] = hi, lo
    return vals


def _merge_top(a, b, sort_result=True):
    n = PEER_TOPK
    t = [jnp.maximum(a[k], b[n - 1 - k]) for k in range(n)]
    if not sort_result:
        return t
    d = n // 2
    while d >= 1:
        for k in range(n):
            if (k // d) % 2 == 0:
                hi = jnp.maximum(t[k], t[k + d])
                lo = jnp.minimum(t[k], t[k + d])
                t[k], t[k + d] = hi, lo
        d //= 2
    return t


def _top16_desc(vals, need_sorted=True):
    groups = [_sort_desc(vals[i:i + PEER_TOPK]) for i in range(0, len(vals), PEER_TOPK)]
    while len(groups) > 1:
        nxt = []
        for i in range(0, len(groups) - 1, 2):
            last = len(groups) == 2
            nxt.append(_merge_top(groups[i], groups[i + 1], sort_result=need_sorted or not last))
        if len(groups) % 2:
            nxt.append(groups[-1])
        groups = nxt
    return groups[0]


ROUTE_TM = 256


def _route_kernel(h_ref, wq_ref, k1_ref, k2_ref, th_ref, p_ref, code_ref, r_ref, s1_scr, s2_scr):
    half = PEER_HEADS * PEER_HALF
    qt = _dot_nt(wq_ref[...], h_ref[...])
    s1 = _dot(k1_ref[...], qt[:half].astype(BF16))
    s2 = _dot(k2_ref[...], qt[half:].astype(BF16))
    for c in range(ROUTE_TM // LANES):
        s1_scr[c] = s1[:, c * LANES:(c + 1) * LANES]
        s2_scr[c] = s2[:, c * LANES:(c + 1) * LANES]
    staircase = [(a, b) for a in range(PEER_TOPK) for b in range(PEER_TOPK) if (a + 1) * (b + 1) <= PEER_TOPK]
    neg_inf = jnp.full((SUBLANES, LANES), -jnp.inf, F32)
    for c in range(ROUTE_TM // LANES):
        tl = pl.ds(c * LANES, LANES)
        row = lambda scr, i, c=c: scr[c, i * SUBLANES:(i + 1) * SUBLANES, :]
        top1 = _top16_desc([row(s1_scr, i) for i in range(PEER_KEYS)])
        top2 = _top16_desc([row(s2_scr, i) for i in range(PEER_KEYS)])
        cand = {(a, b): top1[a] + top2[b] for a, b in staircase}
        cvals = [cand[ab] for ab in staircase]
        cvals += [neg_inf] * (-len(cvals) % PEER_TOPK)
        tau = _top16_desc(cvals, need_sorted=False)
        thr = tau[0]
        for v in tau[1:]:
            thr = jnp.minimum(thr, v)
        cmax = cand[(0, 0)]
        zsum = jnp.zeros((SUBLANES, LANES), F32)
        never = jnp.full((SUBLANES, LANES), PEER_TOPK + 1.0, F32)
        need = [never] * PEER_TOPK
        for a, b in staircase:
            ok = cand[(a, b)] >= thr
            zsum = zsum + jnp.where(ok, jnp.exp(cand[(a, b)] - cmax), 0.0)
            need[a] = need[a] - jnp.where(ok, 1.0, 0.0)
        inv_z = 1.0 / zsum
        for i in range(PEER_KEYS):
            s = row(s1_scr, i)
            th = never
            for a in range(PEER_TOPK):
                th = jnp.where(s == top1[a], need[a], th)
            th_ref[i, :, tl] = th
            p_ref[i, :, tl] = jnp.exp(s - top1[0]) * inv_z
        for h in range(PEER_HEADS):
            s2h = s2_scr[c, pl.ds(h, PEER_KEYS, stride=SUBLANES), :]
            code = jnp.zeros((PEER_KEYS, LANES), F32)
            for b in range(PEER_TOPK):
                code = code + jnp.where(s2h >= top2[b][h:h + 1, :], 1.0, 0.0)
            tiles = (PEER_KEYS // BF16_ROWS, BF16_ROWS, LANES)
            code_ref[h, :, :, tl] = code.reshape(tiles).astype(BF16)
            r_ref[h, :, :, tl] = jnp.exp(s2h - top2[0][h:h + 1, :]).reshape(tiles).astype(BF16)


def _route(h2, wq_t, kbd1, kbd2):
    t = h2.shape[0]
    tm = ROUTE_TM
    half = PEER_HEADS * PEER_HALF
    kt = PEER_KEYS // BF16_ROWS
    return pl.pallas_call(
        _route_kernel,
        grid=(t // tm,),
        in_specs=[pl.BlockSpec((tm, D_MODEL), lambda i: (i, 0)),
                  _const_spec(wq_t.shape, 1), _const_spec(kbd1.shape, 1), _const_spec(kbd2.shape, 1)],
        out_specs=[pl.BlockSpec((PEER_KEYS, PEER_HEADS, tm), lambda i: (0, 0, i)),
                   pl.BlockSpec((PEER_KEYS, PEER_HEADS, tm), lambda i: (0, 0, i)),
                   pl.BlockSpec((PEER_HEADS, kt, BF16_ROWS, tm), lambda i: (0, 0, 0, i)),
                   pl.BlockSpec((PEER_HEADS, kt, BF16_ROWS, tm), lambda i: (0, 0, 0, i))],
        out_shape=[jax.ShapeDtypeStruct((PEER_KEYS, PEER_HEADS, t), F32),
                   jax.ShapeDtypeStruct((PEER_KEYS, PEER_HEADS, t), F32),
                   jax.ShapeDtypeStruct((PEER_HEADS, kt, BF16_ROWS, t), BF16),
                   jax.ShapeDtypeStruct((PEER_HEADS, kt, BF16_ROWS, t), BF16)],
        scratch_shapes=[pltpu.VMEM((tm // LANES, half, LANES), F32), pltpu.VMEM((tm // LANES, half, LANES), F32)],
        compiler_params=_params("arbitrary"),
        name="peer_route",
    )(h2, wq_t, kbd1, kbd2)


PEER_TM = 1024
PEER_KB = 8
PEER_SUB = 2
BF16_ROWS = 16


def _gelu_tanh(x):
    return 0.5 * x * (1.0 + jnp.tanh(math.sqrt(2.0 / math.pi) * (x + 0.044715 * (x * x * x))))


def _peer_kernel(h_ref, u_ref, vt_ref, th_ref, p_ref, code_ref, r_ref, o_ref, acc_ref):
    j = pl.program_id(1)
    tm = h_ref.shape[0]
    kt = PEER_KEYS // BF16_ROWS

    @pl.when(j == 0)
    def _():
        acc_ref[...] = jnp.zeros_like(acc_ref)

    zero = jnp.zeros((), BF16)
    h = h_ref[...]
    total = None
    for sb in range(PEER_KB // PEER_SUB):
        rows = slice(sb * PEER_SUB * PEER_KEYS, (sb + 1) * PEER_SUB * PEER_KEYS)
        act = _gelu_tanh(_dot_nt(u_ref[rows, :], h))
        act = act.reshape(PEER_SUB * kt, BF16_ROWS, tm).astype(BF16)
        parts = []
        for s in range(PEER_SUB):
            ii = sb * PEER_SUB + s
            w = None
            for hd in range(PEER_HEADS):
                need = jnp.broadcast_to(th_ref[ii, hd:hd + 1, :], (BF16_ROWS, tm)).astype(BF16)
                pr = jnp.broadcast_to(p_ref[ii, hd:hd + 1, :], (BF16_ROWS, tm)).astype(BF16)
                sel = jnp.where(code_ref[hd] >= need[None], r_ref[hd], zero) * pr[None]
                w = sel if w is None else w + sel
            parts.append(w * act[s * kt:(s + 1) * kt])
        a = jnp.concatenate(parts, axis=0).reshape(PEER_SUB * PEER_KEYS, tm)
        contrib = _dot(vt_ref[:, rows], a)
        total = contrib if total is None else total + contrib
    acc_ref[...] += total

    @pl.when(j == pl.num_programs(1) - 1)
    def _():
        o_ref[...] = acc_ref[...].T


def _peer_dense(h2, u_b, vt_b, th, p, code, r):
    t = h2.shape[0]
    tm = min(PEER_TM, t)
    eb = PEER_KB * PEER_KEYS
    nj = PEER_KEYS // PEER_KB
    kt = PEER_KEYS // BF16_ROWS
    mask_spec = pl.BlockSpec((PEER_HEADS, kt, BF16_ROWS, tm), lambda i, j: (0, 0, 0, i))
    return pl.pallas_call(
        _peer_kernel,
        grid=(t // tm, nj),
        in_specs=[pl.BlockSpec((tm, D_MODEL), lambda i, j: (i, 0)),
                  pl.BlockSpec((eb, D_MODEL), lambda i, j: (j, 0)),
                  pl.BlockSpec((D_MODEL, eb), lambda i, j: (0, j)),
                  pl.BlockSpec((PEER_KB, PEER_HEADS, tm), lambda i, j: (j, 0, i)),
                  pl.BlockSpec((PEER_KB, PEER_HEADS, tm), lambda i, j: (j, 0, i)),
                  mask_spec, mask_spec],
        out_specs=pl.BlockSpec((tm, D_MODEL), lambda i, j: (i, 0)),
        out_shape=jax.ShapeDtypeStruct((t, D_MODEL), F32),
        scratch_shapes=[pltpu.VMEM((D_MODEL, tm), F32)],
        compiler_params=_params("arbitrary", "arbitrary"),
        name="peer_dense",
    )(h2, u_b, vt_b, th, p, code, r)


def _final_kernel(x_ref, pe_ref, g2_ref, nw_ref, y_ref):
    x1 = x_ref[...]
    g, r, d = x1.shape
    x2 = x1 + g2_ref[...] * pe_ref[...].reshape(g, r, d)
    y_ref[...] = x2 * lax.rsqrt(jnp.mean(x2 * x2, axis=-1, keepdims=True) + EPS) * nw_ref[...]


def _final(x1_3, peer_out, mod3, norm_f_w, groups_per_block, blocks_per_mod):
    ng, r, d = x1_3.shape
    gpb = groups_per_block
    nblk = ng // gpb
    tm = gpb * r
    xspec = pl.BlockSpec((gpb, r, d), lambda i: (i, 0, 0))
    return pl.pallas_call(
        _final_kernel,
        grid=(nblk,),
        in_specs=[xspec, pl.BlockSpec((tm, d), lambda i: (i, 0)),
                  pl.BlockSpec((gpb, 1, d), _mod_map(gpb, blocks_per_mod, 5)),
                  pl.BlockSpec((1, 1, d), lambda i: (0, 0, 0))],
        out_specs=xspec,
        out_shape=jax.ShapeDtypeStruct((ng, r, d), F32),
        compiler_params=_params("arbitrary"),
        name="final_norm",
    )(x1_3, peer_out, mod3, norm_f_w.reshape(1, 1, d))


def _pack_w_in(w_in):
    q_end = 3 * SB_WIDTH + SSM_INNER + CONV_DIM
    dt_w = jnp.pad(w_in[:, q_end:q_end + SSM_HEADS], ((0, 0), (0, LANES - SSM_HEADS)))
    return jnp.concatenate([w_in[:, :q_end], dt_w, w_in[:, q_end + SSM_HEADS:]], axis=1).astype(BF16)


def _pack_peer(peer_w_query, peer_keys1, peer_keys2):
    d = peer_w_query.shape[0]
    wq = peer_w_query.reshape(d, PEER_HEADS, 2, PEER_HALF)
    wq_t = jnp.transpose(wq, (2, 1, 3, 0)).reshape(2 * PEER_HEADS * PEER_HALF, d).astype(BF16)
    eye = jnp.eye(PEER_HEADS, dtype=F32)

    def block_diag(keys):
        return jnp.einsum("kd,hg->khgd", keys, eye).reshape(PEER_KEYS * PEER_HEADS, PEER_HEADS * PEER_HALF).astype(BF16)

    return wq_t, block_diag(peer_keys1), block_diag(peer_keys2)


def _layer_tail(o_attn, y_ssm, proj, x3, mod3, lw, gpb, bpm):
    x1_3, h2 = _merge(o_attn, y_ssm, proj["ga"], proj["gs"], x3, mod3, lw["norm2_w"], lw["wa"], lw["ws"], lw["wo"],
                      gpb, bpm)
    th, p, s2t, r = _route(h2, lw["wq_t"], lw["kbd1"], lw["kbd2"])
    peer_out = _peer_dense(h2, lw["u_b"], lw["vt_b"], th, p, s2t, r)
    return _final(x1_3, peer_out, mod3, lw["norm_f_w"], gpb, bpm)


def _proj_dict(outs):
    names = ("q", "k", "v", "kb", "vb", "z", "xbc", "dt", "ga", "gs")
    return dict(zip(names, outs))


INPROJ_TM = 256


def kernel(x_prompt, x_sample, cache_k, cache_v, state_conv, state_ssm, page_table, c_prompt, c_sample, norm1_w,
           norm2_w, w_ada, b_ada, w_in, sb_bias, conv_w, conv_b, dt_bias, a_log, d_skip, ssm_norm_w, w_br_attn,
           w_br_ssm, w_out, peer_w_query, peer_keys1, peer_keys2, peer_u, peer_v, norm_f_w):
    batch, seq, d = x_prompt.shape
    n_seq, dec_seq, _ = x_sample.shape
    assert w_ada.shape[0] == 1, "single layer"
    lw = {"norm2_w": norm2_w[0], "wa": w_br_attn[0].astype(BF16), "ws": w_br_ssm[0].astype(BF16),
          "wo": w_out[0].astype(BF16), "norm_f_w": norm_f_w,
          "u_b": peer_u[0].astype(BF16), "vt_b": peer_v[0].T.astype(BF16)}
    lw["wq_t"], lw["kbd1"], lw["kbd2"] = _pack_peer(peer_w_query[0], peer_keys1[0], peer_keys2[0])
    w_packed = _pack_w_in(w_in[0])
    consts = _ssd_consts(conv_w[0], conv_b[0], dt_bias[0], a_log[0], d_skip[0], ssm_norm_w[0])

    mod = _adaln(jnp.concatenate([c_prompt, c_sample], axis=0), w_ada[0], b_ada[0])
    mod_p = mod[:batch].reshape(batch, 1, 6 * d)
    mod_s = mod[batch:].reshape(n_seq, 1, 6 * d)

    tm = INPROJ_TM
    bpm = seq // tm
    xp3 = x_prompt.reshape(batch * bpm, tm, d)
    pp = _proj_dict(_inproj(xp3, mod_p, norm1_w[0], w_packed, 1, bpm))
    o_attn_p = _attn_prompt(pp["q"], pp["kb"], pp["vb"], sb_bias[0], batch, seq)
    y_ssm_p, tail_p, st_p = _ssd_prompt(pp["xbc"], pp["z"], pp["dt"], consts, batch, seq)
    y_prompt = _layer_tail(o_attn_p, y_ssm_p, pp, xp3, mod_p, lw, 1, bpm).reshape(batch, seq, d)

    gpb = tm // dec_seq
    ps = _proj_dict(_inproj(x_sample, mod_s, norm1_w[0], w_packed, gpb, 1))
    n_pool = cache_k.shape[1]
    o_attn_s = _attn_sample(ps["q"], ps["k"], ps["v"], cache_k[0].reshape(n_pool, PAGE, SB_WIDTH),
                            cache_v[0].reshape(n_pool, PAGE, SB_WIDTH), page_table, sb_bias[0], dec_seq)
    prev = jnp.pad(state_conv[0], ((0, 0), (dec_seq - (CONV_W - 1), 0), (0, 0))).reshape(n_seq * dec_seq, CONV_DIM)
    st_in = state_ssm[0].reshape(n_seq, SSM_PAIRS, 2 * SSM_P, SSM_N).transpose(0, 1, 3, 2)
    y_ssm_s, st_s = _ssd_sample(ps["xbc"], prev, ps["z"], ps["dt"], st_in, consts, dec_seq)
    y_sample = _layer_tail(o_attn_s, y_ssm_s, ps, x_sample, mod_s, lw, gpb, 1)

    def state_out(st):
        b = st.shape[0]
        return st.transpose(0, 1, 3, 2).reshape(1, b, SSM_HEADS, SSM_P, SSM_N)

    k_prompt = pp["k"].reshape(1, batch, seq, SB_HEADS, SB_HEAD_DIM)
    v_prompt = pp["v"].reshape(1, batch, seq, SB_HEADS, SB_HEAD_DIM)
    conv_prompt = tail_p[:, SUBLANES - (CONV_W - 1):, :][None]
    k_sample = ps["k"].reshape(1, n_seq, dec_seq, SB_HEADS, SB_HEAD_DIM)
    v_sample = ps["v"].reshape(1, n_seq, dec_seq, SB_HEADS, SB_HEAD_DIM)
    conv_sample = ps["xbc"].reshape(n_seq, dec_seq, CONV_DIM)[:, dec_seq - (CONV_W - 1):, :][None]
    return (y_prompt, y_sample, k_prompt, v_prompt, conv_prompt, state_out(st_p), k_sample, v_sample, conv_sample,
            state_out(st_s))
```

```python
import functools
import math

import jax
import jax.numpy as jnp
from jax import lax
from jax.experimental import pallas as pl
from jax.experimental.pallas import tpu as pltpu

F32 = jnp.float32
BF16 = jnp.bfloat16

D_MODEL = 1024
EPS = 1e-6
SB_HEADS = 8
SB_HEAD_DIM = 64
SB_WIDTH = SB_HEADS * SB_HEAD_DIM
PAGE = 128
SSM_HEADS = 16
SSM_P = 64
SSM_GROUPS = 2
SSM_N = 128
SSM_INNER = SSM_HEADS * SSM_P
SSM_PAIRS = SSM_HEADS // 2
CONV_W = 4
CONV_DIM = SSM_INNER + 2 * SSM_GROUPS * SSM_N
SSM_CHUNK = 128
PEER_HEADS = 8
PEER_KEYS = 128
PEER_HALF = 128
PEER_TOPK = 16

LANES = 128
SUBLANES = 8
VMEM_LIMIT = 56 * 1024 * 1024

_PROJ_GROUPS = (("q", SB_WIDTH), ("k", SB_WIDTH), ("v", SB_WIDTH), ("z", SSM_INNER), ("xbc", CONV_DIM),
                ("dt", LANES), ("ga", D_MODEL), ("gs", D_MODEL))
_PROJ_COLS = sum(w for _, w in _PROJ_GROUPS)


def _params(*sem):
    return pltpu.CompilerParams(dimension_semantics=sem, vmem_limit_bytes=VMEM_LIMIT)


def _dot(a, b):
    return jnp.dot(a, b, preferred_element_type=F32)


def _dot_nt(a, b):
    return lax.dot_general(a, b, (((1,), (1,)), ((), ())), preferred_element_type=F32)


def _split3(x):
    hi = x.astype(BF16)
    r = x - hi.astype(F32)
    mid = r.astype(BF16)
    lo = (r - mid.astype(F32)).astype(BF16)
    return hi, mid, lo


def _dot3_l(x, m):
    hi, mid, lo = _split3(x)
    return _dot(hi, m) + _dot(mid, m) + _dot(lo, m)


def _dot3_r(m, x):
    hi, mid, lo = _split3(x)
    return _dot(m, hi) + _dot(m, mid) + _dot(m, lo)


def _softplus(x):
    return jnp.maximum(x, 0.0) + jnp.log1p(jnp.exp(-jnp.abs(x)))


def _silu(x):
    return x * jax.nn.sigmoid(x)


def _adaln_kernel(c_ref, w_ref, b_ref, o_ref):
    s = _silu(c_ref[...]).astype(BF16)
    o_ref[...] = _dot(s, w_ref[...].astype(BF16)) + b_ref[...]


def _adaln(c_all, w_ada, b_ada):
    m, n = c_all.shape[0], w_ada.shape[1]
    tn = 1024
    return pl.pallas_call(
        _adaln_kernel,
        grid=(n // tn,),
        in_specs=[pl.BlockSpec((m, D_MODEL), lambda j: (0, 0)),
                  pl.BlockSpec((D_MODEL, tn), lambda j: (0, j)),
                  pl.BlockSpec((1, tn), lambda j: (0, j))],
        out_specs=pl.BlockSpec((m, tn), lambda j: (0, j)),
        out_shape=jax.ShapeDtypeStruct((m, n), F32),
        compiler_params=_params("arbitrary"),
        name="adaln",
    )(c_all, w_ada, b_ada.reshape(1, n))


def _inproj_kernel(x_ref, sh_ref, sc_ref, nw_ref, w_ref, q_ref, k_ref, v_ref, kb_ref, vb_ref, z_ref, xbc_ref,
                   dt_ref, ga_ref, gs_ref):
    x = x_ref[...]
    g, r, d = x.shape
    y = x * lax.rsqrt(jnp.mean(x * x, axis=-1, keepdims=True) + EPS) * nw_ref[...]
    h = y * (1.0 + sc_ref[...]) + sh_ref[...]
    hb = h.reshape(g * r, d).astype(BF16)
    outs = {"q": q_ref, "k": k_ref, "v": v_ref, "z": z_ref, "xbc": xbc_ref, "dt": dt_ref, "ga": ga_ref, "gs": gs_ref}
    col = 0
    for name, width in _PROJ_GROUPS:
        res = _dot(hb, w_ref[:, col:col + width])
        col += width
        if name == "q":
            res = res * (SB_HEAD_DIM ** -0.5)
        outs[name][...] = res.astype(outs[name].dtype)
        if name == "k":
            kb_ref[...] = res.astype(BF16)
        if name == "v":
            vb_ref[...] = res.astype(BF16)


def _inproj(x3, mod3, norm_w, w_packed, groups_per_block, blocks_per_mod):
    ng, r, d = x3.shape
    gpb = groups_per_block
    nblk = ng // gpb
    tm = gpb * r
    t = ng * r
    if gpb == 1:
        mod_map = lambda col: (lambda i: (i // blocks_per_mod, 0, col))
    else:
        mod_map = lambda col: (lambda i: (i, 0, col))
    widths = dict(_PROJ_GROUPS)
    out_shapes = [
        jax.ShapeDtypeStruct((t, widths["q"]), F32), jax.ShapeDtypeStruct((t, widths["k"]), F32),
        jax.ShapeDtypeStruct((t, widths["v"]), F32), jax.ShapeDtypeStruct((t, widths["k"]), BF16),
        jax.ShapeDtypeStruct((t, widths["v"]), BF16), jax.ShapeDtypeStruct((t, widths["z"]), F32),
        jax.ShapeDtypeStruct((t, widths["xbc"]), F32), jax.ShapeDtypeStruct((t, widths["dt"]), F32),
        jax.ShapeDtypeStruct((t, widths["ga"]), F32), jax.ShapeDtypeStruct((t, widths["gs"]), F32)]
    out_specs = [pl.BlockSpec((tm, s.shape[1]), lambda i: (i, 0)) for s in out_shapes]
    return pl.pallas_call(
        _inproj_kernel,
        grid=(nblk,),
        in_specs=[pl.BlockSpec((gpb, r, d), lambda i: (i, 0, 0)),
                  pl.BlockSpec((gpb, 1, d), mod_map(0)),
                  pl.BlockSpec((gpb, 1, d), mod_map(1)),
                  pl.BlockSpec((1, 1, d), lambda i: (0, 0, 0)),
                  pl.BlockSpec((d, _PROJ_COLS), lambda i: (0, 0))],
        out_specs=out_specs,
        out_shape=out_shapes,
        compiler_params=_params("arbitrary"),
        name="inproj",
    )(x3, mod3, mod3, norm_w.reshape(1, 1, d), w_packed)


ATT_TQ = 512
ATT_TK = 256
ATT_HEADS = 4


def _stick_tile(z, mask, tri, cum):
    sp = jnp.maximum(z, 0.0) + jnp.log(1.0 + jnp.exp(-jnp.abs(z)))
    if mask is not None:
        sp = jnp.where(mask, sp, 0.0)
    hi = sp.astype(BF16)
    lo = (sp - hi.astype(F32)).astype(BF16)
    n = z.shape[0]
    both = _dot(jnp.concatenate([hi, lo], axis=0), tri)
    incl = both[:n] + both[n:]
    w = jnp.exp(z - incl - cum)
    if mask is not None:
        w = jnp.where(mask, w, 0.0)
    return w, cum + incl[:, 0:1]


def _attn_prompt_kernel(bias_ref, q_ref, k_ref, v_ref, tri_ref, o_ref):
    hg = pl.program_id(1)
    qi = pl.program_id(2)
    tq, tk = ATT_TQ, ATT_TK
    ratio = tq // tk
    width = ATT_HEADS * SB_HEAD_DIM
    q = q_ref[...].astype(BF16)
    lane_head = lax.broadcasted_iota(jnp.int32, (tq, width), 1) // SB_HEAD_DIM
    rel = lax.broadcasted_iota(jnp.int32, (tq, tk), 0) - lax.broadcasted_iota(jnp.int32, (tq, tk), 1)
    tri = tri_ref[...]
    qhs = [jnp.where(lane_head == hh, q, jnp.zeros_like(q)) for hh in range(ATT_HEADS)]
    biases = [bias_ref[hg * ATT_HEADS + hh] for hh in range(ATT_HEADS)]

    def block(kb, carry, mask):
        off = pl.multiple_of(kb * tk, tk)
        kblk = k_ref[pl.ds(off, tk), :]
        vblk = v_ref[pl.ds(off, tk), :]
        out = []
        for hh in range(ATT_HEADS):
            acc, cum = carry[hh]
            z = _dot_nt(qhs[hh], kblk) + biases[hh]
            w, cum = _stick_tile(z, mask, tri, cum)
            out.append((acc + _dot(w.astype(BF16), vblk), cum))
        return tuple(out)

    carry = tuple((jnp.zeros((tq, width), F32), jnp.zeros((tq, 1), F32)) for _ in range(ATT_HEADS))
    for d in range(ratio - 1, -1, -1):
        carry = block(qi * ratio + d, carry, rel > d * tk)
    res = lax.fori_loop(0, qi * ratio, lambda step, c: block(qi * ratio - 1 - step, c, None), carry)
    o = res[0][0]
    for hh in range(1, ATT_HEADS):
        o = jnp.where(lane_head == hh, res[hh][0], o)
    o_ref[...] = o.astype(o_ref.dtype)


def _tri_incl(n):
    r = lax.broadcasted_iota(jnp.int32, (n, n), 0)
    c = lax.broadcasted_iota(jnp.int32, (n, n), 1)
    return (r >= c).astype(BF16)


def _attn_prompt(q, kb, vb, sb_bias, batch, seq):
    tq = ATT_TQ
    nq = seq // tq
    width = ATT_HEADS * SB_HEAD_DIM
    return pl.pallas_call(
        _attn_prompt_kernel,
        grid=(batch, SB_HEADS // ATT_HEADS, nq),
        in_specs=[pl.BlockSpec(memory_space=pltpu.SMEM),
                  pl.BlockSpec((tq, width), lambda b, h, i: (b * nq + i, h)),
                  pl.BlockSpec((seq, width), lambda b, h, i: (b, h)),
                  pl.BlockSpec((seq, width), lambda b, h, i: (b, h)),
                  pl.BlockSpec((ATT_TK, ATT_TK), lambda b, h, i: (0, 0))],
        out_specs=pl.BlockSpec((tq, width), lambda b, h, i: (b * nq + i, h)),
        out_shape=jax.ShapeDtypeStruct((batch * seq, SB_WIDTH), BF16),
        compiler_params=_params("arbitrary", "arbitrary", "arbitrary"),
        name="attn_prompt",
    )(sb_bias, q, kb, vb, _tri_incl(ATT_TK))


def _attn_sample_kernel(n_pages, dec_seq, pt_ref, bias_ref, q_ref, kn_ref, vn_ref, tri_ref, *refs):
    k_pages = refs[:n_pages]
    v_pages = refs[n_pages:2 * n_pages]
    o_ref = refs[2 * n_pages]
    rows = SB_HEADS * dec_seq
    q = q_ref[...]
    qt = jnp.concatenate([q] * SB_HEADS, axis=0)
    row_head = lax.broadcasted_iota(jnp.int32, (rows, SB_WIDTH), 0) // dec_seq
    lane_head = lax.broadcasted_iota(jnp.int32, (rows, SB_WIDTH), 1) // SB_HEAD_DIM
    qbd = jnp.where(row_head == lane_head, qt, 0.0).astype(BF16)
    bias = bias_ref[...]
    tri = tri_ref[...]
    r_i = lax.broadcasted_iota(jnp.int32, (rows, PAGE), 0) % dec_seq
    c_i = lax.broadcasted_iota(jnp.int32, (rows, PAGE), 1)
    new_valid = c_i < r_i

    def pad_rows(a):
        return jnp.concatenate([a, jnp.zeros((PAGE - dec_seq, SB_WIDTH), a.dtype)], axis=0)

    def page_tile(ref):
        return jnp.concatenate([ref[pl.ds(h, PAGE, stride=SB_HEADS), :] for h in range(SB_HEADS)], axis=1)

    acc = jnp.zeros((rows, SB_WIDTH), F32)
    cum = jnp.zeros((rows, 1), F32)
    blocks = [(lambda: pad_rows(kn_ref[...]), lambda: pad_rows(vn_ref[...]), new_valid)]
    for p in range(n_pages - 1, -1, -1):
        blocks.append((lambda p=p: page_tile(k_pages[p]), lambda p=p: page_tile(v_pages[p]), None))
    for load_k, load_v, mask in blocks:
        kblk = load_k()
        vblk = load_v()
        z = _dot_nt(qbd, kblk.astype(BF16)) + bias
        w, cum = _stick_tile(z, mask, tri, cum)
        acc = acc + _dot(w.astype(BF16), vblk.astype(BF16))
    out = jnp.zeros((dec_seq, SB_WIDTH), F32)
    lane_head8 = lax.broadcasted_iota(jnp.int32, (dec_seq, SB_WIDTH), 1) // SB_HEAD_DIM
    for h in range(SB_HEADS):
        out = out + jnp.where(lane_head8 == h, acc[h * dec_seq:(h + 1) * dec_seq, :], 0.0)
    o_ref[...] = out.astype(o_ref.dtype)


def _attn_sample(q, k_new, v_new, cache_k, cache_v, page_table, sb_bias, dec_seq):
    n_seq, n_pages = page_table.shape
    rows = SB_HEADS * dec_seq
    bias_rows = jnp.broadcast_to(jnp.repeat(sb_bias, dec_seq)[:, None], (rows, PAGE))

    def page_spec(p):
        return pl.BlockSpec((PAGE * SB_HEADS, SB_HEAD_DIM), lambda s, pt: (pt[s, p], 0))

    tok_spec = pl.BlockSpec((dec_seq, SB_WIDTH), lambda s, pt: (s, 0))
    grid_spec = pltpu.PrefetchScalarGridSpec(
        num_scalar_prefetch=1,
        grid=(n_seq,),
        in_specs=[pl.BlockSpec((rows, PAGE), lambda s, pt: (0, 0)), tok_spec, tok_spec, tok_spec,
                  pl.BlockSpec((PAGE, PAGE), lambda s, pt: (0, 0))]
                 + [page_spec(p) for p in range(n_pages)] + [page_spec(p) for p in range(n_pages)],
        out_specs=tok_spec,
    )
    return pl.pallas_call(
        functools.partial(_attn_sample_kernel, n_pages, dec_seq),
        grid_spec=grid_spec,
        out_shape=jax.ShapeDtypeStruct((n_seq * dec_seq, SB_WIDTH), F32),
        compiler_params=_params("arbitrary"),
        name="attn_sample",
    )(page_table, bias_rows, q, k_new, v_new, _tri_incl(PAGE), *([cache_k] * n_pages), *([cache_v] * n_pages))


def _expand_matrix(reps):
    r = lax.broadcasted_iota(jnp.int32, (LANES, SSM_HEADS * reps), 0)
    c = lax.broadcasted_iota(jnp.int32, (LANES, SSM_HEADS * reps), 1)
    return (c // reps == r).astype(BF16)


def _ssd_core(act, dt_raw, z, valid, tri, e1, e2, dtb, a_row, dsk, nw, state_in, c_masks):
    xs = act[:, :SSM_INNER]
    bm = act[:, SSM_INNER:SSM_INNER + SSM_GROUPS * SSM_N]
    cm = act[:, SSM_INNER + SSM_GROUPS * SSM_N:]
    dt = _softplus(dt_raw + dtb)
    dta = dt * a_row
    acs = _dot3_r(tri, dta)
    acs_t = acs.T
    acs_e2 = _dot3_l(acs, e2)
    dt_e = _dot3_l(dt, e1)
    eacs_e = _dot3_l(jnp.exp(acs), e1)
    n_slots = len(state_in)
    dend_e = []
    cdec_e = []
    for s in range(n_slots):
        if c_masks[s] is None:
            last = acs[SSM_CHUNK - 1:SSM_CHUNK, :]
        else:
            last = jnp.min(jnp.where(c_masks[s], acs, jnp.inf), axis=0, keepdims=True)
        de = jnp.exp(last - acs)
        if c_masks[s] is not None:
            de = jnp.where(c_masks[s], de, 0.0)
        dend_e.append(_dot3_l(de, e1))
        cdec_e.append(_dot3_l(jnp.broadcast_to(jnp.exp(last), (SUBLANES, LANES)), e1)[0:1, :])
    bmb = [bm[:, g * SSM_N:(g + 1) * SSM_N] for g in range(SSM_GROUPS)]
    cmb = [cm[:, g * SSM_N:(g + 1) * SSM_N].astype(BF16) for g in range(SSM_GROUPS)]
    cb = [_dot_nt(cmb[g], bmb[g].astype(BF16)) for g in range(SSM_GROUPS)]
    bt = [bmb[g].T.astype(BF16) for g in range(SSM_GROUPS)]
    lane = lax.broadcasted_iota(jnp.int32, (SSM_CHUNK, LANES), 1)
    y_pairs = []
    new_states = [[None] * SSM_PAIRS for _ in range(n_slots)]
    for k in range(SSM_PAIRS):
        g = (2 * k) // (SSM_HEADS // SSM_GROUPS)
        sl = slice(k * LANES, (k + 1) * LANES)
        xdt = xs[:, sl] * dt_e[:, sl]
        xdt_b = xdt.astype(BF16)
        parts = []
        for hh in range(2):
            h = 2 * k + hh
            seg = acs_e2[:, h * LANES:(h + 1) * LANES] - acs_t[h:h + 1, :]
            decay = jnp.where(valid, jnp.exp(jnp.where(valid, seg, 0.0)), 0.0)
            parts.append(_dot((cb[g] * decay).astype(BF16), xdt_b))
        y = jnp.where(lane < SSM_P, parts[0], parts[1])
        y_off = None
        for s in range(n_slots):
            st = state_in[s][k]
            c_s = cmb[g] if c_masks[s] is None else jnp.where(c_masks[s], cmb[g], jnp.zeros_like(cmb[g]))
            contrib = _dot(c_s, st.astype(BF16))
            y_off = contrib if y_off is None else y_off + contrib
            xdd = (xdt * dend_e[s][:, sl]).astype(BF16)
            new_states[s][k] = st * cdec_e[s][:, sl] + _dot(bt[g], xdd)
        y = y + y_off * eacs_e[:, sl] + dsk[:, sl] * xs[:, sl]
        y_pairs.append(y)
    y = jnp.concatenate(y_pairs, axis=1) * _silu(z)
    gsz = SSM_INNER // SSM_GROUPS
    outs = []
    for g in range(SSM_GROUPS):
        yg = y[:, g * gsz:(g + 1) * gsz]
        outs.append(yg * lax.rsqrt(jnp.mean(yg * yg, axis=-1, keepdims=True) + EPS) * nw[:, g * gsz:(g + 1) * gsz])
    return jnp.concatenate(outs, axis=1), new_states


def _ssd_prompt_kernel(xbc_ref, z_ref, dt_ref, cw_ref, cbias_ref, dtb_ref, a_ref, dsk_ref, nw_ref, tri_ref, e1_ref,
                       e2_ref, y_ref, tail_ref, st_ref, xs_scr, st_scr):
    c = pl.program_id(1)

    @pl.when(c == 0)
    def _():
        xs_scr[0:SUBLANES, :] = jnp.zeros((SUBLANES, CONV_DIM), F32)
        st_scr[...] = jnp.zeros_like(st_scr)

    xs_scr[SUBLANES:SUBLANES + SSM_CHUNK, :] = xbc_ref[...]
    conv = cbias_ref[...]
    for i in range(CONV_W):
        off = SUBLANES - (CONV_W - 1) + i
        conv = conv + xs_scr[off:off + SSM_CHUNK, :] * cw_ref[i:i + 1, :]
    tail = xs_scr[SSM_CHUNK:SSM_CHUNK + SUBLANES, :]
    xs_scr[0:SUBLANES, :] = tail
    tail_ref[0] = tail
    act = _silu(conv)
    r = lax.broadcasted_iota(jnp.int32, (SSM_CHUNK, SSM_CHUNK), 0)
    cc = lax.broadcasted_iota(jnp.int32, (SSM_CHUNK, SSM_CHUNK), 1)
    valid = r >= cc
    state_in = [[st_scr[k] for k in range(SSM_PAIRS)]]
    y, new_states = _ssd_core(act, dt_ref[...], z_ref[...], valid, tri_ref[...], e1_ref[...], e2_ref[...],
                              dtb_ref[...], a_ref[...], dsk_ref[...], nw_ref[...], state_in, [None])
    for k in range(SSM_PAIRS):
        st_scr[k] = new_states[0][k]
        st_ref[0, k] = new_states[0][k]
    y_ref[...] = y.astype(y_ref.dtype)


def _ssd_consts(conv_w, conv_b, dt_bias, a_log, d_skip, ssm_norm_w):
    pad = LANES - SSM_HEADS
    cw = jnp.concatenate([conv_w, jnp.zeros((SUBLANES - CONV_W, CONV_DIM), F32)], axis=0)
    dtb = jnp.pad(dt_bias, (0, pad)).reshape(1, LANES)
    a_row = jnp.pad(-jnp.exp(a_log), (0, pad)).reshape(1, LANES)
    dsk = jnp.repeat(d_skip, SSM_P).reshape(1, SSM_INNER)
    return cw, conv_b.reshape(1, CONV_DIM), dtb, a_row, dsk, ssm_norm_w.reshape(1, SSM_INNER)


def _const_spec(shape, ngrid):
    zeros = (0,) * len(shape)
    if ngrid == 1:
        return pl.BlockSpec(shape, lambda i: zeros)
    return pl.BlockSpec(shape, lambda i, j: zeros)


def _ssd_prompt(xbc, z, dt, consts, batch, seq):
    nc = seq // SSM_CHUNK
    cw, cbias, dtb, a_row, dsk, nw = consts
    tok = lambda w: pl.BlockSpec((SSM_CHUNK, w), lambda b, c: (b * nc + c, 0))
    tri = _tri_incl(SSM_CHUNK)
    return pl.pallas_call(
        _ssd_prompt_kernel,
        grid=(batch, nc),
        in_specs=[tok(CONV_DIM), tok(SSM_INNER), tok(LANES),
                  _const_spec(cw.shape, 2), _const_spec(cbias.shape, 2), _const_spec(dtb.shape, 2),
                  _const_spec(a_row.shape, 2), _const_spec(dsk.shape, 2), _const_spec(nw.shape, 2),
                  _const_spec((SSM_CHUNK, SSM_CHUNK), 2), _const_spec((LANES, SSM_INNER), 2),
                  _const_spec((LANES, SSM_HEADS * LANES), 2)],
        out_specs=[tok(SSM_INNER),
                   pl.BlockSpec((1, SUBLANES, CONV_DIM), lambda b, c: (b, 0, 0)),
                   pl.BlockSpec((1, SSM_PAIRS, SSM_N, LANES), lambda b, c: (b, 0, 0, 0))],
        out_shape=[jax.ShapeDtypeStruct((batch * seq, SSM_INNER), BF16),
                   jax.ShapeDtypeStruct((batch, SUBLANES, CONV_DIM), F32),
                   jax.ShapeDtypeStruct((batch, SSM_PAIRS, SSM_N, LANES), F32)],
        scratch_shapes=[pltpu.VMEM((SSM_CHUNK + SUBLANES, CONV_DIM), F32),
                        pltpu.VMEM((SSM_PAIRS, SSM_N, LANES), F32)],
        compiler_params=_params("arbitrary", "arbitrary"),
        name="ssd_prompt",
    )(xbc, z, dt, cw, cbias, dtb, a_row, dsk, nw, tri, _expand_matrix(SSM_P), _expand_matrix(LANES))


def _ssd_sample_kernel(dec_seq, xbc_ref, prev_ref, z_ref, dt_ref, st_in_ref, cw_ref, cbias_ref, dtb_ref, a_ref,
                       dsk_ref, nw_ref, tri_ref, e1_ref, e2_ref, y_ref, st_ref):
    n_slots = SSM_CHUNK // dec_seq
    cur = xbc_ref[...]
    prev = prev_ref[...]
    pos = lax.broadcasted_iota(jnp.int32, (SSM_CHUNK, CONV_DIM), 0) % dec_seq
    conv = cbias_ref[...] + cur * cw_ref[CONV_W - 1:CONV_W, :]
    for k in range(1, CONV_W):
        shifted = jnp.where(pos < k, pltpu.roll(prev, SSM_CHUNK - dec_seq + k, axis=0), pltpu.roll(cur, k, axis=0))
        conv = conv + shifted * cw_ref[CONV_W - 1 - k:CONV_W - k, :]
    act = _silu(conv)
    r = lax.broadcasted_iota(jnp.int32, (SSM_CHUNK, SSM_CHUNK), 0)
    cc = lax.broadcasted_iota(jnp.int32, (SSM_CHUNK, SSM_CHUNK), 1)
    valid = (r >= cc) & ((r // dec_seq) == (cc // dec_seq))
    row_seq = lax.broadcasted_iota(jnp.int32, (SSM_CHUNK, LANES), 0) // dec_seq
    c_masks = [row_seq == s for s in range(n_slots)]
    state_in = [[st_in_ref[s, k] for k in range(SSM_PAIRS)] for s in range(n_slots)]
    y, new_states = _ssd_core(act, dt_ref[...], z_ref[...], valid, tri_ref[...], e1_ref[...], e2_ref[...],
                              dtb_ref[...], a_ref[...], dsk_ref[...], nw_ref[...], state_in, c_masks)
    for s in range(n_slots):
        for k in range(SSM_PAIRS):
            st_ref[s, k] = new_states[s][k]
    y_ref[...] = y.astype(y_ref.dtype)


def _ssd_sample(xbc, prev, z, dt, state_t, consts, dec_seq):
    n_tok = xbc.shape[0]
    n_slots = SSM_CHUNK // dec_seq
    nblk = n_tok // SSM_CHUNK
    cw, cbias, dtb, a_row, dsk, nw = consts
    tok = lambda w: pl.BlockSpec((SSM_CHUNK, w), lambda i: (i, 0))
    st_spec = pl.BlockSpec((n_slots, SSM_PAIRS, SSM_N, LANES), lambda i: (i, 0, 0, 0))
    r = lax.broadcasted_iota(jnp.int32, (SSM_CHUNK, SSM_CHUNK), 0)
    c = lax.broadcasted_iota(jnp.int32, (SSM_CHUNK, SSM_CHUNK), 1)
    tri = ((r >= c) & ((r // dec_seq) == (c // dec_seq))).astype(BF16)
    return pl.pallas_call(
        functools.partial(_ssd_sample_kernel, dec_seq),
        grid=(nblk,),
        in_specs=[tok(CONV_DIM), tok(CONV_DIM), tok(SSM_INNER), tok(LANES), st_spec,
                  _const_spec(cw.shape, 1), _const_spec(cbias.shape, 1), _const_spec(dtb.shape, 1),
                  _const_spec(a_row.shape, 1), _const_spec(dsk.shape, 1), _const_spec(nw.shape, 1),
                  _const_spec((SSM_CHUNK, SSM_CHUNK), 1), _const_spec((LANES, SSM_INNER), 1),
                  _const_spec((LANES, SSM_HEADS * LANES), 1)],
        out_specs=[tok(SSM_INNER), st_spec],
        out_shape=[jax.ShapeDtypeStruct((n_tok, SSM_INNER), BF16),
                   jax.ShapeDtypeStruct(state_t.shape, F32)],
        compiler_params=_params("arbitrary"),
        name="ssd_sample",
    )(xbc, prev, z, dt, state_t, cw, cbias, dtb, a_row, dsk, nw, tri, _expand_matrix(SSM_P), _expand_matrix(LANES))


def _merge_kernel(o_ref, y_ref, ga_ref, gs_ref, x_ref, g1_ref, sh_ref, sc_ref, nw_ref, wa_ref, ws_ref, wo_ref,
                  x1_ref, h2_ref):
    merged = (jax.nn.sigmoid(ga_ref[...]) * _dot(o_ref[...].astype(BF16), wa_ref[...])
              + jax.nn.sigmoid(gs_ref[...]) * _dot(y_ref[...], ws_ref[...]))
    mix = _dot(merged.astype(BF16), wo_ref[...])
    x = x_ref[...]
    g, r, d = x.shape
    x1 = x + g1_ref[...] * mix.reshape(g, r, d)
    x1_ref[...] = x1
    y = x1 * lax.rsqrt(jnp.mean(x1 * x1, axis=-1, keepdims=True) + EPS) * nw_ref[...]
    h2 = y * (1.0 + sc_ref[...]) + sh_ref[...]
    h2_ref[...] = h2.reshape(g * r, d).astype(h2_ref.dtype)


def _mod_map(gpb, blocks_per_mod, col):
    if gpb == 1:
        return lambda i: (i // blocks_per_mod, 0, col)
    return lambda i: (i, 0, col)


def _merge(o_attn, y_ssm, ga, gs, x3, mod3, norm2_w, wa, ws, wo, groups_per_block, blocks_per_mod):
    ng, r, d = x3.shape
    gpb = groups_per_block
    nblk = ng // gpb
    tm = gpb * r
    t = ng * r
    tok = lambda w: pl.BlockSpec((tm, w), lambda i: (i, 0))
    xspec = pl.BlockSpec((gpb, r, d), lambda i: (i, 0, 0))
    mspec = lambda col: pl.BlockSpec((gpb, 1, d), _mod_map(gpb, blocks_per_mod, col))
    return pl.pallas_call(
        _merge_kernel,
        grid=(nblk,),
        in_specs=[tok(SB_WIDTH), tok(SSM_INNER), tok(d), tok(d), xspec, mspec(2), mspec(3), mspec(4),
                  pl.BlockSpec((1, 1, d), lambda i: (0, 0, 0)),
                  _const_spec(wa.shape, 1), _const_spec(ws.shape, 1), _const_spec(wo.shape, 1)],
        out_specs=[xspec, tok(d)],
        out_shape=[jax.ShapeDtypeStruct((ng, r, d), F32), jax.ShapeDtypeStruct((t, d), BF16)],
        compiler_params=_params("arbitrary"),
        name="merge",
    )(o_attn, y_ssm, ga, gs, x3, mod3, mod3, mod3, norm2_w.reshape(1, 1, d), wa, ws, wo)


def _batcher_pairs(n):
    pairs = []
    p = 1
    while p < n:
        k = p
        while k >= 1:
            for j in range(k % p, n - k, 2 * k):
                for i in range(min(k, n - j - k)):
                    if (i + j) // (2 * p) == (i + j + k) // (2 * p):
                        pairs.append((i + j, i + j + k))
            k //= 2
        p *= 2
    return pairs


_SORT16 = _batcher_pairs(PEER_TOPK)


def _sort_desc(vals):
    vals = list(vals)
    for i, j in _SORT16:
        hi = jnp.maximum(vals[i], vals[j])
        lo = jnp.minimum(vals[i], vals[j])
        vals[i], vals[j] = hi, lo
    return vals


def _merge_top(a, b, sort_result=True):
    n = PEER_TOPK
    t = [jnp.maximum(a[k], b[n - 1 - k]) for k in range(n)]
    if not sort_result:
        return t
    d = n // 2
    while d >= 1:
        for k in range(n):
            if (k // d) % 2 == 0:
                hi = jnp.maximum(t[k], t[k + d])
                lo = jnp.minimum(t[k], t[k + d])
                t[k], t[k + d] = hi, lo
        d //= 2
    return t


def _top16_desc(vals, need_sorted=True):
    groups = [_sort_desc(vals[i:i + PEER_TOPK]) for i in range(0, len(vals), PEER_TOPK)]
    while len(groups) > 1:
        nxt = []
        for i in range(0, len(groups) - 1, 2):
            last = len(groups) == 2
            nxt.append(_merge_top(groups[i], groups[i + 1], sort_result=need_sorted or not last))
        if len(groups) % 2:
            nxt.append(groups[-1])
        groups = nxt
    return groups[0]


ROUTE_TM = 256


def _route_kernel(h_ref, wq_ref, k1_ref, k2_ref, th_ref, p_ref, code_ref, r_ref, s1_scr, s2_scr):
    half = PEER_HEADS * PEER_HALF
    qt = _dot_nt(wq_ref[...], h_ref[...])
    s1 = _dot(k1_ref[...], qt[:half].astype(BF16))
    s2 = _dot(k2_ref[...], qt[half:].astype(BF16))
    for c in range(ROUTE_TM // LANES):
        s1_scr[c] = s1[:, c * LANES:(c + 1) * LANES]
        s2_scr[c] = s2[:, c * LANES:(c + 1) * LANES]
    staircase = [(a, b) for a in range(PEER_TOPK) for b in range(PEER_TOPK) if (a + 1) * (b + 1) <= PEER_TOPK]
    neg_inf = jnp.full((SUBLANES, LANES), -jnp.inf, F32)
    for c in range(ROUTE_TM // LANES):
        tl = pl.ds(c * LANES, LANES)
        row = lambda scr, i, c=c: scr[c, i * SUBLANES:(i + 1) * SUBLANES, :]
        top1 = _top16_desc([row(s1_scr, i) for i in range(PEER_KEYS)])
        top2 = _top16_desc([row(s2_scr, i) for i in range(PEER_KEYS)])
        cand = {(a, b): top1[a] + top2[b] for a, b in staircase}
        cvals = [cand[ab] for ab in staircase]
        cvals += [neg_inf] * (-len(cvals) % PEER_TOPK)
        tau = _top16_desc(cvals, need_sorted=False)
        thr = tau[0]
        for v in tau[1:]:
            thr = jnp.minimum(thr, v)
        cmax = cand[(0, 0)]
        zsum = jnp.zeros((SUBLANES, LANES), F32)
        never = jnp.full((SUBLANES, LANES), PEER_TOPK + 1.0, F32)
        need = [never] * PEER_TOPK
        for a, b in staircase:
            ok = cand[(a, b)] >= thr
            zsum = zsum + jnp.where(ok, jnp.exp(cand[(a, b)] - cmax), 0.0)
            need[a] = need[a] - jnp.where(ok, 1.0, 0.0)
        inv_z = 1.0 / zsum
        for i in range(PEER_KEYS):
            s = row(s1_scr, i)
            th = never
            for a in range(PEER_TOPK):
                th = jnp.where(s == top1[a], need[a], th)
            th_ref[i, :, tl] = th
            p_ref[i, :, tl] = jnp.exp(s - top1[0]) * inv_z
        for h in range(PEER_HEADS):
            s2h = s2_scr[c, pl.ds(h, PEER_KEYS, stride=SUBLANES), :]
            code = jnp.zeros((PEER_KEYS, LANES), F32)
            for b in range(PEER_TOPK):
                code = code + jnp.where(s2h >= top2[b][h:h + 1, :], 1.0, 0.0)
            tiles = (PEER_KEYS // BF16_ROWS, BF16_ROWS, LANES)
            code_ref[h, :, :, tl] = code.reshape(tiles).astype(BF16)
            r_ref[h, :, :, tl] = jnp.exp(s2h - top2[0][h:h + 1, :]).reshape(tiles).astype(BF16)


def _route(h2, wq_t, kbd1, kbd2):
    t = h2.shape[0]
    tm = ROUTE_TM
    half = PEER_HEADS * PEER_HALF
    kt = PEER_KEYS // BF16_ROWS
    return pl.pallas_call(
        _route_kernel,
        grid=(t // tm,),
        in_specs=[pl.BlockSpec((tm, D_MODEL), lambda i: (i, 0)),
                  _const_spec(wq_t.shape, 1), _const_spec(kbd1.shape, 1), _const_spec(kbd2.shape, 1)],
        out_specs=[pl.BlockSpec((PEER_KEYS, PEER_HEADS, tm), lambda i: (0, 0, i)),
                   pl.BlockSpec((PEER_KEYS, PEER_HEADS, tm), lambda i: (0, 0, i)),
                   pl.BlockSpec((PEER_HEADS, kt, BF16_ROWS, tm), lambda i: (0, 0, 0, i)),
                   pl.BlockSpec((PEER_HEADS, kt, BF16_ROWS, tm), lambda i: (0, 0, 0, i))],
        out_shape=[jax.ShapeDtypeStruct((PEER_KEYS, PEER_HEADS, t), F32),
                   jax.ShapeDtypeStruct((PEER_KEYS, PEER_HEADS, t), F32),
                   jax.ShapeDtypeStruct((PEER_HEADS, kt, BF16_ROWS, t), BF16),
                   jax.ShapeDtypeStruct((PEER_HEADS, kt, BF16_ROWS, t), BF16)],
        scratch_shapes=[pltpu.VMEM((tm // LANES, half, LANES), F32), pltpu.VMEM((tm // LANES, half, LANES), F32)],
        compiler_params=_params("arbitrary"),
        name="peer_route",
    )(h2, wq_t, kbd1, kbd2)


PEER_TM = 1024
PEER_KB = 8
PEER_SUB = 8
BF16_ROWS = 16


def _gelu_tanh(x):
    return 0.5 * x * (1.0 + jnp.tanh(math.sqrt(2.0 / math.pi) * (x + 0.044715 * (x * x * x))))


def _peer_kernel(h_ref, u_ref, vt_ref, th_ref, p_ref, code_ref, r_ref, o_ref, acc_ref):
    j = pl.program_id(1)
    tm = h_ref.shape[0]
    kt = PEER_KEYS // BF16_ROWS

    @pl.when(j == 0)
    def _():
        acc_ref[...] = jnp.zeros_like(acc_ref)

    zero = jnp.zeros((), BF16)
    h = h_ref[...]
    total = None
    for sb in range(PEER_KB // PEER_SUB):
        rows = slice(sb * PEER_SUB * PEER_KEYS, (sb + 1) * PEER_SUB * PEER_KEYS)
        act = _gelu_tanh(_dot_nt(u_ref[rows, :], h))
        act = act.reshape(PEER_SUB * kt, BF16_ROWS, tm).astype(BF16)
        parts = []
        for s in range(PEER_SUB):
            ii = sb * PEER_SUB + s
            w = None
            for hd in range(PEER_HEADS):
                need = jnp.broadcast_to(th_ref[ii, hd:hd + 1, :], (BF16_ROWS, tm)).astype(BF16)
                pr = jnp.broadcast_to(p_ref[ii, hd:hd + 1, :], (BF16_ROWS, tm)).astype(BF16)
                sel = jnp.where(code_ref[hd] >= need[None], r_ref[hd], zero) * pr[None]
                w = sel if w is None else w + sel
            parts.append(w * act[s * kt:(s + 1) * kt])
        a = jnp.concatenate(parts, axis=0).reshape(PEER_SUB * PEER_KEYS, tm)
        contrib = _dot(vt_ref[:, rows], a)
        total = contrib if total is None else total + contrib
    acc_ref[...] += total

    @pl.when(j == pl.num_programs(1) - 1)
    def _():
        o_ref[...] = acc_ref[...].T


def _peer_dense(h2, u_b, vt_b, th, p, code, r):
    t = h2.shape[0]
    tm = min(PEER_TM, t)
    eb = PEER_KB * PEER_KEYS
    nj = PEER_KEYS // PEER_KB
    kt = PEER_KEYS // BF16_ROWS
    mask_spec = pl.BlockSpec((PEER_HEADS, kt, BF16_ROWS, tm), lambda i, j: (0, 0, 0, i))
    return pl.pallas_call(
        _peer_kernel,
        grid=(t // tm, nj),
        in_specs=[pl.BlockSpec((tm, D_MODEL), lambda i, j: (i, 0)),
                  pl.BlockSpec((eb, D_MODEL), lambda i, j: (j, 0)),
                  pl.BlockSpec((D_MODEL, eb), lambda i, j: (0, j)),
                  pl.BlockSpec((PEER_KB, PEER_HEADS, tm), lambda i, j: (j, 0, i)),
                  pl.BlockSpec((PEER_KB, PEER_HEADS, tm), lambda i, j: (j, 0, i)),
                  mask_spec, mask_spec],
        out_specs=pl.BlockSpec((tm, D_MODEL), lambda i, j: (i, 0)),
        out_shape=jax.ShapeDtypeStruct((t, D_MODEL), F32),
        scratch_shapes=[pltpu.VMEM((D_MODEL, tm), F32)],
        compiler_params=_params("arbitrary", "arbitrary"),
        name="peer_dense",
    )(h2, u_b, vt_b, th, p, code, r)


def _final_kernel(x_ref, pe_ref, g2_ref, nw_ref, y_ref):
    x1 = x_ref[...]
    g, r, d = x1.shape
    x2 = x1 + g2_ref[...] * pe_ref[...].reshape(g, r, d)
    y_ref[...] = x2 * lax.rsqrt(jnp.mean(x2 * x2, axis=-1, keepdims=True) + EPS) * nw_ref[...]


def _final(x1_3, peer_out, mod3, norm_f_w, groups_per_block, blocks_per_mod):
    ng, r, d = x1_3.shape
    gpb = groups_per_block
    nblk = ng // gpb
    tm = gpb * r
    xspec = pl.BlockSpec((gpb, r, d), lambda i: (i, 0, 0))
    return pl.pallas_call(
        _final_kernel,
        grid=(nblk,),
        in_specs=[xspec, pl.BlockSpec((tm, d), lambda i: (i, 0)),
                  pl.BlockSpec((gpb, 1, d), _mod_map(gpb, blocks_per_mod, 5)),
                  pl.BlockSpec((1, 1, d), lambda i: (0, 0, 0))],
        out_specs=xspec,
        out_shape=jax.ShapeDtypeStruct((ng, r, d), F32),
        compiler_params=_params("arbitrary"),
        name="final_norm",
    )(x1_3, peer_out, mod3, norm_f_w.reshape(1, 1, d))


def _pack_w_in(w_in):
    q_end = 3 * SB_WIDTH + SSM_INNER + CONV_DIM
    dt_w = jnp.pad(w_in[:, q_end:q_end + SSM_HEADS], ((0, 0), (0, LANES - SSM_HEADS)))
    return jnp.concatenate([w_in[:, :q_end], dt_w, w_in[:, q_end + SSM_HEADS:]], axis=1).astype(BF16)


def _pack_peer(peer_w_query, peer_keys1, peer_keys2):
    d = peer_w_query.shape[0]
    wq = peer_w_query.reshape(d, PEER_HEADS, 2, PEER_HALF)
    wq_t = jnp.transpose(wq, (2, 1, 3, 0)).reshape(2 * PEER_HEADS * PEER_HALF, d).astype(BF16)
    eye = jnp.eye(PEER_HEADS, dtype=F32)

    def block_diag(keys):
        return jnp.einsum("kd,hg->khgd", keys, eye).reshape(PEER_KEYS * PEER_HEADS, PEER_HEADS * PEER_HALF).astype(BF16)

    return wq_t, block_diag(peer_keys1), block_diag(peer_keys2)


def _layer_tail(o_attn, y_ssm, proj, x3, mod3, lw, gpb, bpm):
    x1_3, h2 = _merge(o_attn, y_ssm, proj["ga"], proj["gs"], x3, mod3, lw["norm2_w"], lw["wa"], lw["ws"], lw["wo"],
                      gpb, bpm)
    th, p, s2t, r = _route(h2, lw["wq_t"], lw["kbd1"], lw["kbd2"])
    peer_out = _peer_dense(h2, lw["u_b"], lw["vt_b"], th, p, s2t, r)
    return _final(x1_3, peer_out, mod3, lw["norm_f_w"], gpb, bpm)


def _proj_dict(outs):
    names = ("q", "k", "v", "kb", "vb", "z", "xbc", "dt", "ga", "gs")
    return dict(zip(names, outs))


INPROJ_TM = 256


def kernel(x_prompt, x_sample, cache_k, cache_v, state_conv, state_ssm, page_table, c_prompt, c_sample, norm1_w,
           norm2_w, w_ada, b_ada, w_in, sb_bias, conv_w, conv_b, dt_bias, a_log, d_skip, ssm_norm_w, w_br_attn,
           w_br_ssm, w_out, peer_w_query, peer_keys1, peer_keys2, peer_u, peer_v, norm_f_w):
    batch, seq, d = x_prompt.shape
    n_seq, dec_seq, _ = x_sample.shape
    assert w_ada.shape[0] == 1, "single layer"
    lw = {"norm2_w": norm2_w[0], "wa": w_br_attn[0].astype(BF16), "ws": w_br_ssm[0].astype(BF16),
          "wo": w_out[0].astype(BF16), "norm_f_w": norm_f_w,
          "u_b": peer_u[0].astype(BF16), "vt_b": peer_v[0].T.astype(BF16)}
    lw["wq_t"], lw["kbd1"], lw["kbd2"] = _pack_peer(peer_w_query[0], peer_keys1[0], peer_keys2[0])
    w_packed = _pack_w_in(w_in[0])
    consts = _ssd_consts(conv_w[0], conv_b[0], dt_bias[0], a_log[0], d_skip[0], ssm_norm_w[0])

    mod = _adaln(jnp.concatenate([c_prompt, c_sample], axis=0), w_ada[0], b_ada[0])
    mod_p = mod[:batch].reshape(batch, 1, 6 * d)
    mod_s = mod[batch:].reshape(n_seq, 1, 6 * d)

    tm = INPROJ_TM
    bpm = seq // tm
    xp3 = x_prompt.reshape(batch * bpm, tm, d)
    pp = _proj_dict(_inproj(xp3, mod_p, norm1_w[0], w_packed, 1, bpm))
    o_attn_p = _attn_prompt(pp["q"], pp["kb"], pp["vb"], sb_bias[0], batch, seq)
    y_ssm_p, tail_p, st_p = _ssd_prompt(pp["xbc"], pp["z"], pp["dt"], consts, batch, seq)
    y_prompt = _layer_tail(o_attn_p, y_ssm_p, pp, xp3, mod_p, lw, 1, bpm).reshape(batch, seq, d)

    gpb = tm // dec_seq
    ps = _proj_dict(_inproj(x_sample, mod_s, norm1_w[0], w_packed, gpb, 1))
    n_pool = cache_k.shape[1]
    pool_rows = n_pool * PAGE * SB_HEADS
    o_attn_s = _attn_sample(ps["q"], ps["k"], ps["v"], cache_k[0].reshape(pool_rows, SB_HEAD_DIM),
                            cache_v[0].reshape(pool_rows, SB_HEAD_DIM), page_table, sb_bias[0], dec_seq)
    prev = jnp.pad(state_conv[0], ((0, 0), (dec_seq - (CONV_W - 1), 0), (0, 0))).reshape(n_seq * dec_seq, CONV_DIM)
    st_in = state_ssm[0].reshape(n_seq, SSM_PAIRS, 2 * SSM_P, SSM_N).transpose(0, 1, 3, 2)
    y_ssm_s, st_s = _ssd_sample(ps["xbc"], prev, ps["z"], ps["dt"], st_in, consts, dec_seq)
    y_sample = _layer_tail(o_attn_s, y_ssm_s, ps, x_sample, mod_s, lw, gpb, 1)

    def state_out(st):
        b = st.shape[0]
        return st.transpose(0, 1, 3, 2).reshape(1, b, SSM_HEADS, SSM_P, SSM_N)

    k_prompt = pp["k"].reshape(1, batch, seq, SB_HEADS, SB_HEAD_DIM)
    v_prompt = pp["v"].reshape(1, batch, seq, SB_HEADS, SB_HEAD_DIM)
    conv_prompt = tail_p[:, SUBLANES - (CONV_W - 1):, :][None]
    k_sample = ps["k"].reshape(1, n_seq, dec_seq, SB_HEADS, SB_HEAD_DIM)
    v_sample = ps["v"].reshape(1, n_seq, dec_seq, SB_HEADS, SB_HEAD_DIM)
    conv_sample = ps["xbc"].reshape(n_seq, dec_seq, CONV_DIM)[:, dec_seq - (CONV_W - 1):, :][None]
    return (y_prompt, y_sample, k_prompt, v_prompt, conv_prompt, state_out(st_p), k_sample, v_sample, conv_sample,
            state_out(st_s))
```

```python
import functools
import math

import jax
import jax.numpy as jnp
from jax import lax
from jax.experimental import pallas as pl
from jax.experimental.pallas import tpu as pltpu

F32 = jnp.float32
BF16 = jnp.bfloat16

D_MODEL = 1024
EPS = 1e-6
SB_HEADS = 8
SB_HEAD_DIM = 64
SB_WIDTH = SB_HEADS * SB_HEAD_DIM
PAGE = 128
SSM_HEADS = 16
SSM_P = 64
SSM_GROUPS = 2
SSM_N = 128
SSM_INNER = SSM_HEADS * SSM_P
SSM_PAIRS = SSM_HEADS // 2
CONV_W = 4
CONV_DIM = SSM_INNER + 2 * SSM_GROUPS * SSM_N
SSM_CHUNK = 128
PEER_HEADS = 8
PEER_KEYS = 128
PEER_HALF = 128
PEER_TOPK = 16

LANES = 128
SUBLANES = 8
VMEM_LIMIT = 56 * 1024 * 1024

_PROJ_GROUPS = (("q", SB_WIDTH), ("k", SB_WIDTH), ("v", SB_WIDTH), ("z", SSM_INNER), ("xbc", CONV_DIM),
                ("dt", LANES), ("ga", D_MODEL), ("gs", D_MODEL))
_PROJ_COLS = sum(w for _, w in _PROJ_GROUPS)


def _params(*sem):
    return pltpu.CompilerParams(dimension_semantics=sem, vmem_limit_bytes=VMEM_LIMIT)


def _dot(a, b):
    return jnp.dot(a, b, preferred_element_type=F32)


def _dot_nt(a, b):
    return lax.dot_general(a, b, (((1,), (1,)), ((), ())), preferred_element_type=F32)


def _split3(x):
    hi = x.astype(BF16)
    r = x - hi.astype(F32)
    mid = r.astype(BF16)
    lo = (r - mid.astype(F32)).astype(BF16)
    return hi, mid, lo


def _dot3_l(x, m):
    hi, mid, lo = _split3(x)
    return _dot(hi, m) + _dot(mid, m) + _dot(lo, m)


def _dot3_r(m, x):
    hi, mid, lo = _split3(x)
    return _dot(m, hi) + _dot(m, mid) + _dot(m, lo)


def _softplus(x):
    return jnp.maximum(x, 0.0) + jnp.log1p(jnp.exp(-jnp.abs(x)))


def _silu(x):
    return x * jax.nn.sigmoid(x)


def _adaln_kernel(c_ref, w_ref, b_ref, o_ref):
    s = _silu(c_ref[...]).astype(BF16)
    o_ref[...] = _dot(s, w_ref[...].astype(BF16)) + b_ref[...]


def _adaln(c_all, w_ada, b_ada):
    m, n = c_all.shape[0], w_ada.shape[1]
    tn = 1024
    return pl.pallas_call(
        _adaln_kernel,
        grid=(n // tn,),
        in_specs=[pl.BlockSpec((m, D_MODEL), lambda j: (0, 0)),
                  pl.BlockSpec((D_MODEL, tn), lambda j: (0, j)),
                  pl.BlockSpec((1, tn), lambda j: (0, j))],
        out_specs=pl.BlockSpec((m, tn), lambda j: (0, j)),
        out_shape=jax.ShapeDtypeStruct((m, n), F32),
        compiler_params=_params("arbitrary"),
        name="adaln",
    )(c_all, w_ada, b_ada.reshape(1, n))


def _inproj_kernel(kv_transposed, x_ref, sh_ref, sc_ref, nw_ref, w_ref, q_ref, k_ref, v_ref, kb_ref, vb_ref, z_ref,
                   xbc_ref, dt_ref, ga_ref, gs_ref):
    x = x_ref[...]
    g, r, d = x.shape
    y = x * lax.rsqrt(jnp.mean(x * x, axis=-1, keepdims=True) + EPS) * nw_ref[...]
    h = y * (1.0 + sc_ref[...]) + sh_ref[...]
    hb = h.reshape(g * r, d).astype(BF16)
    outs = {"q": q_ref, "k": k_ref, "v": v_ref, "z": z_ref, "xbc": xbc_ref, "dt": dt_ref, "ga": ga_ref, "gs": gs_ref}
    col = 0
    for name, width in _PROJ_GROUPS:
        res = _dot(hb, w_ref[:, col:col + width])
        col += width
        if name == "q":
            res = res * (SB_HEAD_DIM ** -0.5)
        if name in ("k", "v") and kv_transposed:
            outs[name][0] = res.T
        else:
            outs[name][...] = res.astype(outs[name].dtype)
        if name == "k":
            kb_ref[...] = res.astype(BF16)
        if name == "v":
            vb_ref[...] = res.astype(BF16)


def _inproj(x3, mod3, norm_w, w_packed, groups_per_block, blocks_per_mod, kv_transposed=False):
    ng, r, d = x3.shape
    gpb = groups_per_block
    nblk = ng // gpb
    tm = gpb * r
    t = ng * r
    if gpb == 1:
        mod_map = lambda col: (lambda i: (i // blocks_per_mod, 0, col))
    else:
        mod_map = lambda col: (lambda i: (i, 0, col))
    widths = dict(_PROJ_GROUPS)
    out_shapes = [
        jax.ShapeDtypeStruct((t, widths["q"]), F32), jax.ShapeDtypeStruct((t, widths["k"]), F32),
        jax.ShapeDtypeStruct((t, widths["v"]), F32), jax.ShapeDtypeStruct((t, widths["k"]), BF16),
        jax.ShapeDtypeStruct((t, widths["v"]), BF16), jax.ShapeDtypeStruct((t, widths["z"]), F32),
        jax.ShapeDtypeStruct((t, widths["xbc"]), F32), jax.ShapeDtypeStruct((t, widths["dt"]), F32),
        jax.ShapeDtypeStruct((t, widths["ga"]), F32), jax.ShapeDtypeStruct((t, widths["gs"]), F32)]
    out_specs = [pl.BlockSpec((tm, s.shape[1]), lambda i: (i, 0)) for s in out_shapes]
    if kv_transposed:
        assert gpb == 1
        for idx in (1, 2):
            out_shapes[idx] = jax.ShapeDtypeStruct((ng // blocks_per_mod, SB_WIDTH, blocks_per_mod * r), F32)
            out_specs[idx] = pl.BlockSpec((1, SB_WIDTH, tm), lambda i: (i // blocks_per_mod, 0, i % blocks_per_mod))
    return pl.pallas_call(
        functools.partial(_inproj_kernel, kv_transposed),
        grid=(nblk,),
        in_specs=[pl.BlockSpec((gpb, r, d), lambda i: (i, 0, 0)),
                  pl.BlockSpec((gpb, 1, d), mod_map(0)),
                  pl.BlockSpec((gpb, 1, d), mod_map(1)),
                  pl.BlockSpec((1, 1, d), lambda i: (0, 0, 0)),
                  pl.BlockSpec((d, _PROJ_COLS), lambda i: (0, 0))],
        out_specs=out_specs,
        out_shape=out_shapes,
        compiler_params=_params("arbitrary"),
        name="inproj",
    )(x3, mod3, mod3, norm_w.reshape(1, 1, d), w_packed)


ATT_TQ = 512
ATT_TK = 256
ATT_HEADS = 4


def _stick_incl(z, mask, tri):
    sp = jnp.maximum(z, 0.0) + jnp.log(1.0 + jnp.exp(-jnp.abs(z)))
    if mask is not None:
        sp = jnp.where(mask, sp, 0.0)
    hi = sp.astype(BF16)
    lo = (sp - hi.astype(F32)).astype(BF16)
    n = z.shape[0]
    both = _dot(jnp.concatenate([hi, lo], axis=0), tri)
    return both[:n] + both[n:]


def _stick_tile(z, mask, tri, cum):
    incl = _stick_incl(z, mask, tri)
    w = jnp.exp(z - incl - cum)
    if mask is not None:
        w = jnp.where(mask, w, 0.0)
    return w, cum + incl[:, 0:1]


def _attn_prompt_kernel(bias_ref, q_ref, k_ref, v_ref, tri_ref, o_ref):
    hg = pl.program_id(1)
    qi = pl.program_id(2)
    tq, tk = ATT_TQ, ATT_TK
    ratio = tq // tk
    width = ATT_HEADS * SB_HEAD_DIM
    q = q_ref[...].astype(BF16)
    lane_head = lax.broadcasted_iota(jnp.int32, (tq, width), 1) // SB_HEAD_DIM
    rel = lax.broadcasted_iota(jnp.int32, (tq, tk), 0) - lax.broadcasted_iota(jnp.int32, (tq, tk), 1)
    tri = tri_ref[...]
    qhs = [jnp.where(lane_head == hh, q, jnp.zeros_like(q)) for hh in range(ATT_HEADS)]
    biases = [bias_ref[hg * ATT_HEADS + hh] for hh in range(ATT_HEADS)]

    def block(kb, carry, mask):
        off = pl.multiple_of(kb * tk, tk)
        kblk = k_ref[pl.ds(off, tk), :]
        vblk = v_ref[pl.ds(off, tk), :]
        out = []
        for hh in range(ATT_HEADS):
            acc, cum = carry[hh]
            z = _dot_nt(qhs[hh], kblk) + biases[hh]
            w, cum = _stick_tile(z, mask, tri, cum)
            out.append((acc + _dot(w.astype(BF16), vblk), cum))
        return tuple(out)

    carry = tuple((jnp.zeros((tq, width), F32), jnp.zeros((tq, 1), F32)) for _ in range(ATT_HEADS))
    for d in range(ratio - 1, -1, -1):
        carry = block(qi * ratio + d, carry, rel > d * tk)
    res = lax.fori_loop(0, qi * ratio, lambda step, c: block(qi * ratio - 1 - step, c, None), carry)
    o = res[0][0]
    for hh in range(1, ATT_HEADS):
        o = jnp.where(lane_head == hh, res[hh][0], o)
    o_ref[...] = o.astype(o_ref.dtype)


def _tri_incl(n):
    r = lax.broadcasted_iota(jnp.int32, (n, n), 0)
    c = lax.broadcasted_iota(jnp.int32, (n, n), 1)
    return (r >= c).astype(BF16)


def _attn_prompt(q, kb, vb, sb_bias, batch, seq):
    tq = ATT_TQ
    nq = seq // tq
    width = ATT_HEADS * SB_HEAD_DIM
    return pl.pallas_call(
        _attn_prompt_kernel,
        grid=(batch, SB_HEADS // ATT_HEADS, nq),
        in_specs=[pl.BlockSpec(memory_space=pltpu.SMEM),
                  pl.BlockSpec((tq, width), lambda b, h, i: (b * nq + i, h)),
                  pl.BlockSpec((seq, width), lambda b, h, i: (b, h)),
                  pl.BlockSpec((seq, width), lambda b, h, i: (b, h)),
                  pl.BlockSpec((ATT_TK, ATT_TK), lambda b, h, i: (0, 0))],
        out_specs=pl.BlockSpec((tq, width), lambda b, h, i: (b * nq + i, h)),
        out_shape=jax.ShapeDtypeStruct((batch * seq, SB_WIDTH), BF16),
        compiler_params=_params("arbitrary", "arbitrary", "arbitrary"),
        name="attn_prompt",
    )(sb_bias, q, kb, vb, _tri_incl(ATT_TK))


def _attn_sample_kernel(n_pages, dec_seq, pt_ref, bias_ref, q_ref, kn_ref, vn_ref, tri_ref, *refs):
    k_pages = refs[:n_pages]
    v_pages = refs[n_pages:2 * n_pages]
    o_ref = refs[2 * n_pages]
    rows = SB_HEADS * dec_seq
    q = q_ref[...]
    qt = jnp.concatenate([q] * SB_HEADS, axis=0)
    row_head = lax.broadcasted_iota(jnp.int32, (rows, SB_WIDTH), 0) // dec_seq
    lane_head = lax.broadcasted_iota(jnp.int32, (rows, SB_WIDTH), 1) // SB_HEAD_DIM
    qbd = jnp.where(row_head == lane_head, qt, 0.0).astype(BF16)
    bias = bias_ref[...]
    tri = tri_ref[...]
    r_i = lax.broadcasted_iota(jnp.int32, (rows, PAGE), 0) % dec_seq
    c_i = lax.broadcasted_iota(jnp.int32, (rows, PAGE), 1)
    new_valid = c_i < r_i

    def pad_rows(a):
        return jnp.concatenate([a, jnp.zeros((PAGE - dec_seq, SB_WIDTH), a.dtype)], axis=0)

    def page_tile(ref):
        return ref[0].reshape(SB_WIDTH, PAGE).astype(BF16)

    zs = [_dot_nt(qbd, pad_rows(kn_ref[...]).astype(BF16)) + bias]
    zs += [_dot(qbd, page_tile(k_pages[p])) + bias for p in range(n_pages - 1, -1, -1)]
    masks = [new_valid] + [None] * n_pages
    incls = [_stick_incl(z, m, tri) for z, m in zip(zs, masks)]
    cum = jnp.zeros((rows, 1), F32)
    acc = None
    for b, (z, m, incl) in enumerate(zip(zs, masks, incls)):
        w = jnp.exp(z - incl - cum)
        cum = cum + incl[:, 0:1]
        if m is not None:
            w = jnp.where(m, w, 0.0)
        wb = w.astype(BF16)
        if b == 0:
            contrib = _dot(wb, pad_rows(vn_ref[...]).astype(BF16))
        else:
            contrib = _dot_nt(wb, page_tile(v_pages[n_pages - b]))
        acc = contrib if acc is None else acc + contrib
    out = jnp.zeros((dec_seq, SB_WIDTH), F32)
    lane_head8 = lax.broadcasted_iota(jnp.int32, (dec_seq, SB_WIDTH), 1) // SB_HEAD_DIM
    for h in range(SB_HEADS):
        out = out + jnp.where(lane_head8 == h, acc[h * dec_seq:(h + 1) * dec_seq, :], 0.0)
    o_ref[...] = out.astype(o_ref.dtype)


def _attn_sample(q, k_new, v_new, cache_k, cache_v, page_table, sb_bias, dec_seq):
    n_seq, n_pages = page_table.shape
    rows = SB_HEADS * dec_seq
    bias_rows = jnp.broadcast_to(jnp.repeat(sb_bias, dec_seq)[:, None], (rows, PAGE))

    def page_spec(p):
        return pl.BlockSpec((1, SB_HEADS, SB_HEAD_DIM, PAGE), lambda s, pt: (pt[s, p], 0, 0, 0))

    tok_spec = pl.BlockSpec((dec_seq, SB_WIDTH), lambda s, pt: (s, 0))
    grid_spec = pltpu.PrefetchScalarGridSpec(
        num_scalar_prefetch=1,
        grid=(n_seq,),
        in_specs=[pl.BlockSpec((rows, PAGE), lambda s, pt: (0, 0)), tok_spec, tok_spec, tok_spec,
                  pl.BlockSpec((PAGE, PAGE), lambda s, pt: (0, 0))]
                 + [page_spec(p) for p in range(n_pages)] + [page_spec(p) for p in range(n_pages)],
        out_specs=tok_spec,
    )
    return pl.pallas_call(
        functools.partial(_attn_sample_kernel, n_pages, dec_seq),
        grid_spec=grid_spec,
        out_shape=jax.ShapeDtypeStruct((n_seq * dec_seq, SB_WIDTH), F32),
        compiler_params=_params("arbitrary"),
        name="attn_sample",
    )(page_table, bias_rows, q, k_new, v_new, _tri_incl(PAGE), *([cache_k] * n_pages), *([cache_v] * n_pages))


def _expand_matrix(reps):
    r = lax.broadcasted_iota(jnp.int32, (LANES, SSM_HEADS * reps), 0)
    c = lax.broadcasted_iota(jnp.int32, (LANES, SSM_HEADS * reps), 1)
    return (c // reps == r).astype(BF16)


def _ssd_core(act, dt_raw, z, valid, tri, e1, e2, dtb, a_row, dsk, nw, state_in, c_masks):
    xs = act[:, :SSM_INNER]
    bm = act[:, SSM_INNER:SSM_INNER + SSM_GROUPS * SSM_N]
    cm = act[:, SSM_INNER + SSM_GROUPS * SSM_N:]
    dt = _softplus(dt_raw + dtb)
    dta = dt * a_row
    acs = _dot3_r(tri, dta)
    acs_t = acs.T
    acs_e2 = _dot3_l(acs, e2)
    dt_e = _dot3_l(dt, e1)
    eacs_e = _dot3_l(jnp.exp(acs), e1)
    n_slots = len(state_in)
    dend_e = []
    cdec_e = []
    for s in range(n_slots):
        if c_masks[s] is None:
            last = acs[SSM_CHUNK - 1:SSM_CHUNK, :]
        else:
            last = jnp.min(jnp.where(c_masks[s], acs, jnp.inf), axis=0, keepdims=True)
        de = jnp.exp(last - acs)
        if c_masks[s] is not None:
            de = jnp.where(c_masks[s], de, 0.0)
        dend_e.append(_dot3_l(de, e1))
        cdec_e.append(_dot3_l(jnp.broadcast_to(jnp.exp(last), (SUBLANES, LANES)), e1)[0:1, :])
    bmb = [bm[:, g * SSM_N:(g + 1) * SSM_N] for g in range(SSM_GROUPS)]
    cmb = [cm[:, g * SSM_N:(g + 1) * SSM_N].astype(BF16) for g in range(SSM_GROUPS)]
    cb = [_dot_nt(cmb[g], bmb[g].astype(BF16)) for g in range(SSM_GROUPS)]
    bt = [bmb[g].T.astype(BF16) for g in range(SSM_GROUPS)]
    lane = lax.broadcasted_iota(jnp.int32, (SSM_CHUNK, LANES), 1)
    y_pairs = []
    new_states = [[None] * SSM_PAIRS for _ in range(n_slots)]
    for k in range(SSM_PAIRS):
        g = (2 * k) // (SSM_HEADS // SSM_GROUPS)
        sl = slice(k * LANES, (k + 1) * LANES)
        xdt = xs[:, sl] * dt_e[:, sl]
        xdt_b = xdt.astype(BF16)
        parts = []
        for hh in range(2):
            h = 2 * k + hh
            seg = acs_e2[:, h * LANES:(h + 1) * LANES] - acs_t[h:h + 1, :]
            decay = jnp.where(valid, jnp.exp(jnp.where(valid, seg, 0.0)), 0.0)
            parts.append(_dot((cb[g] * decay).astype(BF16), xdt_b))
        y = jnp.where(lane < SSM_P, parts[0], parts[1])
        y_off = None
        for s in range(n_slots):
            st = state_in[s][k]
            c_s = cmb[g] if c_masks[s] is None else jnp.where(c_masks[s], cmb[g], jnp.zeros_like(cmb[g]))
            contrib = _dot(c_s, st.astype(BF16))
            y_off = contrib if y_off is None else y_off + contrib
            xdd = (xdt * dend_e[s][:, sl]).astype(BF16)
            new_states[s][k] = st * cdec_e[s][:, sl] + _dot(bt[g], xdd)
        y = y + y_off * eacs_e[:, sl] + dsk[:, sl] * xs[:, sl]
        y_pairs.append(y)
    y = jnp.concatenate(y_pairs, axis=1) * _silu(z)
    gsz = SSM_INNER // SSM_GROUPS
    outs = []
    for g in range(SSM_GROUPS):
        yg = y[:, g * gsz:(g + 1) * gsz]
        outs.append(yg * lax.rsqrt(jnp.mean(yg * yg, axis=-1, keepdims=True) + EPS) * nw[:, g * gsz:(g + 1) * gsz])
    return jnp.concatenate(outs, axis=1), new_states


def _ssd_prompt_kernel(xbc_ref, z_ref, dt_ref, cw_ref, cbias_ref, dtb_ref, a_ref, dsk_ref, nw_ref, tri_ref, e1_ref,
                       e2_ref, y_ref, tail_ref, st_ref, xs_scr, st_scr):
    c = pl.program_id(1)

    @pl.when(c == 0)
    def _():
        xs_scr[0:SUBLANES, :] = jnp.zeros((SUBLANES, CONV_DIM), F32)
        st_scr[...] = jnp.zeros_like(st_scr)

    xs_scr[SUBLANES:SUBLANES + SSM_CHUNK, :] = xbc_ref[...]
    conv = cbias_ref[...]
    for i in range(CONV_W):
        off = SUBLANES - (CONV_W - 1) + i
        conv = conv + xs_scr[off:off + SSM_CHUNK, :] * cw_ref[i:i + 1, :]
    tail = xs_scr[SSM_CHUNK:SSM_CHUNK + SUBLANES, :]
    xs_scr[0:SUBLANES, :] = tail
    tail_ref[0] = tail
    act = _silu(conv)
    r = lax.broadcasted_iota(jnp.int32, (SSM_CHUNK, SSM_CHUNK), 0)
    cc = lax.broadcasted_iota(jnp.int32, (SSM_CHUNK, SSM_CHUNK), 1)
    valid = r >= cc
    state_in = [[st_scr[k] for k in range(SSM_PAIRS)]]
    y, new_states = _ssd_core(act, dt_ref[...], z_ref[...], valid, tri_ref[...], e1_ref[...], e2_ref[...],
                              dtb_ref[...], a_ref[...], dsk_ref[...], nw_ref[...], state_in, [None])
    for k in range(SSM_PAIRS):
        st_scr[k] = new_states[0][k]
        st_ref[0, k] = new_states[0][k]
    y_ref[...] = y.astype(y_ref.dtype)


def _ssd_consts(conv_w, conv_b, dt_bias, a_log, d_skip, ssm_norm_w):
    pad = LANES - SSM_HEADS
    cw = jnp.concatenate([conv_w, jnp.zeros((SUBLANES - CONV_W, CONV_DIM), F32)], axis=0)
    dtb = jnp.pad(dt_bias, (0, pad)).reshape(1, LANES)
    a_row = jnp.pad(-jnp.exp(a_log), (0, pad)).reshape(1, LANES)
    dsk = jnp.repeat(d_skip, SSM_P).reshape(1, SSM_INNER)
    return cw, conv_b.reshape(1, CONV_DIM), dtb, a_row, dsk, ssm_norm_w.reshape(1, SSM_INNER)


def _const_spec(shape, ngrid):
    zeros = (0,) * len(shape)
    if ngrid == 1:
        return pl.BlockSpec(shape, lambda i: zeros)
    return pl.BlockSpec(shape, lambda i, j: zeros)


def _ssd_prompt(xbc, z, dt, consts, batch, seq):
    nc = seq // SSM_CHUNK
    cw, cbias, dtb, a_row, dsk, nw = consts
    tok = lambda w: pl.BlockSpec((SSM_CHUNK, w), lambda b, c: (b * nc + c, 0))
    tri = _tri_incl(SSM_CHUNK)
    return pl.pallas_call(
        _ssd_prompt_kernel,
        grid=(batch, nc),
        in_specs=[tok(CONV_DIM), tok(SSM_INNER), tok(LANES),
                  _const_spec(cw.shape, 2), _const_spec(cbias.shape, 2), _const_spec(dtb.shape, 2),
                  _const_spec(a_row.shape, 2), _const_spec(dsk.shape, 2), _const_spec(nw.shape, 2),
                  _const_spec((SSM_CHUNK, SSM_CHUNK), 2), _const_spec((LANES, SSM_INNER), 2),
                  _const_spec((LANES, SSM_HEADS * LANES), 2)],
        out_specs=[tok(SSM_INNER),
                   pl.BlockSpec((1, SUBLANES, CONV_DIM), lambda b, c: (b, 0, 0)),
                   pl.BlockSpec((1, SSM_PAIRS, SSM_N, LANES), lambda b, c: (b, 0, 0, 0))],
        out_shape=[jax.ShapeDtypeStruct((batch * seq, SSM_INNER), BF16),
                   jax.ShapeDtypeStruct((batch, SUBLANES, CONV_DIM), F32),
                   jax.ShapeDtypeStruct((batch, SSM_PAIRS, SSM_N, LANES), F32)],
        scratch_shapes=[pltpu.VMEM((SSM_CHUNK + SUBLANES, CONV_DIM), F32),
                        pltpu.VMEM((SSM_PAIRS, SSM_N, LANES), F32)],
        compiler_params=_params("arbitrary", "arbitrary"),
        name="ssd_prompt",
    )(xbc, z, dt, cw, cbias, dtb, a_row, dsk, nw, tri, _expand_matrix(SSM_P), _expand_matrix(LANES))


def _ssd_sample_kernel(dec_seq, xbc_ref, prev_ref, z_ref, dt_ref, st_in_ref, cw_ref, cbias_ref, dtb_ref, a_ref,
                       dsk_ref, nw_ref, tri_ref, e1_ref, e2_ref, y_ref, st_ref):
    n_slots = SSM_CHUNK // dec_seq
    cur = xbc_ref[...]
    prev = prev_ref[...]
    pos = lax.broadcasted_iota(jnp.int32, (SSM_CHUNK, CONV_DIM), 0) % dec_seq
    conv = cbias_ref[...] + cur * cw_ref[CONV_W - 1:CONV_W, :]
    for k in range(1, CONV_W):
        shifted = jnp.where(pos < k, pltpu.roll(prev, SSM_CHUNK - dec_seq + k, axis=0), pltpu.roll(cur, k, axis=0))
        conv = conv + shifted * cw_ref[CONV_W - 1 - k:CONV_W - k, :]
    act = _silu(conv)
    r = lax.broadcasted_iota(jnp.int32, (SSM_CHUNK, SSM_CHUNK), 0)
    cc = lax.broadcasted_iota(jnp.int32, (SSM_CHUNK, SSM_CHUNK), 1)
    valid = (r >= cc) & ((r // dec_seq) == (cc // dec_seq))
    row_seq = lax.broadcasted_iota(jnp.int32, (SSM_CHUNK, LANES), 0) // dec_seq
    c_masks = [row_seq == s for s in range(n_slots)]
    state_in = [[st_in_ref[s, k] for k in range(SSM_PAIRS)] for s in range(n_slots)]
    y, new_states = _ssd_core(act, dt_ref[...], z_ref[...], valid, tri_ref[...], e1_ref[...], e2_ref[...],
                              dtb_ref[...], a_ref[...], dsk_ref[...], nw_ref[...], state_in, c_masks)
    for s in range(n_slots):
        for k in range(SSM_PAIRS):
            st_ref[s, k] = new_states[s][k]
    y_ref[...] = y.astype(y_ref.dtype)


def _ssd_sample(xbc, prev, z, dt, state_t, consts, dec_seq):
    n_tok = xbc.shape[0]
    n_slots = SSM_CHUNK // dec_seq
    nblk = n_tok // SSM_CHUNK
    cw, cbias, dtb, a_row, dsk, nw = consts
    tok = lambda w: pl.BlockSpec((SSM_CHUNK, w), lambda i: (i, 0))
    st_spec = pl.BlockSpec((n_slots, SSM_PAIRS, SSM_N, LANES), lambda i: (i, 0, 0, 0))
    r = lax.broadcasted_iota(jnp.int32, (SSM_CHUNK, SSM_CHUNK), 0)
    c = lax.broadcasted_iota(jnp.int32, (SSM_CHUNK, SSM_CHUNK), 1)
    tri = ((r >= c) & ((r // dec_seq) == (c // dec_seq))).astype(BF16)
    return pl.pallas_call(
        functools.partial(_ssd_sample_kernel, dec_seq),
        grid=(nblk,),
        in_specs=[tok(CONV_DIM), tok(CONV_DIM), tok(SSM_INNER), tok(LANES), st_spec,
                  _const_spec(cw.shape, 1), _const_spec(cbias.shape, 1), _const_spec(dtb.shape, 1),
                  _const_spec(a_row.shape, 1), _const_spec(dsk.shape, 1), _const_spec(nw.shape, 1),
                  _const_spec((SSM_CHUNK, SSM_CHUNK), 1), _const_spec((LANES, SSM_INNER), 1),
                  _const_spec((LANES, SSM_HEADS * LANES), 1)],
        out_specs=[tok(SSM_INNER), st_spec],
        out_shape=[jax.ShapeDtypeStruct((n_tok, SSM_INNER), BF16),
                   jax.ShapeDtypeStruct(state_t.shape, F32)],
        compiler_params=_params("arbitrary"),
        name="ssd_sample",
    )(xbc, prev, z, dt, state_t, cw, cbias, dtb, a_row, dsk, nw, tri, _expand_matrix(SSM_P), _expand_matrix(LANES))


def _merge_kernel(o_ref, y_ref, ga_ref, gs_ref, x_ref, g1_ref, sh_ref, sc_ref, nw_ref, wa_ref, ws_ref, wo_ref,
                  x1_ref, h2_ref):
    merged = (jax.nn.sigmoid(ga_ref[...]) * _dot(o_ref[...].astype(BF16), wa_ref[...])
              + jax.nn.sigmoid(gs_ref[...]) * _dot(y_ref[...], ws_ref[...]))
    mix = _dot(merged.astype(BF16), wo_ref[...])
    x = x_ref[...]
    g, r, d = x.shape
    x1 = x + g1_ref[...] * mix.reshape(g, r, d)
    x1_ref[...] = x1
    y = x1 * lax.rsqrt(jnp.mean(x1 * x1, axis=-1, keepdims=True) + EPS) * nw_ref[...]
    h2 = y * (1.0 + sc_ref[...]) + sh_ref[...]
    h2_ref[...] = h2.reshape(g * r, d).astype(h2_ref.dtype)


def _mod_map(gpb, blocks_per_mod, col):
    if gpb == 1:
        return lambda i: (i // blocks_per_mod, 0, col)
    return lambda i: (i, 0, col)


def _merge(o_attn, y_ssm, ga, gs, x3, mod3, norm2_w, wa, ws, wo, groups_per_block, blocks_per_mod):
    ng, r, d = x3.shape
    gpb = groups_per_block
    nblk = ng // gpb
    tm = gpb * r
    t = ng * r
    tok = lambda w: pl.BlockSpec((tm, w), lambda i: (i, 0))
    xspec = pl.BlockSpec((gpb, r, d), lambda i: (i, 0, 0))
    mspec = lambda col: pl.BlockSpec((gpb, 1, d), _mod_map(gpb, blocks_per_mod, col))
    return pl.pallas_call(
        _merge_kernel,
        grid=(nblk,),
        in_specs=[tok(SB_WIDTH), tok(SSM_INNER), tok(d), tok(d), xspec, mspec(2), mspec(3), mspec(4),
                  pl.BlockSpec((1, 1, d), lambda i: (0, 0, 0)),
                  _const_spec(wa.shape, 1), _const_spec(ws.shape, 1), _const_spec(wo.shape, 1)],
        out_specs=[xspec, tok(d)],
        out_shape=[jax.ShapeDtypeStruct((ng, r, d), F32), jax.ShapeDtypeStruct((t, d), BF16)],
        compiler_params=_params("arbitrary"),
        name="merge",
    )(o_attn, y_ssm, ga, gs, x3, mod3, mod3, mod3, norm2_w.reshape(1, 1, d), wa, ws, wo)


def _batcher_pairs(n):
    pairs = []
    p = 1
    while p < n:
        k = p
        while k >= 1:
            for j in range(k % p, n - k, 2 * k):
                for i in range(min(k, n - j - k)):
                    if (i + j) // (2 * p) == (i + j + k) // (2 * p):
                        pairs.append((i + j, i + j + k))
            k //= 2
        p *= 2
    return pairs


_SORT16 = _batcher_pairs(PEER_TOPK)


def _sort_desc(vals):
    vals = list(vals)
    for i, j in _SORT16:
        hi = jnp.maximum(vals[i], vals[j])
        lo = jnp.minimum(vals[i], vals[j])
        vals[i], vals[j] = hi, lo
    return vals


def _merge_top(a, b, sort_result=True):
    n = PEER_TOPK
    t = [jnp.maximum(a[k], b[n - 1 - k]) for k in range(n)]
    if not sort_result:
        return t
    d = n // 2
    while d >= 1:
        for k in range(n):
            if (k // d) % 2 == 0:
                hi = jnp.maximum(t[k], t[k + d])
                lo = jnp.minimum(t[k], t[k + d])
                t[k], t[k + d] = hi, lo
        d //= 2
    return t


def _top16_desc(vals, need_sorted=True):
    groups = [_sort_desc(vals[i:i + PEER_TOPK]) for i in range(0, len(vals), PEER_TOPK)]
    while len(groups) > 1:
        nxt = []
        for i in range(0, len(groups) - 1, 2):
            last = len(groups) == 2
            nxt.append(_merge_top(groups[i], groups[i + 1], sort_result=need_sorted or not last))
        if len(groups) % 2:
            nxt.append(groups[-1])
        groups = nxt
    return groups[0]


ROUTE_TM = 256


def _route_kernel(h_ref, wq_ref, k1_ref, k2_ref, th_ref, p_ref, code_ref, r_ref, s1_scr, s2_scr):
    half = PEER_HEADS * PEER_HALF
    qt = _dot_nt(wq_ref[...], h_ref[...])
    s1 = _dot(k1_ref[...], qt[:half].astype(BF16))
    s2 = _dot(k2_ref[...], qt[half:].astype(BF16))
    for c in range(ROUTE_TM // LANES):
        s1_scr[c] = s1[:, c * LANES:(c + 1) * LANES]
        s2_scr[c] = s2[:, c * LANES:(c + 1) * LANES]
    staircase = [(a, b) for a in range(PEER_TOPK) for b in range(PEER_TOPK) if (a + 1) * (b + 1) <= PEER_TOPK]
    neg_inf = jnp.full((SUBLANES, LANES), -jnp.inf, F32)
    for c in range(ROUTE_TM // LANES):
        tl = pl.ds(c * LANES, LANES)
        row = lambda scr, i, c=c: scr[c, i * SUBLANES:(i + 1) * SUBLANES, :]
        top1 = _top16_desc([row(s1_scr, i) for i in range(PEER_KEYS)])
        top2 = _top16_desc([row(s2_scr, i) for i in range(PEER_KEYS)])
        cand = {(a, b): top1[a] + top2[b] for a, b in staircase}
        cvals = [cand[ab] for ab in staircase]
        cvals += [neg_inf] * (-len(cvals) % PEER_TOPK)
        tau = _top16_desc(cvals, need_sorted=False)
        thr = tau[0]
        for v in tau[1:]:
            thr = jnp.minimum(thr, v)
        cmax = cand[(0, 0)]
        zsum = jnp.zeros((SUBLANES, LANES), F32)
        never = jnp.full((SUBLANES, LANES), PEER_TOPK + 1.0, F32)
        need = [never] * PEER_TOPK
        for a, b in staircase:
            ok = cand[(a, b)] >= thr
            zsum = zsum + jnp.where(ok, jnp.exp(cand[(a, b)] - cmax), 0.0)
            need[a] = need[a] - jnp.where(ok, 1.0, 0.0)
        inv_z = 1.0 / zsum
        for i in range(PEER_KEYS):
            s = row(s1_scr, i)
            th = never
            for a in range(PEER_TOPK):
                th = jnp.where(s == top1[a], need[a], th)
            th_ref[i, :, tl] = th
            p_ref[i, :, tl] = jnp.exp(s - top1[0]) * inv_z
        for h in range(PEER_HEADS):
            s2h = s2_scr[c, pl.ds(h, PEER_KEYS, stride=SUBLANES), :]
            code = jnp.zeros((PEER_KEYS, LANES), F32)
            for b in range(PEER_TOPK):
                code = code + jnp.where(s2h >= top2[b][h:h + 1, :], 1.0, 0.0)
            tiles = (PEER_KEYS // BF16_ROWS, BF16_ROWS, LANES)
            code_ref[h, :, :, tl] = code.reshape(tiles).astype(BF16)
            r_ref[h, :, :, tl] = jnp.exp(s2h - top2[0][h:h + 1, :]).reshape(tiles).astype(BF16)


def _route(h2, wq_t, kbd1, kbd2):
    t = h2.shape[0]
    tm = ROUTE_TM
    half = PEER_HEADS * PEER_HALF
    kt = PEER_KEYS // BF16_ROWS
    return pl.pallas_call(
        _route_kernel,
        grid=(t // tm,),
        in_specs=[pl.BlockSpec((tm, D_MODEL), lambda i: (i, 0)),
                  _const_spec(wq_t.shape, 1), _const_spec(kbd1.shape, 1), _const_spec(kbd2.shape, 1)],
        out_specs=[pl.BlockSpec((PEER_KEYS, PEER_HEADS, tm), lambda i: (0, 0, i)),
                   pl.BlockSpec((PEER_KEYS, PEER_HEADS, tm), lambda i: (0, 0, i)),
                   pl.BlockSpec((PEER_HEADS, kt, BF16_ROWS, tm), lambda i: (0, 0, 0, i)),
                   pl.BlockSpec((PEER_HEADS, kt, BF16_ROWS, tm), lambda i: (0, 0, 0, i))],
        out_shape=[jax.ShapeDtypeStruct((PEER_KEYS, PEER_HEADS, t), F32),
                   jax.ShapeDtypeStruct((PEER_KEYS, PEER_HEADS, t), F32),
                   jax.ShapeDtypeStruct((PEER_HEADS, kt, BF16_ROWS, t), BF16),
                   jax.ShapeDtypeStruct((PEER_HEADS, kt, BF16_ROWS, t), BF16)],
        scratch_shapes=[pltpu.VMEM((tm // LANES, half, LANES), F32), pltpu.VMEM((tm // LANES, half, LANES), F32)],
        compiler_params=_params("arbitrary"),
        name="peer_route",
    )(h2, wq_t, kbd1, kbd2)


PEER_TM = 1024
PEER_KB = 8
PEER_SUB = 8
BF16_ROWS = 16


def _gelu_tanh(x):
    return 0.5 * x * (1.0 + jnp.tanh(math.sqrt(2.0 / math.pi) * (x + 0.044715 * (x * x * x))))


def _peer_kernel(h_ref, u_ref, vt_ref, th_ref, p_ref, code_ref, r_ref, o_ref, acc_ref):
    j = pl.program_id(1)
    tm = h_ref.shape[0]
    kt = PEER_KEYS // BF16_ROWS

    @pl.when(j == 0)
    def _():
        acc_ref[...] = jnp.zeros_like(acc_ref)

    zero = jnp.zeros((), BF16)
    h = h_ref[...]
    total = None
    for sb in range(PEER_KB // PEER_SUB):
        rows = slice(sb * PEER_SUB * PEER_KEYS, (sb + 1) * PEER_SUB * PEER_KEYS)
        act = _gelu_tanh(_dot_nt(u_ref[rows, :], h))
        act = act.reshape(PEER_SUB * kt, BF16_ROWS, tm).astype(BF16)
        parts = []
        for s in range(PEER_SUB):
            ii = sb * PEER_SUB + s
            w = None
            for hd in range(PEER_HEADS):
                need = jnp.broadcast_to(th_ref[ii, hd:hd + 1, :], (BF16_ROWS, tm)).astype(BF16)
                pr = jnp.broadcast_to(p_ref[ii, hd:hd + 1, :], (BF16_ROWS, tm)).astype(BF16)
                sel = jnp.where(code_ref[hd] >= need[None], r_ref[hd], zero) * pr[None]
                w = sel if w is None else w + sel
            parts.append(w * act[s * kt:(s + 1) * kt])
        a = jnp.concatenate(parts, axis=0).reshape(PEER_SUB * PEER_KEYS, tm)
        contrib = _dot(vt_ref[:, rows], a)
        total = contrib if total is None else total + contrib
    acc_ref[...] += total

    @pl.when(j == pl.num_programs(1) - 1)
    def _():
        o_ref[...] = acc_ref[...].T


def _peer_dense(h2, u_b, vt_b, th, p, code, r):
    t = h2.shape[0]
    tm = min(PEER_TM, t)
    eb = PEER_KB * PEER_KEYS
    nj = PEER_KEYS // PEER_KB
    kt = PEER_KEYS // BF16_ROWS
    mask_spec = pl.BlockSpec((PEER_HEADS, kt, BF16_ROWS, tm), lambda i, j: (0, 0, 0, i))
    return pl.pallas_call(
        _peer_kernel,
        grid=(t // tm, nj),
        in_specs=[pl.BlockSpec((tm, D_MODEL), lambda i, j: (i, 0)),
                  pl.BlockSpec((eb, D_MODEL), lambda i, j: (j, 0)),
                  pl.BlockSpec((D_MODEL, eb), lambda i, j: (0, j)),
                  pl.BlockSpec((PEER_KB, PEER_HEADS, tm), lambda i, j: (j, 0, i)),
                  pl.BlockSpec((PEER_KB, PEER_HEADS, tm), lambda i, j: (j, 0, i)),
                  mask_spec, mask_spec],
        out_specs=pl.BlockSpec((tm, D_MODEL), lambda i, j: (i, 0)),
        out_shape=jax.ShapeDtypeStruct((t, D_MODEL), F32),
        scratch_shapes=[pltpu.VMEM((D_MODEL, tm), F32)],
        compiler_params=_params("arbitrary", "arbitrary"),
        name="peer_dense",
    )(h2, u_b, vt_b, th, p, code, r)


def _final_kernel(x_ref, pe_ref, g2_ref, nw_ref, y_ref):
    x1 = x_ref[...]
    g, r, d = x1.shape
    x2 = x1 + g2_ref[...] * pe_ref[...].reshape(g, r, d)
    y_ref[...] = x2 * lax.rsqrt(jnp.mean(x2 * x2, axis=-1, keepdims=True) + EPS) * nw_ref[...]


def _final(x1_3, peer_out, mod3, norm_f_w, groups_per_block, blocks_per_mod):
    ng, r, d = x1_3.shape
    gpb = groups_per_block
    nblk = ng // gpb
    tm = gpb * r
    xspec = pl.BlockSpec((gpb, r, d), lambda i: (i, 0, 0))
    return pl.pallas_call(
        _final_kernel,
        grid=(nblk,),
        in_specs=[xspec, pl.BlockSpec((tm, d), lambda i: (i, 0)),
                  pl.BlockSpec((gpb, 1, d), _mod_map(gpb, blocks_per_mod, 5)),
                  pl.BlockSpec((1, 1, d), lambda i: (0, 0, 0))],
        out_specs=xspec,
        out_shape=jax.ShapeDtypeStruct((ng, r, d), F32),
        compiler_params=_params("arbitrary"),
        name="final_norm",
    )(x1_3, peer_out, mod3, norm_f_w.reshape(1, 1, d))


def _pack_w_in(w_in):
    q_end = 3 * SB_WIDTH + SSM_INNER + CONV_DIM
    dt_w = jnp.pad(w_in[:, q_end:q_end + SSM_HEADS], ((0, 0), (0, LANES - SSM_HEADS)))
    return jnp.concatenate([w_in[:, :q_end], dt_w, w_in[:, q_end + SSM_HEADS:]], axis=1).astype(BF16)


def _pack_peer(peer_w_query, peer_keys1, peer_keys2):
    d = peer_w_query.shape[0]
    wq = peer_w_query.reshape(d, PEER_HEADS, 2, PEER_HALF)
    wq_t = jnp.transpose(wq, (2, 1, 3, 0)).reshape(2 * PEER_HEADS * PEER_HALF, d).astype(BF16)
    eye = jnp.eye(PEER_HEADS, dtype=F32)

    def block_diag(keys):
        return jnp.einsum("kd,hg->khgd", keys, eye).reshape(PEER_KEYS * PEER_HEADS, PEER_HEADS * PEER_HALF).astype(BF16)

    return wq_t, block_diag(peer_keys1), block_diag(peer_keys2)


def _layer_tail(o_attn, y_ssm, proj, x3, mod3, lw, gpb, bpm):
    x1_3, h2 = _merge(o_attn, y_ssm, proj["ga"], proj["gs"], x3, mod3, lw["norm2_w"], lw["wa"], lw["ws"], lw["wo"],
                      gpb, bpm)
    th, p, s2t, r = _route(h2, lw["wq_t"], lw["kbd1"], lw["kbd2"])
    peer_out = _peer_dense(h2, lw["u_b"], lw["vt_b"], th, p, s2t, r)
    return _final(x1_3, peer_out, mod3, lw["norm_f_w"], gpb, bpm)


def _proj_dict(outs):
    names = ("q", "k", "v", "kb", "vb", "z", "xbc", "dt", "ga", "gs")
    return dict(zip(names, outs))


INPROJ_TM = 256


def kernel(x_prompt, x_sample, cache_k, cache_v, state_conv, state_ssm, page_table, c_prompt, c_sample, norm1_w,
           norm2_w, w_ada, b_ada, w_in, sb_bias, conv_w, conv_b, dt_bias, a_log, d_skip, ssm_norm_w, w_br_attn,
           w_br_ssm, w_out, peer_w_query, peer_keys1, peer_keys2, peer_u, peer_v, norm_f_w):
    batch, seq, d = x_prompt.shape
    n_seq, dec_seq, _ = x_sample.shape
    assert w_ada.shape[0] == 1, "single layer"
    lw = {"norm2_w": norm2_w[0], "wa": w_br_attn[0].astype(BF16), "ws": w_br_ssm[0].astype(BF16),
          "wo": w_out[0].astype(BF16), "norm_f_w": norm_f_w,
          "u_b": peer_u[0].astype(BF16), "vt_b": peer_v[0].T.astype(BF16)}
    lw["wq_t"], lw["kbd1"], lw["kbd2"] = _pack_peer(peer_w_query[0], peer_keys1[0], peer_keys2[0])
    w_packed = _pack_w_in(w_in[0])
    consts = _ssd_consts(conv_w[0], conv_b[0], dt_bias[0], a_log[0], d_skip[0], ssm_norm_w[0])

    mod = _adaln(jnp.concatenate([c_prompt, c_sample], axis=0), w_ada[0], b_ada[0])
    mod_p = mod[:batch].reshape(batch, 1, 6 * d)
    mod_s = mod[batch:].reshape(n_seq, 1, 6 * d)

    tm = INPROJ_TM
    bpm = seq // tm
    xp3 = x_prompt.reshape(batch * bpm, tm, d)
    pp = _proj_dict(_inproj(xp3, mod_p, norm1_w[0], w_packed, 1, bpm, kv_transposed=True))
    o_attn_p = _attn_prompt(pp["q"], pp["kb"], pp["vb"], sb_bias[0], batch, seq)
    y_ssm_p, tail_p, st_p = _ssd_prompt(pp["xbc"], pp["z"], pp["dt"], consts, batch, seq)
    y_prompt = _layer_tail(o_attn_p, y_ssm_p, pp, xp3, mod_p, lw, 1, bpm).reshape(batch, seq, d)

    gpb = tm // dec_seq
    ps = _proj_dict(_inproj(x_sample, mod_s, norm1_w[0], w_packed, gpb, 1))
    o_attn_s = _attn_sample(ps["q"], ps["k"], ps["v"], cache_k[0].transpose(0, 2, 3, 1),
                            cache_v[0].transpose(0, 2, 3, 1), page_table, sb_bias[0], dec_seq)
    prev = jnp.pad(state_conv[0], ((0, 0), (dec_seq - (CONV_W - 1), 0), (0, 0))).reshape(n_seq * dec_seq, CONV_DIM)
    st_in = state_ssm[0].reshape(n_seq, SSM_PAIRS, 2 * SSM_P, SSM_N).transpose(0, 1, 3, 2)
    y_ssm_s, st_s = _ssd_sample(ps["xbc"], prev, ps["z"], ps["dt"], st_in, consts, dec_seq)
    y_sample = _layer_tail(o_attn_s, y_ssm_s, ps, x_sample, mod_s, lw, gpb, 1)

    def state_out(st):
        b = st.shape[0]
        return st.transpose(0, 1, 3, 2).reshape(1, b, SSM_HEADS, SSM_P, SSM_N)

    k_prompt = pp["k"].reshape(1, batch, SB_HEADS, SB_HEAD_DIM, seq).transpose(0, 1, 4, 2, 3)
    v_prompt = pp["v"].reshape(1, batch, SB_HEADS, SB_HEAD_DIM, seq).transpose(0, 1, 4, 2, 3)
    conv_prompt = tail_p[:, SUBLANES - (CONV_W - 1):, :][None]
    k_sample = ps["k"].reshape(1, n_seq, dec_seq, SB_HEADS, SB_HEAD_DIM)
    v_sample = ps["v"].reshape(1, n_seq, dec_seq, SB_HEADS, SB_HEAD_DIM)
    conv_sample = ps["xbc"].reshape(n_seq, dec_seq, CONV_DIM)[:, dec_seq - (CONV_W - 1):, :][None]
    return (y_prompt, y_sample, k_prompt, v_prompt, conv_prompt, state_out(st_p), k_sample, v_sample, conv_sample,
            state_out(st_s))
```

```python
import functools
import math

import jax
import jax.numpy as jnp
from jax import lax
from jax.experimental import pallas as pl
from jax.experimental.pallas import tpu as pltpu

F32 = jnp.float32
BF16 = jnp.bfloat16

D_MODEL = 1024
EPS = 1e-6
SB_HEADS = 8
SB_HEAD_DIM = 64
SB_WIDTH = SB_HEADS * SB_HEAD_DIM
PAGE = 128
SSM_HEADS = 16
SSM_P = 64
SSM_GROUPS = 2
SSM_N = 128
SSM_INNER = SSM_HEADS * SSM_P
SSM_PAIRS = SSM_HEADS // 2
CONV_W = 4
CONV_DIM = SSM_INNER + 2 * SSM_GROUPS * SSM_N
SSM_CHUNK = 128
PEER_HEADS = 8
PEER_KEYS = 128
PEER_HALF = 128
PEER_TOPK = 16

LANES = 128
SUBLANES = 8
VMEM_LIMIT = 56 * 1024 * 1024

_PROJ_GROUPS = (("q", SB_WIDTH), ("k", SB_WIDTH), ("v", SB_WIDTH), ("z", SSM_INNER), ("xbc", CONV_DIM),
                ("dt", LANES), ("ga", D_MODEL), ("gs", D_MODEL))
_PROJ_COLS = sum(w for _, w in _PROJ_GROUPS)


def _params(*sem):
    return pltpu.CompilerParams(dimension_semantics=sem, vmem_limit_bytes=VMEM_LIMIT)


def _dot(a, b):
    return jnp.dot(a, b, preferred_element_type=F32)


def _dot_nt(a, b):
    return lax.dot_general(a, b, (((1,), (1,)), ((), ())), preferred_element_type=F32)


def _split3(x):
    hi = x.astype(BF16)
    r = x - hi.astype(F32)
    mid = r.astype(BF16)
    lo = (r - mid.astype(F32)).astype(BF16)
    return hi, mid, lo


def _dot3_l(x, m):
    hi, mid, lo = _split3(x)
    return _dot(hi, m) + _dot(mid, m) + _dot(lo, m)


def _dot3_r(m, x):
    hi, mid, lo = _split3(x)
    return _dot(m, hi) + _dot(m, mid) + _dot(m, lo)


def _softplus(x):
    return jnp.maximum(x, 0.0) + jnp.log1p(jnp.exp(-jnp.abs(x)))


def _silu(x):
    return x * jax.nn.sigmoid(x)


def _adaln_kernel(c_ref, w_ref, b_ref, o_ref):
    s = _silu(c_ref[...]).astype(BF16)
    o_ref[...] = _dot(s, w_ref[...].astype(BF16)) + b_ref[...]


def _adaln(c_all, w_ada, b_ada):
    m, n = c_all.shape[0], w_ada.shape[1]
    tn = 1024
    return pl.pallas_call(
        _adaln_kernel,
        grid=(n // tn,),
        in_specs=[pl.BlockSpec((m, D_MODEL), lambda j: (0, 0)),
                  pl.BlockSpec((D_MODEL, tn), lambda j: (0, j)),
                  pl.BlockSpec((1, tn), lambda j: (0, j))],
        out_specs=pl.BlockSpec((m, tn), lambda j: (0, j)),
        out_shape=jax.ShapeDtypeStruct((m, n), F32),
        compiler_params=_params("arbitrary"),
        name="adaln",
    )(c_all, w_ada, b_ada.reshape(1, n))


def _inproj_kernel(kv_transposed, x_ref, sh_ref, sc_ref, nw_ref, w_ref, q_ref, k_ref, v_ref, kb_ref, vb_ref, z_ref,
                   xbc_ref, dt_ref, ga_ref, gs_ref):
    x = x_ref[...]
    g, r, d = x.shape
    y = x * lax.rsqrt(jnp.mean(x * x, axis=-1, keepdims=True) + EPS) * nw_ref[...]
    h = y * (1.0 + sc_ref[...]) + sh_ref[...]
    hb = h.reshape(g * r, d).astype(BF16)
    outs = {"q": q_ref, "k": k_ref, "v": v_ref, "z": z_ref, "xbc": xbc_ref, "dt": dt_ref, "ga": ga_ref, "gs": gs_ref}
    col = 0
    for name, width in _PROJ_GROUPS:
        res = _dot(hb, w_ref[:, col:col + width])
        col += width
        if name == "q":
            res = res * (SB_HEAD_DIM ** -0.5)
        if name in ("k", "v") and kv_transposed:
            outs[name][0] = res.T
        else:
            outs[name][...] = res.astype(outs[name].dtype)
        if name == "k":
            kb_ref[...] = res.astype(BF16)
        if name == "v":
            vb_ref[...] = res.astype(BF16)


def _inproj(x3, mod3, norm_w, w_packed, groups_per_block, blocks_per_mod, kv_transposed=False):
    ng, r, d = x3.shape
    gpb = groups_per_block
    nblk = ng // gpb
    tm = gpb * r
    t = ng * r
    if gpb == 1:
        mod_map = lambda col: (lambda i: (i // blocks_per_mod, 0, col))
    else:
        mod_map = lambda col: (lambda i: (i, 0, col))
    widths = dict(_PROJ_GROUPS)
    out_shapes = [
        jax.ShapeDtypeStruct((t, widths["q"]), F32), jax.ShapeDtypeStruct((t, widths["k"]), F32),
        jax.ShapeDtypeStruct((t, widths["v"]), F32), jax.ShapeDtypeStruct((t, widths["k"]), BF16),
        jax.ShapeDtypeStruct((t, widths["v"]), BF16), jax.ShapeDtypeStruct((t, widths["z"]), F32),
        jax.ShapeDtypeStruct((t, widths["xbc"]), F32), jax.ShapeDtypeStruct((t, widths["dt"]), F32),
        jax.ShapeDtypeStruct((t, widths["ga"]), F32), jax.ShapeDtypeStruct((t, widths["gs"]), F32)]
    out_specs = [pl.BlockSpec((tm, s.shape[1]), lambda i: (i, 0)) for s in out_shapes]
    if kv_transposed:
        assert gpb == 1
        for idx in (1, 2):
            out_shapes[idx] = jax.ShapeDtypeStruct((ng // blocks_per_mod, SB_WIDTH, blocks_per_mod * r), F32)
            out_specs[idx] = pl.BlockSpec((1, SB_WIDTH, tm), lambda i: (i // blocks_per_mod, 0, i % blocks_per_mod))
    return pl.pallas_call(
        functools.partial(_inproj_kernel, kv_transposed),
        grid=(nblk,),
        in_specs=[pl.BlockSpec((gpb, r, d), lambda i: (i, 0, 0)),
                  pl.BlockSpec((gpb, 1, d), mod_map(0)),
                  pl.BlockSpec((gpb, 1, d), mod_map(1)),
                  pl.BlockSpec((1, 1, d), lambda i: (0, 0, 0)),
                  pl.BlockSpec((d, _PROJ_COLS), lambda i: (0, 0))],
        out_specs=out_specs,
        out_shape=out_shapes,
        compiler_params=_params("arbitrary"),
        name="inproj",
    )(x3, mod3, mod3, norm_w.reshape(1, 1, d), w_packed)


ATT_TQ = 512
ATT_TK = 512
ATT_HEADS = 2


def _stick_incl(z, mask, tri):
    sp = jnp.maximum(z, 0.0) + jnp.log(1.0 + jnp.exp(-jnp.abs(z)))
    if mask is not None:
        sp = jnp.where(mask, sp, 0.0)
    hi = sp.astype(BF16)
    lo = (sp - hi.astype(F32)).astype(BF16)
    n = z.shape[0]
    both = _dot(jnp.concatenate([hi, lo], axis=0), tri)
    return both[:n] + both[n:]


def _stick_tile(z, mask, tri, cum):
    incl = _stick_incl(z, mask, tri)
    w = jnp.exp(z - incl - cum)
    if mask is not None:
        w = jnp.where(mask, w, 0.0)
    return w, cum + incl[:, 0:1]


def _attn_prompt_kernel(bias_ref, q_ref, k_ref, v_ref, tri_ref, o_ref):
    hg = pl.program_id(1)
    qi = pl.program_id(2)
    tq, tk = ATT_TQ, ATT_TK
    ratio = tq // tk
    width = ATT_HEADS * SB_HEAD_DIM
    q = q_ref[...].astype(BF16)
    lane_head = lax.broadcasted_iota(jnp.int32, (tq, width), 1) // SB_HEAD_DIM
    rel = lax.broadcasted_iota(jnp.int32, (tq, tk), 0) - lax.broadcasted_iota(jnp.int32, (tq, tk), 1)
    tri = tri_ref[...]
    qhs = [jnp.where(lane_head == hh, q, jnp.zeros_like(q)) for hh in range(ATT_HEADS)]
    biases = [bias_ref[hg * ATT_HEADS + hh] for hh in range(ATT_HEADS)]

    def block(kb, carry, mask):
        off = pl.multiple_of(kb * tk, tk)
        kblk = k_ref[pl.ds(off, tk), :]
        vblk = v_ref[pl.ds(off, tk), :]
        out = []
        for hh in range(ATT_HEADS):
            acc, cum = carry[hh]
            z = _dot_nt(qhs[hh], kblk) + biases[hh]
            w, cum = _stick_tile(z, mask, tri, cum)
            out.append((acc + _dot(w.astype(BF16), vblk), cum))
        return tuple(out)

    carry = tuple((jnp.zeros((tq, width), F32), jnp.zeros((tq, 1), F32)) for _ in range(ATT_HEADS))
    for d in range(ratio - 1, -1, -1):
        carry = block(qi * ratio + d, carry, rel > d * tk)
    res = lax.fori_loop(0, qi * ratio, lambda step, c: block(qi * ratio - 1 - step, c, None), carry)
    o = res[0][0]
    for hh in range(1, ATT_HEADS):
        o = jnp.where(lane_head == hh, res[hh][0], o)
    o_ref[...] = o.astype(o_ref.dtype)


def _tri_incl(n):
    r = lax.broadcasted_iota(jnp.int32, (n, n), 0)
    c = lax.broadcasted_iota(jnp.int32, (n, n), 1)
    return (r >= c).astype(BF16)


def _attn_prompt(q, kb, vb, sb_bias, batch, seq):
    tq = ATT_TQ
    nq = seq // tq
    width = ATT_HEADS * SB_HEAD_DIM
    return pl.pallas_call(
        _attn_prompt_kernel,
        grid=(batch, SB_HEADS // ATT_HEADS, nq),
        in_specs=[pl.BlockSpec(memory_space=pltpu.SMEM),
                  pl.BlockSpec((tq, width), lambda b, h, i: (b * nq + i, h)),
                  pl.BlockSpec((seq, width), lambda b, h, i: (b, h)),
                  pl.BlockSpec((seq, width), lambda b, h, i: (b, h)),
                  pl.BlockSpec((ATT_TK, ATT_TK), lambda b, h, i: (0, 0))],
        out_specs=pl.BlockSpec((tq, width), lambda b, h, i: (b * nq + i, h)),
        out_shape=jax.ShapeDtypeStruct((batch * seq, SB_WIDTH), BF16),
        compiler_params=_params("arbitrary", "arbitrary", "arbitrary"),
        name="attn_prompt",
    )(sb_bias, q, kb, vb, _tri_incl(ATT_TK))


def _attn_sample_kernel(n_pages, dec_seq, pt_ref, bias_ref, q_ref, kn_ref, vn_ref, tri_ref, *refs):
    k_pages = refs[:n_pages]
    v_pages = refs[n_pages:2 * n_pages]
    o_ref = refs[2 * n_pages]
    rows = SB_HEADS * dec_seq
    q = q_ref[...]
    qt = jnp.concatenate([q] * SB_HEADS, axis=0)
    row_head = lax.broadcasted_iota(jnp.int32, (rows, SB_WIDTH), 0) // dec_seq
    lane_head = lax.broadcasted_iota(jnp.int32, (rows, SB_WIDTH), 1) // SB_HEAD_DIM
    qbd = jnp.where(row_head == lane_head, qt, 0.0).astype(BF16)
    bias = bias_ref[...]
    tri = tri_ref[...]
    r_i = lax.broadcasted_iota(jnp.int32, (rows, PAGE), 0) % dec_seq
    c_i = lax.broadcasted_iota(jnp.int32, (rows, PAGE), 1)
    new_valid = c_i < r_i

    def pad_rows(a):
        return jnp.concatenate([a, jnp.zeros((PAGE - dec_seq, SB_WIDTH), a.dtype)], axis=0)

    def page_tile(ref):
        return ref[0].reshape(SB_WIDTH, PAGE).astype(BF16)

    zs = [_dot_nt(qbd, pad_rows(kn_ref[...]).astype(BF16)) + bias]
    zs += [_dot(qbd, page_tile(k_pages[p])) + bias for p in range(n_pages - 1, -1, -1)]
    masks = [new_valid] + [None] * n_pages
    incls = [_stick_incl(z, m, tri) for z, m in zip(zs, masks)]
    cum = jnp.zeros((rows, 1), F32)
    acc = None
    for b, (z, m, incl) in enumerate(zip(zs, masks, incls)):
        w = jnp.exp(z - incl - cum)
        cum = cum + incl[:, 0:1]
        if m is not None:
            w = jnp.where(m, w, 0.0)
        wb = w.astype(BF16)
        if b == 0:
            contrib = _dot(wb, pad_rows(vn_ref[...]).astype(BF16))
        else:
            contrib = _dot_nt(wb, page_tile(v_pages[n_pages - b]))
        acc = contrib if acc is None else acc + contrib
    out = jnp.zeros((dec_seq, SB_WIDTH), F32)
    lane_head8 = lax.broadcasted_iota(jnp.int32, (dec_seq, SB_WIDTH), 1) // SB_HEAD_DIM
    for h in range(SB_HEADS):
        out = out + jnp.where(lane_head8 == h, acc[h * dec_seq:(h + 1) * dec_seq, :], 0.0)
    o_ref[...] = out.astype(o_ref.dtype)


def _attn_sample(q, k_new, v_new, cache_k, cache_v, page_table, sb_bias, dec_seq):
    n_seq, n_pages = page_table.shape
    rows = SB_HEADS * dec_seq
    bias_rows = jnp.broadcast_to(jnp.repeat(sb_bias, dec_seq)[:, None], (rows, PAGE))

    def page_spec(p):
        return pl.BlockSpec((1, SB_HEADS, SB_HEAD_DIM, PAGE), lambda s, pt: (pt[s, p], 0, 0, 0))

    tok_spec = pl.BlockSpec((dec_seq, SB_WIDTH), lambda s, pt: (s, 0))
    grid_spec = pltpu.PrefetchScalarGridSpec(
        num_scalar_prefetch=1,
        grid=(n_seq,),
        in_specs=[pl.BlockSpec((rows, PAGE), lambda s, pt: (0, 0)), tok_spec, tok_spec, tok_spec,
                  pl.BlockSpec((PAGE, PAGE), lambda s, pt: (0, 0))]
                 + [page_spec(p) for p in range(n_pages)] + [page_spec(p) for p in range(n_pages)],
        out_specs=tok_spec,
    )
    return pl.pallas_call(
        functools.partial(_attn_sample_kernel, n_pages, dec_seq),
        grid_spec=grid_spec,
        out_shape=jax.ShapeDtypeStruct((n_seq * dec_seq, SB_WIDTH), F32),
        compiler_params=_params("arbitrary"),
        name="attn_sample",
    )(page_table, bias_rows, q, k_new, v_new, _tri_incl(PAGE), *([cache_k] * n_pages), *([cache_v] * n_pages))


def _expand_matrix(reps):
    r = lax.broadcasted_iota(jnp.int32, (LANES, SSM_HEADS * reps), 0)
    c = lax.broadcasted_iota(jnp.int32, (LANES, SSM_HEADS * reps), 1)
    return (c // reps == r).astype(BF16)


def _ssd_core(act, dt_raw, z, valid, tri, e1, e2, dtb, a_row, dsk, nw, state_in, c_masks):
    xs = act[:, :SSM_INNER]
    bm = act[:, SSM_INNER:SSM_INNER + SSM_GROUPS * SSM_N]
    cm = act[:, SSM_INNER + SSM_GROUPS * SSM_N:]
    dt = _softplus(dt_raw + dtb)
    dta = dt * a_row
    acs = _dot3_r(tri, dta)
    acs_t = acs.T
    acs_e2 = _dot3_l(acs, e2)
    dt_e = _dot3_l(dt, e1)
    eacs_e = _dot3_l(jnp.exp(acs), e1)
    n_slots = len(state_in)
    dend_e = []
    cdec_e = []
    for s in range(n_slots):
        if c_masks[s] is None:
            last = acs[SSM_CHUNK - 1:SSM_CHUNK, :]
        else:
            last = jnp.min(jnp.where(c_masks[s], acs, jnp.inf), axis=0, keepdims=True)
        de = jnp.exp(last - acs)
        if c_masks[s] is not None:
            de = jnp.where(c_masks[s], de, 0.0)
        dend_e.append(_dot3_l(de, e1))
        cdec_e.append(_dot3_l(jnp.broadcast_to(jnp.exp(last), (SUBLANES, LANES)), e1)[0:1, :])
    bmb = [bm[:, g * SSM_N:(g + 1) * SSM_N] for g in range(SSM_GROUPS)]
    cmb = [cm[:, g * SSM_N:(g + 1) * SSM_N].astype(BF16) for g in range(SSM_GROUPS)]
    cb = [_dot_nt(cmb[g], bmb[g].astype(BF16)) for g in range(SSM_GROUPS)]
    bt = [bmb[g].T.astype(BF16) for g in range(SSM_GROUPS)]
    lane = lax.broadcasted_iota(jnp.int32, (SSM_CHUNK, LANES), 1)
    y_pairs = []
    new_states = [[None] * SSM_PAIRS for _ in range(n_slots)]
    for k in range(SSM_PAIRS):
        g = (2 * k) // (SSM_HEADS // SSM_GROUPS)
        sl = slice(k * LANES, (k + 1) * LANES)
        xdt = xs[:, sl] * dt_e[:, sl]
        xdt_b = xdt.astype(BF16)
        parts = []
        for hh in range(2):
            h = 2 * k + hh
            seg = acs_e2[:, h * LANES:(h + 1) * LANES] - acs_t[h:h + 1, :]
            decay = jnp.where(valid, jnp.exp(jnp.where(valid, seg, 0.0)), 0.0)
            parts.append(_dot((cb[g] * decay).astype(BF16), xdt_b))
        y = jnp.where(lane < SSM_P, parts[0], parts[1])
        y_off = None
        for s in range(n_slots):
            st = state_in[s][k]
            c_s = cmb[g] if c_masks[s] is None else jnp.where(c_masks[s], cmb[g], jnp.zeros_like(cmb[g]))
            contrib = _dot(c_s, st.astype(BF16))
            y_off = contrib if y_off is None else y_off + contrib
            xdd = (xdt * dend_e[s][:, sl]).astype(BF16)
            new_states[s][k] = st * cdec_e[s][:, sl] + _dot(bt[g], xdd)
        y = y + y_off * eacs_e[:, sl] + dsk[:, sl] * xs[:, sl]
        y_pairs.append(y)
    y = jnp.concatenate(y_pairs, axis=1) * _silu(z)
    gsz = SSM_INNER // SSM_GROUPS
    outs = []
    for g in range(SSM_GROUPS):
        yg = y[:, g * gsz:(g + 1) * gsz]
        outs.append(yg * lax.rsqrt(jnp.mean(yg * yg, axis=-1, keepdims=True) + EPS) * nw[:, g * gsz:(g + 1) * gsz])
    return jnp.concatenate(outs, axis=1), new_states


def _ssd_prompt_kernel(xbc_ref, z_ref, dt_ref, cw_ref, cbias_ref, dtb_ref, a_ref, dsk_ref, nw_ref, tri_ref, e1_ref,
                       e2_ref, y_ref, tail_ref, st_ref, xs_scr, st_scr):
    c = pl.program_id(1)

    @pl.when(c == 0)
    def _():
        xs_scr[0:SUBLANES, :] = jnp.zeros((SUBLANES, CONV_DIM), F32)
        st_scr[...] = jnp.zeros_like(st_scr)

    xs_scr[SUBLANES:SUBLANES + SSM_CHUNK, :] = xbc_ref[...]
    conv = cbias_ref[...]
    for i in range(CONV_W):
        off = SUBLANES - (CONV_W - 1) + i
        conv = conv + xs_scr[off:off + SSM_CHUNK, :] * cw_ref[i:i + 1, :]
    tail = xs_scr[SSM_CHUNK:SSM_CHUNK + SUBLANES, :]
    xs_scr[0:SUBLANES, :] = tail
    tail_ref[0] = tail
    act = _silu(conv)
    r = lax.broadcasted_iota(jnp.int32, (SSM_CHUNK, SSM_CHUNK), 0)
    cc = lax.broadcasted_iota(jnp.int32, (SSM_CHUNK, SSM_CHUNK), 1)
    valid = r >= cc
    state_in = [[st_scr[k] for k in range(SSM_PAIRS)]]
    y, new_states = _ssd_core(act, dt_ref[...], z_ref[...], valid, tri_ref[...], e1_ref[...], e2_ref[...],
                              dtb_ref[...], a_ref[...], dsk_ref[...], nw_ref[...], state_in, [None])
    for k in range(SSM_PAIRS):
        st_scr[k] = new_states[0][k]
        st_ref[0, k] = new_states[0][k]
    y_ref[...] = y.astype(y_ref.dtype)


def _ssd_consts(conv_w, conv_b, dt_bias, a_log, d_skip, ssm_norm_w):
    pad = LANES - SSM_HEADS
    cw = jnp.concatenate([conv_w, jnp.zeros((SUBLANES - CONV_W, CONV_DIM), F32)], axis=0)
    dtb = jnp.pad(dt_bias, (0, pad)).reshape(1, LANES)
    a_row = jnp.pad(-jnp.exp(a_log), (0, pad)).reshape(1, LANES)
    dsk = jnp.repeat(d_skip, SSM_P).reshape(1, SSM_INNER)
    return cw, conv_b.reshape(1, CONV_DIM), dtb, a_row, dsk, ssm_norm_w.reshape(1, SSM_INNER)


def _const_spec(shape, ngrid):
    zeros = (0,) * len(shape)
    if ngrid == 1:
        return pl.BlockSpec(shape, lambda i: zeros)
    return pl.BlockSpec(shape, lambda i, j: zeros)


def _ssd_prompt(xbc, z, dt, consts, batch, seq):
    nc = seq // SSM_CHUNK
    cw, cbias, dtb, a_row, dsk, nw = consts
    tok = lambda w: pl.BlockSpec((SSM_CHUNK, w), lambda b, c: (b * nc + c, 0))
    tri = _tri_incl(SSM_CHUNK)
    return pl.pallas_call(
        _ssd_prompt_kernel,
        grid=(batch, nc),
        in_specs=[tok(CONV_DIM), tok(SSM_INNER), tok(LANES),
                  _const_spec(cw.shape, 2), _const_spec(cbias.shape, 2), _const_spec(dtb.shape, 2),
                  _const_spec(a_row.shape, 2), _const_spec(dsk.shape, 2), _const_spec(nw.shape, 2),
                  _const_spec((SSM_CHUNK, SSM_CHUNK), 2), _const_spec((LANES, SSM_INNER), 2),
                  _const_spec((LANES, SSM_HEADS * LANES), 2)],
        out_specs=[tok(SSM_INNER),
                   pl.BlockSpec((1, SUBLANES, CONV_DIM), lambda b, c: (b, 0, 0)),
                   pl.BlockSpec((1, SSM_PAIRS, SSM_N, LANES), lambda b, c: (b, 0, 0, 0))],
        out_shape=[jax.ShapeDtypeStruct((batch * seq, SSM_INNER), BF16),
                   jax.ShapeDtypeStruct((batch, SUBLANES, CONV_DIM), F32),
                   jax.ShapeDtypeStruct((batch, SSM_PAIRS, SSM_N, LANES), F32)],
        scratch_shapes=[pltpu.VMEM((SSM_CHUNK + SUBLANES, CONV_DIM), F32),
                        pltpu.VMEM((SSM_PAIRS, SSM_N, LANES), F32)],
        compiler_params=_params("arbitrary", "arbitrary"),
        name="ssd_prompt",
    )(xbc, z, dt, cw, cbias, dtb, a_row, dsk, nw, tri, _expand_matrix(SSM_P), _expand_matrix(LANES))


def _ssd_sample_kernel(dec_seq, xbc_ref, prev_ref, z_ref, dt_ref, st_in_ref, cw_ref, cbias_ref, dtb_ref, a_ref,
                       dsk_ref, nw_ref, tri_ref, e1_ref, e2_ref, y_ref, st_ref):
    n_slots = SSM_CHUNK // dec_seq
    cur = xbc_ref[...]
    prev = prev_ref[...]
    pos = lax.broadcasted_iota(jnp.int32, (SSM_CHUNK, CONV_DIM), 0) % dec_seq
    conv = cbias_ref[...] + cur * cw_ref[CONV_W - 1:CONV_W, :]
    for k in range(1, CONV_W):
        shifted = jnp.where(pos < k, pltpu.roll(prev, SSM_CHUNK - dec_seq + k, axis=0), pltpu.roll(cur, k, axis=0))
        conv = conv + shifted * cw_ref[CONV_W - 1 - k:CONV_W - k, :]
    act = _silu(conv)
    r = lax.broadcasted_iota(jnp.int32, (SSM_CHUNK, SSM_CHUNK), 0)
    cc = lax.broadcasted_iota(jnp.int32, (SSM_CHUNK, SSM_CHUNK), 1)
    valid = (r >= cc) & ((r // dec_seq) == (cc // dec_seq))
    row_seq = lax.broadcasted_iota(jnp.int32, (SSM_CHUNK, LANES), 0) // dec_seq
    c_masks = [row_seq == s for s in range(n_slots)]
    state_in = [[st_in_ref[s, k] for k in range(SSM_PAIRS)] for s in range(n_slots)]
    y, new_states = _ssd_core(act, dt_ref[...], z_ref[...], valid, tri_ref[...], e1_ref[...], e2_ref[...],
                              dtb_ref[...], a_ref[...], dsk_ref[...], nw_ref[...], state_in, c_masks)
    for s in range(n_slots):
        for k in range(SSM_PAIRS):
            st_ref[s, k] = new_states[s][k]
    y_ref[...] = y.astype(y_ref.dtype)


def _ssd_sample(xbc, prev, z, dt, state_t, consts, dec_seq):
    n_tok = xbc.shape[0]
    n_slots = SSM_CHUNK // dec_seq
    nblk = n_tok // SSM_CHUNK
    cw, cbias, dtb, a_row, dsk, nw = consts
    tok = lambda w: pl.BlockSpec((SSM_CHUNK, w), lambda i: (i, 0))
    st_spec = pl.BlockSpec((n_slots, SSM_PAIRS, SSM_N, LANES), lambda i: (i, 0, 0, 0))
    r = lax.broadcasted_iota(jnp.int32, (SSM_CHUNK, SSM_CHUNK), 0)
    c = lax.broadcasted_iota(jnp.int32, (SSM_CHUNK, SSM_CHUNK), 1)
    tri = ((r >= c) & ((r // dec_seq) == (c // dec_seq))).astype(BF16)
    return pl.pallas_call(
        functools.partial(_ssd_sample_kernel, dec_seq),
        grid=(nblk,),
        in_specs=[tok(CONV_DIM), tok(CONV_DIM), tok(SSM_INNER), tok(LANES), st_spec,
                  _const_spec(cw.shape, 1), _const_spec(cbias.shape, 1), _const_spec(dtb.shape, 1),
                  _const_spec(a_row.shape, 1), _const_spec(dsk.shape, 1), _const_spec(nw.shape, 1),
                  _const_spec((SSM_CHUNK, SSM_CHUNK), 1), _const_spec((LANES, SSM_INNER), 1),
                  _const_spec((LANES, SSM_HEADS * LANES), 1)],
        out_specs=[tok(SSM_INNER), st_spec],
        out_shape=[jax.ShapeDtypeStruct((n_tok, SSM_INNER), BF16),
                   jax.ShapeDtypeStruct(state_t.shape, F32)],
        compiler_params=_params("arbitrary"),
        name="ssd_sample",
    )(xbc, prev, z, dt, state_t, cw, cbias, dtb, a_row, dsk, nw, tri, _expand_matrix(SSM_P), _expand_matrix(LANES))


def _merge_kernel(o_ref, y_ref, ga_ref, gs_ref, x_ref, g1_ref, sh_ref, sc_ref, nw_ref, wa_ref, ws_ref, wo_ref,
                  x1_ref, h2_ref):
    merged = (jax.nn.sigmoid(ga_ref[...]) * _dot(o_ref[...].astype(BF16), wa_ref[...])
              + jax.nn.sigmoid(gs_ref[...]) * _dot(y_ref[...], ws_ref[...]))
    mix = _dot(merged.astype(BF16), wo_ref[...])
    x = x_ref[...]
    g, r, d = x.shape
    x1 = x + g1_ref[...] * mix.reshape(g, r, d)
    x1_ref[...] = x1
    y = x1 * lax.rsqrt(jnp.mean(x1 * x1, axis=-1, keepdims=True) + EPS) * nw_ref[...]
    h2 = y * (1.0 + sc_ref[...]) + sh_ref[...]
    h2_ref[...] = h2.reshape(g * r, d).astype(h2_ref.dtype)


def _mod_map(gpb, blocks_per_mod, col):
    if gpb == 1:
        return lambda i: (i // blocks_per_mod, 0, col)
    return lambda i: (i, 0, col)


def _merge(o_attn, y_ssm, ga, gs, x3, mod3, norm2_w, wa, ws, wo, groups_per_block, blocks_per_mod):
    ng, r, d = x3.shape
    gpb = groups_per_block
    nblk = ng // gpb
    tm = gpb * r
    t = ng * r
    tok = lambda w: pl.BlockSpec((tm, w), lambda i: (i, 0))
    xspec = pl.BlockSpec((gpb, r, d), lambda i: (i, 0, 0))
    mspec = lambda col: pl.BlockSpec((gpb, 1, d), _mod_map(gpb, blocks_per_mod, col))
    return pl.pallas_call(
        _merge_kernel,
        grid=(nblk,),
        in_specs=[tok(SB_WIDTH), tok(SSM_INNER), tok(d), tok(d), xspec, mspec(2), mspec(3), mspec(4),
                  pl.BlockSpec((1, 1, d), lambda i: (0, 0, 0)),
                  _const_spec(wa.shape, 1), _const_spec(ws.shape, 1), _const_spec(wo.shape, 1)],
        out_specs=[xspec, tok(d)],
        out_shape=[jax.ShapeDtypeStruct((ng, r, d), F32), jax.ShapeDtypeStruct((t, d), BF16)],
        compiler_params=_params("arbitrary"),
        name="merge",
    )(o_attn, y_ssm, ga, gs, x3, mod3, mod3, mod3, norm2_w.reshape(1, 1, d), wa, ws, wo)


def _batcher_pairs(n):
    pairs = []
    p = 1
    while p < n:
        k = p
        while k >= 1:
            for j in range(k % p, n - k, 2 * k):
                for i in range(min(k, n - j - k)):
                    if (i + j) // (2 * p) == (i + j + k) // (2 * p):
                        pairs.append((i + j, i + j + k))
            k //= 2
        p *= 2
    return pairs


_SORT16 = _batcher_pairs(PEER_TOPK)


def _sort_desc(vals):
    vals = list(vals)
    for i, j in _SORT16:
        hi = jnp.maximum(vals[i], vals[j])
        lo = jnp.minimum(vals[i], vals[j])
        vals[i], vals[j] = hi, lo
    return vals


def _merge_top(a, b, sort_result=True):
    n = PEER_TOPK
    t = [jnp.maximum(a[k], b[n - 1 - k]) for k in range(n)]
    if not sort_result:
        return t
    d = n // 2
    while d >= 1:
        for k in range(n):
            if (k // d) % 2 == 0:
                hi = jnp.maximum(t[k], t[k + d])
                lo = jnp.minimum(t[k], t[k + d])
                t[k], t[k + d] = hi, lo
        d //= 2
    return t


def _top16_desc(vals, need_sorted=True):
    groups = [_sort_desc(vals[i:i + PEER_TOPK]) for i in range(0, len(vals), PEER_TOPK)]
    while len(groups) > 1:
        nxt = []
        for i in range(0, len(groups) - 1, 2):
            last = len(groups) == 2
            nxt.append(_merge_top(groups[i], groups[i + 1], sort_result=need_sorted or not last))
        if len(groups) % 2:
            nxt.append(groups[-1])
        groups = nxt
    return groups[0]


ROUTE_TM = 256


def _route_kernel(h_ref, wq_ref, k1_ref, k2_ref, th_ref, p_ref, code_ref, r_ref, s1_scr, s2_scr):
    half = PEER_HEADS * PEER_HALF
    qt = _dot_nt(wq_ref[...], h_ref[...])
    s1 = _dot(k1_ref[...], qt[:half].astype(BF16))
    s2 = _dot(k2_ref[...], qt[half:].astype(BF16))
    for c in range(ROUTE_TM // LANES):
        s1_scr[c] = s1[:, c * LANES:(c + 1) * LANES]
        s2_scr[c] = s2[:, c * LANES:(c + 1) * LANES]
    staircase = [(a, b) for a in range(PEER_TOPK) for b in range(PEER_TOPK) if (a + 1) * (b + 1) <= PEER_TOPK]
    neg_inf = jnp.full((SUBLANES, LANES), -jnp.inf, F32)
    for c in range(ROUTE_TM // LANES):
        tl = pl.ds(c * LANES, LANES)
        row = lambda scr, i, c=c: scr[c, i * SUBLANES:(i + 1) * SUBLANES, :]
        top1 = _top16_desc([row(s1_scr, i) for i in range(PEER_KEYS)])
        top2 = _top16_desc([row(s2_scr, i) for i in range(PEER_KEYS)])
        cand = {(a, b): top1[a] + top2[b] for a, b in staircase}
        cvals = [cand[ab] for ab in staircase]
        cvals += [neg_inf] * (-len(cvals) % PEER_TOPK)
        tau = _top16_desc(cvals, need_sorted=False)
        thr = tau[0]
        for v in tau[1:]:
            thr = jnp.minimum(thr, v)
        cmax = cand[(0, 0)]
        zsum = jnp.zeros((SUBLANES, LANES), F32)
        never = jnp.full((SUBLANES, LANES), PEER_TOPK + 1.0, F32)
        need = [never] * PEER_TOPK
        for a, b in staircase:
            ok = cand[(a, b)] >= thr
            zsum = zsum + jnp.where(ok, jnp.exp(cand[(a, b)] - cmax), 0.0)
            need[a] = need[a] - jnp.where(ok, 1.0, 0.0)
        inv_z = 1.0 / zsum
        for i in range(PEER_KEYS):
            s = row(s1_scr, i)
            th = never
            for a in range(PEER_TOPK):
                th = jnp.where(s == top1[a], need[a], th)
            th_ref[i, :, tl] = th
            p_ref[i, :, tl] = jnp.exp(s - top1[0]) * inv_z
        for h in range(PEER_HEADS):
            s2h = s2_scr[c, pl.ds(h, PEER_KEYS, stride=SUBLANES), :]
            code = jnp.zeros((PEER_KEYS, LANES), F32)
            for b in range(PEER_TOPK):
                code = code + jnp.where(s2h >= top2[b][h:h + 1, :], 1.0, 0.0)
            tiles = (PEER_KEYS // BF16_ROWS, BF16_ROWS, LANES)
            code_ref[h, :, :, tl] = code.reshape(tiles).astype(BF16)
            r_ref[h, :, :, tl] = jnp.exp(s2h - top2[0][h:h + 1, :]).reshape(tiles).astype(BF16)


def _route(h2, wq_t, kbd1, kbd2):
    t = h2.shape[0]
    tm = ROUTE_TM
    half = PEER_HEADS * PEER_HALF
    kt = PEER_KEYS // BF16_ROWS
    return pl.pallas_call(
        _route_kernel,
        grid=(t // tm,),
        in_specs=[pl.BlockSpec((tm, D_MODEL), lambda i: (i, 0)),
                  _const_spec(wq_t.shape, 1), _const_spec(kbd1.shape, 1), _const_spec(kbd2.shape, 1)],
        out_specs=[pl.BlockSpec((PEER_KEYS, PEER_HEADS, tm), lambda i: (0, 0, i)),
                   pl.BlockSpec((PEER_KEYS, PEER_HEADS, tm), lambda i: (0, 0, i)),
                   pl.BlockSpec((PEER_HEADS, kt, BF16_ROWS, tm), lambda i: (0, 0, 0, i)),
                   pl.BlockSpec((PEER_HEADS, kt, BF16_ROWS, tm), lambda i: (0, 0, 0, i))],
        out_shape=[jax.ShapeDtypeStruct((PEER_KEYS, PEER_HEADS, t), F32),
                   jax.ShapeDtypeStruct((PEER_KEYS, PEER_HEADS, t), F32),
                   jax.ShapeDtypeStruct((PEER_HEADS, kt, BF16_ROWS, t), BF16),
                   jax.ShapeDtypeStruct((PEER_HEADS, kt, BF16_ROWS, t), BF16)],
        scratch_shapes=[pltpu.VMEM((tm // LANES, half, LANES), F32), pltpu.VMEM((tm // LANES, half, LANES), F32)],
        compiler_params=_params("arbitrary"),
        name="peer_route",
    )(h2, wq_t, kbd1, kbd2)


PEER_TM = 1024
PEER_KB = 8
PEER_SUB = 8
BF16_ROWS = 16


def _gelu_tanh(x):
    return 0.5 * x * (1.0 + jnp.tanh(math.sqrt(2.0 / math.pi) * (x + 0.044715 * (x * x * x))))


def _peer_kernel(h_ref, u_ref, vt_ref, th_ref, p_ref, code_ref, r_ref, x_ref, g2_ref, nw_ref, y_ref, acc_ref):
    j = pl.program_id(1)
    tm = h_ref.shape[0]
    kt = PEER_KEYS // BF16_ROWS

    @pl.when(j == 0)
    def _():
        acc_ref[...] = jnp.zeros_like(acc_ref)

    zero = jnp.zeros((), BF16)
    h = h_ref[...]
    total = None
    for sb in range(PEER_KB // PEER_SUB):
        rows = slice(sb * PEER_SUB * PEER_KEYS, (sb + 1) * PEER_SUB * PEER_KEYS)
        act = _gelu_tanh(_dot_nt(u_ref[rows, :], h))
        act = act.reshape(PEER_SUB * kt, BF16_ROWS, tm).astype(BF16)
        parts = []
        for s in range(PEER_SUB):
            ii = sb * PEER_SUB + s
            w = None
            for hd in range(PEER_HEADS):
                need = jnp.broadcast_to(th_ref[ii, hd:hd + 1, :], (BF16_ROWS, tm)).astype(BF16)
                pr = jnp.broadcast_to(p_ref[ii, hd:hd + 1, :], (BF16_ROWS, tm)).astype(BF16)
                sel = jnp.where(code_ref[hd] >= need[None], r_ref[hd], zero) * pr[None]
                w = sel if w is None else w + sel
            parts.append(w * act[s * kt:(s + 1) * kt])
        a = jnp.concatenate(parts, axis=0).reshape(PEER_SUB * PEER_KEYS, tm)
        contrib = _dot(vt_ref[:, rows], a)
        total = contrib if total is None else total + contrib
    acc_ref[...] += total

    @pl.when(j == pl.num_programs(1) - 1)
    def _():
        x1 = x_ref[...]
        g, rr, d = x1.shape
        x2 = x1 + g2_ref[...] * acc_ref[...].T.reshape(g, rr, d)
        y_ref[...] = x2 * lax.rsqrt(jnp.mean(x2 * x2, axis=-1, keepdims=True) + EPS) * nw_ref[...]


def _peer_dense(h2, u_b, vt_b, th, p, code, r, x1_3, mod3, norm_f_w, mod_per_group):
    t = h2.shape[0]
    tm = min(PEER_TM, t)
    ng, rr, d = x1_3.shape
    gt = tm // rr
    if mod_per_group:
        g2_spec = pl.BlockSpec((gt, 1, d), lambda i, j: (i, 0, 5))
    else:
        tiles_per_mod = (ng // mod3.shape[0]) // gt
        g2_spec = pl.BlockSpec((1, 1, d), lambda i, j: (i // tiles_per_mod, 0, 5))
    x_spec = pl.BlockSpec((gt, rr, d), lambda i, j: (i, 0, 0))
    eb = PEER_KB * PEER_KEYS
    nj = PEER_KEYS // PEER_KB
    kt = PEER_KEYS // BF16_ROWS
    mask_spec = pl.BlockSpec((PEER_HEADS, kt, BF16_ROWS, tm), lambda i, j: (0, 0, 0, i))
    return pl.pallas_call(
        _peer_kernel,
        grid=(t // tm, nj),
        in_specs=[pl.BlockSpec((tm, D_MODEL), lambda i, j: (i, 0)),
                  pl.BlockSpec((eb, D_MODEL), lambda i, j: (j, 0)),
                  pl.BlockSpec((D_MODEL, eb), lambda i, j: (0, j)),
                  pl.BlockSpec((PEER_KB, PEER_HEADS, tm), lambda i, j: (j, 0, i)),
                  pl.BlockSpec((PEER_KB, PEER_HEADS, tm), lambda i, j: (j, 0, i)),
                  mask_spec, mask_spec, x_spec, g2_spec, pl.BlockSpec((1, 1, d), lambda i, j: (0, 0, 0))],
        out_specs=x_spec,
        out_shape=jax.ShapeDtypeStruct((ng, rr, d), F32),
        scratch_shapes=[pltpu.VMEM((D_MODEL, tm), F32)],
        compiler_params=_params("arbitrary", "arbitrary"),
        name="peer_dense",
    )(h2, u_b, vt_b, th, p, code, r, x1_3, mod3, norm_f_w.reshape(1, 1, d))


def _pack_w_in(w_in):
    q_end = 3 * SB_WIDTH + SSM_INNER + CONV_DIM
    dt_w = jnp.pad(w_in[:, q_end:q_end + SSM_HEADS], ((0, 0), (0, LANES - SSM_HEADS)))
    return jnp.concatenate([w_in[:, :q_end], dt_w, w_in[:, q_end + SSM_HEADS:]], axis=1).astype(BF16)


def _pack_peer(peer_w_query, peer_keys1, peer_keys2):
    d = peer_w_query.shape[0]
    wq = peer_w_query.reshape(d, PEER_HEADS, 2, PEER_HALF)
    wq_t = jnp.transpose(wq, (2, 1, 3, 0)).reshape(2 * PEER_HEADS * PEER_HALF, d).astype(BF16)
    eye = jnp.eye(PEER_HEADS, dtype=F32)

    def block_diag(keys):
        return jnp.einsum("kd,hg->khgd", keys, eye).reshape(PEER_KEYS * PEER_HEADS, PEER_HEADS * PEER_HALF).astype(BF16)

    return wq_t, block_diag(peer_keys1), block_diag(peer_keys2)


def _layer_tail(o_attn, y_ssm, proj, x3, mod3, lw, gpb, bpm):
    x1_3, h2 = _merge(o_attn, y_ssm, proj["ga"], proj["gs"], x3, mod3, lw["norm2_w"], lw["wa"], lw["ws"], lw["wo"],
                      gpb, bpm)
    th, p, code, r = _route(h2, lw["wq_t"], lw["kbd1"], lw["kbd2"])
    return _peer_dense(h2, lw["u_b"], lw["vt_b"], th, p, code, r, x1_3, mod3, lw["norm_f_w"], mod_per_group=gpb > 1)


def _proj_dict(outs):
    names = ("q", "k", "v", "kb", "vb", "z", "xbc", "dt", "ga", "gs")
    return dict(zip(names, outs))


INPROJ_TM = 256


def kernel(x_prompt, x_sample, cache_k, cache_v, state_conv, state_ssm, page_table, c_prompt, c_sample, norm1_w,
           norm2_w, w_ada, b_ada, w_in, sb_bias, conv_w, conv_b, dt_bias, a_log, d_skip, ssm_norm_w, w_br_attn,
           w_br_ssm, w_out, peer_w_query, peer_keys1, peer_keys2, peer_u, peer_v, norm_f_w):
    batch, seq, d = x_prompt.shape
    n_seq, dec_seq, _ = x_sample.shape
    assert w_ada.shape[0] == 1, "single layer"
    lw = {"norm2_w": norm2_w[0], "wa": w_br_attn[0].astype(BF16), "ws": w_br_ssm[0].astype(BF16),
          "wo": w_out[0].astype(BF16), "norm_f_w": norm_f_w,
          "u_b": peer_u[0].astype(BF16), "vt_b": peer_v[0].T.astype(BF16)}
    lw["wq_t"], lw["kbd1"], lw["kbd2"] = _pack_peer(peer_w_query[0], peer_keys1[0], peer_keys2[0])
    w_packed = _pack_w_in(w_in[0])
    consts = _ssd_consts(conv_w[0], conv_b[0], dt_bias[0], a_log[0], d_skip[0], ssm_norm_w[0])

    mod = _adaln(jnp.concatenate([c_prompt, c_sample], axis=0), w_ada[0], b_ada[0])
    mod_p = mod[:batch].reshape(batch, 1, 6 * d)
    mod_s = mod[batch:].reshape(n_seq, 1, 6 * d)

    tm = INPROJ_TM
    bpm = seq // tm
    xp3 = x_prompt.reshape(batch * bpm, tm, d)
    pp = _proj_dict(_inproj(xp3, mod_p, norm1_w[0], w_packed, 1, bpm, kv_transposed=True))
    o_attn_p = _attn_prompt(pp["q"], pp["kb"], pp["vb"], sb_bias[0], batch, seq)
    y_ssm_p, tail_p, st_p = _ssd_prompt(pp["xbc"], pp["z"], pp["dt"], consts, batch, seq)
    y_prompt = _layer_tail(o_attn_p, y_ssm_p, pp, xp3, mod_p, lw, 1, bpm).reshape(batch, seq, d)

    gpb = tm // dec_seq
    ps = _proj_dict(_inproj(x_sample, mod_s, norm1_w[0], w_packed, gpb, 1))
    o_attn_s = _attn_sample(ps["q"], ps["k"], ps["v"], cache_k[0].transpose(0, 2, 3, 1),
                            cache_v[0].transpose(0, 2, 3, 1), page_table, sb_bias[0], dec_seq)
    prev = jnp.pad(state_conv[0], ((0, 0), (dec_seq - (CONV_W - 1), 0), (0, 0))).reshape(n_seq * dec_seq, CONV_DIM)
    st_in = state_ssm[0].reshape(n_seq, SSM_PAIRS, 2 * SSM_P, SSM_N).transpose(0, 1, 3, 2)
    y_ssm_s, st_s = _ssd_sample(ps["xbc"], prev, ps["z"], ps["dt"], st_in, consts, dec_seq)
    y_sample = _layer_tail(o_attn_s, y_ssm_s, ps, x_sample, mod_s, lw, gpb, 1)

    def state_out(st):
        b = st.shape[0]
        return st.transpose(0, 1, 3, 2).reshape(1, b, SSM_HEADS, SSM_P, SSM_N)

    k_prompt = pp["k"].reshape(1, batch, SB_HEADS, SB_HEAD_DIM, seq).transpose(0, 1, 4, 2, 3)
    v_prompt = pp["v"].reshape(1, batch, SB_HEADS, SB_HEAD_DIM, seq).transpose(0, 1, 4, 2, 3)
    conv_prompt = tail_p[:, SUBLANES - (CONV_W - 1):, :][None]
    k_sample = ps["k"].reshape(1, n_seq, dec_seq, SB_HEADS, SB_HEAD_DIM)
    v_sample = ps["v"].reshape(1, n_seq, dec_seq, SB_HEADS, SB_HEAD_DIM)
    conv_sample = ps["xbc"].reshape(n_seq, dec_seq, CONV_DIM)[:, dec_seq - (CONV_W - 1):, :][None]
    return (y_prompt, y_sample, k_prompt, v_prompt, conv_prompt, state_out(st_p), k_sample, v_sample, conv_sample,
            state_out(st_s))
```

```python
import functools
import math

import jax
import jax.numpy as jnp
from jax import lax
from jax.experimental import pallas as pl
from jax.experimental.pallas import tpu as pltpu

F32 = jnp.float32
BF16 = jnp.bfloat16

D_MODEL = 1024
EPS = 1e-6
SB_HEADS = 8
SB_HEAD_DIM = 64
SB_WIDTH = SB_HEADS * SB_HEAD_DIM
PAGE = 128
SSM_HEADS = 16
SSM_P = 64
SSM_GROUPS = 2
SSM_N = 128
SSM_INNER = SSM_HEADS * SSM_P
SSM_PAIRS = SSM_HEADS // 2
CONV_W = 4
CONV_DIM = SSM_INNER + 2 * SSM_GROUPS * SSM_N
SSM_CHUNK = 128
PEER_HEADS = 8
PEER_KEYS = 128
PEER_HALF = 128
PEER_TOPK = 16

LANES = 128
SUBLANES = 8
VMEM_LIMIT = 56 * 1024 * 1024

_PROJ_GROUPS = (("q", SB_WIDTH), ("k", SB_WIDTH), ("v", SB_WIDTH), ("z", SSM_INNER), ("xbc", CONV_DIM),
                ("dt", LANES), ("ga", D_MODEL), ("gs", D_MODEL))
_PROJ_COLS = sum(w for _, w in _PROJ_GROUPS)


def _params(*sem):
    return pltpu.CompilerParams(dimension_semantics=sem, vmem_limit_bytes=VMEM_LIMIT)


def _dot(a, b):
    return jnp.dot(a, b, preferred_element_type=F32)


def _dot_nt(a, b):
    return lax.dot_general(a, b, (((1,), (1,)), ((), ())), preferred_element_type=F32)


def _split3(x):
    hi = x.astype(BF16)
    r = x - hi.astype(F32)
    mid = r.astype(BF16)
    lo = (r - mid.astype(F32)).astype(BF16)
    return hi, mid, lo


def _dot3_l(x, m):
    hi, mid, lo = _split3(x)
    return _dot(hi, m) + _dot(mid, m) + _dot(lo, m)


def _dot3_r(m, x):
    hi, mid, lo = _split3(x)
    return _dot(m, hi) + _dot(m, mid) + _dot(m, lo)


def _softplus(x):
    return jnp.maximum(x, 0.0) + jnp.log1p(jnp.exp(-jnp.abs(x)))


def _silu(x):
    return x * jax.nn.sigmoid(x)


def _adaln_kernel(c_ref, w_ref, b_ref, o_ref):
    s = _silu(c_ref[...]).astype(BF16)
    o_ref[...] = _dot(s, w_ref[...].astype(BF16)) + b_ref[...]


def _adaln(c_all, w_ada, b_ada):
    m, n = c_all.shape[0], w_ada.shape[1]
    tn = 1024
    return pl.pallas_call(
        _adaln_kernel,
        grid=(n // tn,),
        in_specs=[pl.BlockSpec((m, D_MODEL), lambda j: (0, 0)),
                  pl.BlockSpec((D_MODEL, tn), lambda j: (0, j)),
                  pl.BlockSpec((1, tn), lambda j: (0, j))],
        out_specs=pl.BlockSpec((m, tn), lambda j: (0, j)),
        out_shape=jax.ShapeDtypeStruct((m, n), F32),
        compiler_params=_params("arbitrary"),
        name="adaln",
    )(c_all, w_ada, b_ada.reshape(1, n))


def _inproj_kernel(kv_transposed, x_ref, sh_ref, sc_ref, nw_ref, w_ref, q_ref, k_ref, v_ref, kb_ref, vb_ref, z_ref,
                   xbc_ref, dt_ref, ga_ref, gs_ref):
    x = x_ref[...]
    g, r, d = x.shape
    y = x * lax.rsqrt(jnp.mean(x * x, axis=-1, keepdims=True) + EPS) * nw_ref[...]
    h = y * (1.0 + sc_ref[...]) + sh_ref[...]
    hb = h.reshape(g * r, d).astype(BF16)
    outs = {"q": q_ref, "k": k_ref, "v": v_ref, "z": z_ref, "xbc": xbc_ref, "dt": dt_ref, "ga": ga_ref, "gs": gs_ref}
    col = 0
    for name, width in _PROJ_GROUPS:
        res = _dot(hb, w_ref[:, col:col + width])
        col += width
        if name == "q":
            res = res * (SB_HEAD_DIM ** -0.5)
        if name in ("k", "v") and kv_transposed:
            outs[name][0] = res.T
        else:
            outs[name][...] = res.astype(outs[name].dtype)
        if name == "k":
            kb_ref[...] = res.astype(BF16)
        if name == "v":
            vb_ref[...] = res.astype(BF16)


def _inproj(x3, mod3, norm_w, w_packed, groups_per_block, blocks_per_mod, kv_transposed=False):
    ng, r, d = x3.shape
    gpb = groups_per_block
    nblk = ng // gpb
    tm = gpb * r
    t = ng * r
    if gpb == 1:
        mod_map = lambda col: (lambda i: (i // blocks_per_mod, 0, col))
    else:
        mod_map = lambda col: (lambda i: (i, 0, col))
    widths = dict(_PROJ_GROUPS)
    out_shapes = [
        jax.ShapeDtypeStruct((t, widths["q"]), F32), jax.ShapeDtypeStruct((t, widths["k"]), F32),
        jax.ShapeDtypeStruct((t, widths["v"]), F32), jax.ShapeDtypeStruct((t, widths["k"]), BF16),
        jax.ShapeDtypeStruct((t, widths["v"]), BF16), jax.ShapeDtypeStruct((t, widths["z"]), F32),
        jax.ShapeDtypeStruct((t, widths["xbc"]), F32), jax.ShapeDtypeStruct((t, widths["dt"]), F32),
        jax.ShapeDtypeStruct((t, widths["ga"]), F32), jax.ShapeDtypeStruct((t, widths["gs"]), F32)]
    out_specs = [pl.BlockSpec((tm, s.shape[1]), lambda i: (i, 0)) for s in out_shapes]
    if kv_transposed:
        assert gpb == 1
        for idx in (1, 2):
            out_shapes[idx] = jax.ShapeDtypeStruct((ng // blocks_per_mod, SB_WIDTH, blocks_per_mod * r), F32)
            out_specs[idx] = pl.BlockSpec((1, SB_WIDTH, tm), lambda i: (i // blocks_per_mod, 0, i % blocks_per_mod))
    return pl.pallas_call(
        functools.partial(_inproj_kernel, kv_transposed),
        grid=(nblk,),
        in_specs=[pl.BlockSpec((gpb, r, d), lambda i: (i, 0, 0)),
                  pl.BlockSpec((gpb, 1, d), mod_map(0)),
                  pl.BlockSpec((gpb, 1, d), mod_map(1)),
                  pl.BlockSpec((1, 1, d), lambda i: (0, 0, 0)),
                  pl.BlockSpec((d, _PROJ_COLS), lambda i: (0, 0))],
        out_specs=out_specs,
        out_shape=out_shapes,
        compiler_params=_params("arbitrary"),
        name="inproj",
    )(x3, mod3, mod3, norm_w.reshape(1, 1, d), w_packed)


ATT_TQ = 512
ATT_TK = 512
ATT_CUM = 256
ATT_HEADS = 2


def _stick_incl(z, mask, tri):
    sp = jnp.maximum(z, 0.0) + jnp.log(1.0 + jnp.exp(-jnp.abs(z)))
    if mask is not None:
        sp = jnp.where(mask, sp, 0.0)
    hi = sp.astype(BF16)
    lo = (sp - hi.astype(F32)).astype(BF16)
    n, cols = z.shape
    wd = tri.shape[0]
    stacked = jnp.concatenate([hi, lo], axis=0)
    chunks = []
    carry = None
    for c in range(cols // wd - 1, -1, -1):
        both = _dot(stacked[:, c * wd:(c + 1) * wd], tri)
        inc = both[:n] + both[n:]
        if carry is not None:
            inc = inc + carry
        carry = inc[:, 0:1]
        chunks.append(inc)
    return chunks[0] if len(chunks) == 1 else jnp.concatenate(chunks[::-1], axis=1)


def _stick_tile(z, mask, tri, cum):
    incl = _stick_incl(z, mask, tri)
    w = jnp.exp(z - incl - cum)
    if mask is not None:
        w = jnp.where(mask, w, 0.0)
    return w, cum + incl[:, 0:1]


def _attn_prompt_kernel(bias_ref, q_ref, k_ref, v_ref, tri_ref, o_ref):
    hg = pl.program_id(1)
    qi = pl.program_id(2)
    tq, tk = ATT_TQ, ATT_TK
    ratio = tq // tk
    width = ATT_HEADS * SB_HEAD_DIM
    q = q_ref[...].astype(BF16)
    lane_head = lax.broadcasted_iota(jnp.int32, (tq, width), 1) // SB_HEAD_DIM
    rel = lax.broadcasted_iota(jnp.int32, (tq, tk), 0) - lax.broadcasted_iota(jnp.int32, (tq, tk), 1)
    tri = tri_ref[...]
    qhs = [jnp.where(lane_head == hh, q, jnp.zeros_like(q)) for hh in range(ATT_HEADS)]
    biases = [bias_ref[hg * ATT_HEADS + hh] for hh in range(ATT_HEADS)]

    def block(kb, carry, mask):
        off = pl.multiple_of(kb * tk, tk)
        kblk = k_ref[pl.ds(off, tk), :]
        vblk = v_ref[pl.ds(off, tk), :]
        out = []
        for hh in range(ATT_HEADS):
            acc, cum = carry[hh]
            z = _dot_nt(qhs[hh], kblk) + biases[hh]
            w, cum = _stick_tile(z, mask, tri, cum)
            out.append((acc + _dot(w.astype(BF16), vblk), cum))
        return tuple(out)

    carry = tuple((jnp.zeros((tq, width), F32), jnp.zeros((tq, 1), F32)) for _ in range(ATT_HEADS))
    for d in range(ratio - 1, -1, -1):
        carry = block(qi * ratio + d, carry, rel > d * tk)
    res = lax.fori_loop(0, qi * ratio, lambda step, c: block(qi * ratio - 1 - step, c, None), carry)
    o = res[0][0]
    for hh in range(1, ATT_HEADS):
        o = jnp.where(lane_head == hh, res[hh][0], o)
    o_ref[...] = o.astype(o_ref.dtype)


def _tri_incl(n):
    r = lax.broadcasted_iota(jnp.int32, (n, n), 0)
    c = lax.broadcasted_iota(jnp.int32, (n, n), 1)
    return (r >= c).astype(BF16)


def _attn_prompt(q, kb, vb, sb_bias, batch, seq):
    tq = ATT_TQ
    nq = seq // tq
    width = ATT_HEADS * SB_HEAD_DIM
    return pl.pallas_call(
        _attn_prompt_kernel,
        grid=(batch, SB_HEADS // ATT_HEADS, nq),
        in_specs=[pl.BlockSpec(memory_space=pltpu.SMEM),
                  pl.BlockSpec((tq, width), lambda b, h, i: (b * nq + i, h)),
                  pl.BlockSpec((seq, width), lambda b, h, i: (b, h)),
                  pl.BlockSpec((seq, width), lambda b, h, i: (b, h)),
                  pl.BlockSpec((ATT_CUM, ATT_CUM), lambda b, h, i: (0, 0))],
        out_specs=pl.BlockSpec((tq, width), lambda b, h, i: (b * nq + i, h)),
        out_shape=jax.ShapeDtypeStruct((batch * seq, SB_WIDTH), BF16),
        compiler_params=_params("arbitrary", "arbitrary", "arbitrary"),
        name="attn_prompt",
    )(sb_bias, q, kb, vb, _tri_incl(ATT_CUM))


def _attn_sample_kernel(n_pages, dec_seq, pt_ref, bias_ref, q_ref, kn_ref, vn_ref, tri_ref, *refs):
    k_pages = refs[:n_pages]
    v_pages = refs[n_pages:2 * n_pages]
    o_ref = refs[2 * n_pages]
    rows = SB_HEADS * dec_seq
    q = q_ref[...]
    qt = jnp.concatenate([q] * SB_HEADS, axis=0)
    row_head = lax.broadcasted_iota(jnp.int32, (rows, SB_WIDTH), 0) // dec_seq
    lane_head = lax.broadcasted_iota(jnp.int32, (rows, SB_WIDTH), 1) // SB_HEAD_DIM
    qbd = jnp.where(row_head == lane_head, qt, 0.0).astype(BF16)
    bias = bias_ref[...]
    tri = tri_ref[...]
    r_i = lax.broadcasted_iota(jnp.int32, (rows, PAGE), 0) % dec_seq
    c_i = lax.broadcasted_iota(jnp.int32, (rows, PAGE), 1)
    new_valid = c_i < r_i

    def pad_rows(a):
        return jnp.concatenate([a, jnp.zeros((PAGE - dec_seq, SB_WIDTH), a.dtype)], axis=0)

    def page_tile(ref):
        return ref[0].reshape(SB_WIDTH, PAGE).astype(BF16)

    zs = [_dot_nt(qbd, pad_rows(kn_ref[...]).astype(BF16)) + bias]
    zs += [_dot(qbd, page_tile(k_pages[p])) + bias for p in range(n_pages - 1, -1, -1)]
    masks = [new_valid] + [None] * n_pages
    incls = [_stick_incl(z, m, tri) for z, m in zip(zs, masks)]
    cum = jnp.zeros((rows, 1), F32)
    acc = None
    for b, (z, m, incl) in enumerate(zip(zs, masks, incls)):
        w = jnp.exp(z - incl - cum)
        cum = cum + incl[:, 0:1]
        if m is not None:
            w = jnp.where(m, w, 0.0)
        wb = w.astype(BF16)
        if b == 0:
            contrib = _dot(wb, pad_rows(vn_ref[...]).astype(BF16))
        else:
            contrib = _dot_nt(wb, page_tile(v_pages[n_pages - b]))
        acc = contrib if acc is None else acc + contrib
    out = jnp.zeros((dec_seq, SB_WIDTH), F32)
    lane_head8 = lax.broadcasted_iota(jnp.int32, (dec_seq, SB_WIDTH), 1) // SB_HEAD_DIM
    for h in range(SB_HEADS):
        out = out + jnp.where(lane_head8 == h, acc[h * dec_seq:(h + 1) * dec_seq, :], 0.0)
    o_ref[...] = out.astype(o_ref.dtype)


def _attn_sample(q, k_new, v_new, cache_k, cache_v, page_table, sb_bias, dec_seq):
    n_seq, n_pages = page_table.shape
    rows = SB_HEADS * dec_seq
    bias_rows = jnp.broadcast_to(jnp.repeat(sb_bias, dec_seq)[:, None], (rows, PAGE))

    def page_spec(p):
        return pl.BlockSpec((1, SB_HEADS, SB_HEAD_DIM, PAGE), lambda s, pt: (pt[s, p], 0, 0, 0))

    tok_spec = pl.BlockSpec((dec_seq, SB_WIDTH), lambda s, pt: (s, 0))
    grid_spec = pltpu.PrefetchScalarGridSpec(
        num_scalar_prefetch=1,
        grid=(n_seq,),
        in_specs=[pl.BlockSpec((rows, PAGE), lambda s, pt: (0, 0)), tok_spec, tok_spec, tok_spec,
                  pl.BlockSpec((PAGE, PAGE), lambda s, pt: (0, 0))]
                 + [page_spec(p) for p in range(n_pages)] + [page_spec(p) for p in range(n_pages)],
        out_specs=tok_spec,
    )
    return pl.pallas_call(
        functools.partial(_attn_sample_kernel, n_pages, dec_seq),
        grid_spec=grid_spec,
        out_shape=jax.ShapeDtypeStruct((n_seq * dec_seq, SB_WIDTH), F32),
        compiler_params=_params("arbitrary"),
        name="attn_sample",
    )(page_table, bias_rows, q, k_new, v_new, _tri_incl(PAGE), *([cache_k] * n_pages), *([cache_v] * n_pages))


def _expand_matrix(reps):
    r = lax.broadcasted_iota(jnp.int32, (LANES, SSM_HEADS * reps), 0)
    c = lax.broadcasted_iota(jnp.int32, (LANES, SSM_HEADS * reps), 1)
    return (c // reps == r).astype(BF16)


def _ssd_core(act, dt_raw, z, valid, tri, e1, e2, dtb, a_row, dsk, nw, state_in, c_masks):
    xs = act[:, :SSM_INNER]
    bm = act[:, SSM_INNER:SSM_INNER + SSM_GROUPS * SSM_N]
    cm = act[:, SSM_INNER + SSM_GROUPS * SSM_N:]
    dt = _softplus(dt_raw + dtb)
    dta = dt * a_row
    acs = _dot3_r(tri, dta)
    acs_t = acs.T
    acs_e2 = _dot3_l(acs, e2)
    dt_e = _dot3_l(dt, e1)
    eacs_e = _dot3_l(jnp.exp(acs), e1)
    n_slots = len(state_in)
    dend_e = []
    cdec_e = []
    for s in range(n_slots):
        if c_masks[s] is None:
            last = acs[SSM_CHUNK - 1:SSM_CHUNK, :]
        else:
            last = jnp.min(jnp.where(c_masks[s], acs, jnp.inf), axis=0, keepdims=True)
        de = jnp.exp(last - acs)
        if c_masks[s] is not None:
            de = jnp.where(c_masks[s], de, 0.0)
        dend_e.append(_dot3_l(de, e1))
        cdec_e.append(_dot3_l(jnp.broadcast_to(jnp.exp(last), (SUBLANES, LANES)), e1)[0:1, :])
    bmb = [bm[:, g * SSM_N:(g + 1) * SSM_N] for g in range(SSM_GROUPS)]
    cmb = [cm[:, g * SSM_N:(g + 1) * SSM_N].astype(BF16) for g in range(SSM_GROUPS)]
    cb = [_dot_nt(cmb[g], bmb[g].astype(BF16)) for g in range(SSM_GROUPS)]
    bt = [bmb[g].T.astype(BF16) for g in range(SSM_GROUPS)]
    lane = lax.broadcasted_iota(jnp.int32, (SSM_CHUNK, LANES), 1)
    y_pairs = []
    new_states = [[None] * SSM_PAIRS for _ in range(n_slots)]
    for k in range(SSM_PAIRS):
        g = (2 * k) // (SSM_HEADS // SSM_GROUPS)
        sl = slice(k * LANES, (k + 1) * LANES)
        xdt = xs[:, sl] * dt_e[:, sl]
        xdt_b = xdt.astype(BF16)
        parts = []
        for hh in range(2):
            h = 2 * k + hh
            seg = acs_e2[:, h * LANES:(h + 1) * LANES] - acs_t[h:h + 1, :]
            decay = jnp.where(valid, jnp.exp(jnp.where(valid, seg, 0.0)), 0.0)
            parts.append(_dot((cb[g] * decay).astype(BF16), xdt_b))
        y = jnp.where(lane < SSM_P, parts[0], parts[1])
        y_off = None
        for s in range(n_slots):
            st = state_in[s][k]
            c_s = cmb[g] if c_masks[s] is None else jnp.where(c_masks[s], cmb[g], jnp.zeros_like(cmb[g]))
            contrib = _dot(c_s, st.astype(BF16))
            y_off = contrib if y_off is None else y_off + contrib
            xdd = (xdt * dend_e[s][:, sl]).astype(BF16)
            new_states[s][k] = st * cdec_e[s][:, sl] + _dot(bt[g], xdd)
        y = y + y_off * eacs_e[:, sl] + dsk[:, sl] * xs[:, sl]
        y_pairs.append(y)
    y = jnp.concatenate(y_pairs, axis=1) * _silu(z)
    gsz = SSM_INNER // SSM_GROUPS
    outs = []
    for g in range(SSM_GROUPS):
        yg = y[:, g * gsz:(g + 1) * gsz]
        outs.append(yg * lax.rsqrt(jnp.mean(yg * yg, axis=-1, keepdims=True) + EPS) * nw[:, g * gsz:(g + 1) * gsz])
    return jnp.concatenate(outs, axis=1), new_states


def _ssd_prompt_kernel(xbc_ref, z_ref, dt_ref, cw_ref, cbias_ref, dtb_ref, a_ref, dsk_ref, nw_ref, tri_ref, e1_ref,
                       e2_ref, y_ref, tail_ref, st_ref, xs_scr, st_scr):
    c = pl.program_id(1)

    @pl.when(c == 0)
    def _():
        xs_scr[0:SUBLANES, :] = jnp.zeros((SUBLANES, CONV_DIM), F32)
        st_scr[...] = jnp.zeros_like(st_scr)

    xs_scr[SUBLANES:SUBLANES + SSM_CHUNK, :] = xbc_ref[...]
    conv = cbias_ref[...]
    for i in range(CONV_W):
        off = SUBLANES - (CONV_W - 1) + i
        conv = conv + xs_scr[off:off + SSM_CHUNK, :] * cw_ref[i:i + 1, :]
    tail = xs_scr[SSM_CHUNK:SSM_CHUNK + SUBLANES, :]
    xs_scr[0:SUBLANES, :] = tail
    tail_ref[0] = tail
    act = _silu(conv)
    r = lax.broadcasted_iota(jnp.int32, (SSM_CHUNK, SSM_CHUNK), 0)
    cc = lax.broadcasted_iota(jnp.int32, (SSM_CHUNK, SSM_CHUNK), 1)
    valid = r >= cc
    state_in = [[st_scr[k] for k in range(SSM_PAIRS)]]
    y, new_states = _ssd_core(act, dt_ref[...], z_ref[...], valid, tri_ref[...], e1_ref[...], e2_ref[...],
                              dtb_ref[...], a_ref[...], dsk_ref[...], nw_ref[...], state_in, [None])
    for k in range(SSM_PAIRS):
        st_scr[k] = new_states[0][k]
        st_ref[0, k] = new_states[0][k]
    y_ref[...] = y.astype(y_ref.dtype)


def _ssd_consts(conv_w, conv_b, dt_bias, a_log, d_skip, ssm_norm_w):
    pad = LANES - SSM_HEADS
    cw = jnp.concatenate([conv_w, jnp.zeros((SUBLANES - CONV_W, CONV_DIM), F32)], axis=0)
    dtb = jnp.pad(dt_bias, (0, pad)).reshape(1, LANES)
    a_row = jnp.pad(-jnp.exp(a_log), (0, pad)).reshape(1, LANES)
    dsk = jnp.repeat(d_skip, SSM_P).reshape(1, SSM_INNER)
    return cw, conv_b.reshape(1, CONV_DIM), dtb, a_row, dsk, ssm_norm_w.reshape(1, SSM_INNER)


def _const_spec(shape, ngrid):
    zeros = (0,) * len(shape)
    if ngrid == 1:
        return pl.BlockSpec(shape, lambda i: zeros)
    return pl.BlockSpec(shape, lambda i, j: zeros)


def _ssd_prompt(xbc, z, dt, consts, batch, seq):
    nc = seq // SSM_CHUNK
    cw, cbias, dtb, a_row, dsk, nw = consts
    tok = lambda w: pl.BlockSpec((SSM_CHUNK, w), lambda b, c: (b * nc + c, 0))
    tri = _tri_incl(SSM_CHUNK)
    return pl.pallas_call(
        _ssd_prompt_kernel,
        grid=(batch, nc),
        in_specs=[tok(CONV_DIM), tok(SSM_INNER), tok(LANES),
                  _const_spec(cw.shape, 2), _const_spec(cbias.shape, 2), _const_spec(dtb.shape, 2),
                  _const_spec(a_row.shape, 2), _const_spec(dsk.shape, 2), _const_spec(nw.shape, 2),
                  _const_spec((SSM_CHUNK, SSM_CHUNK), 2), _const_spec((LANES, SSM_INNER), 2),
                  _const_spec((LANES, SSM_HEADS * LANES), 2)],
        out_specs=[tok(SSM_INNER),
                   pl.BlockSpec((1, SUBLANES, CONV_DIM), lambda b, c: (b, 0, 0)),
                   pl.BlockSpec((1, SSM_PAIRS, SSM_N, LANES), lambda b, c: (b, 0, 0, 0))],
        out_shape=[jax.ShapeDtypeStruct((batch * seq, SSM_INNER), BF16),
                   jax.ShapeDtypeStruct((batch, SUBLANES, CONV_DIM), F32),
                   jax.ShapeDtypeStruct((batch, SSM_PAIRS, SSM_N, LANES), F32)],
        scratch_shapes=[pltpu.VMEM((SSM_CHUNK + SUBLANES, CONV_DIM), F32),
                        pltpu.VMEM((SSM_PAIRS, SSM_N, LANES), F32)],
        compiler_params=_params("arbitrary", "arbitrary"),
        name="ssd_prompt",
    )(xbc, z, dt, cw, cbias, dtb, a_row, dsk, nw, tri, _expand_matrix(SSM_P), _expand_matrix(LANES))


def _ssd_sample_kernel(dec_seq, xbc_ref, prev_ref, z_ref, dt_ref, st_in_ref, cw_ref, cbias_ref, dtb_ref, a_ref,
                       dsk_ref, nw_ref, tri_ref, e1_ref, e2_ref, y_ref, st_ref):
    n_slots = SSM_CHUNK // dec_seq
    cur = xbc_ref[...]
    prev = prev_ref[...]
    pos = lax.broadcasted_iota(jnp.int32, (SSM_CHUNK, CONV_DIM), 0) % dec_seq
    conv = cbias_ref[...] + cur * cw_ref[CONV_W - 1:CONV_W, :]
    for k in range(1, CONV_W):
        shifted = jnp.where(pos < k, pltpu.roll(prev, SSM_CHUNK - dec_seq + k, axis=0), pltpu.roll(cur, k, axis=0))
        conv = conv + shifted * cw_ref[CONV_W - 1 - k:CONV_W - k, :]
    act = _silu(conv)
    r = lax.broadcasted_iota(jnp.int32, (SSM_CHUNK, SSM_CHUNK), 0)
    cc = lax.broadcasted_iota(jnp.int32, (SSM_CHUNK, SSM_CHUNK), 1)
    valid = (r >= cc) & ((r // dec_seq) == (cc // dec_seq))
    row_seq = lax.broadcasted_iota(jnp.int32, (SSM_CHUNK, LANES), 0) // dec_seq
    c_masks = [row_seq == s for s in range(n_slots)]
    state_in = [[st_in_ref[s, k] for k in range(SSM_PAIRS)] for s in range(n_slots)]
    y, new_states = _ssd_core(act, dt_ref[...], z_ref[...], valid, tri_ref[...], e1_ref[...], e2_ref[...],
                              dtb_ref[...], a_ref[...], dsk_ref[...], nw_ref[...], state_in, c_masks)
    for s in range(n_slots):
        for k in range(SSM_PAIRS):
            st_ref[s, k] = new_states[s][k]
    y_ref[...] = y.astype(y_ref.dtype)


def _ssd_sample(xbc, prev, z, dt, state_t, consts, dec_seq):
    n_tok = xbc.shape[0]
    n_slots = SSM_CHUNK // dec_seq
    nblk = n_tok // SSM_CHUNK
    cw, cbias, dtb, a_row, dsk, nw = consts
    tok = lambda w: pl.BlockSpec((SSM_CHUNK, w), lambda i: (i, 0))
    st_spec = pl.BlockSpec((n_slots, SSM_PAIRS, SSM_N, LANES), lambda i: (i, 0, 0, 0))
    r = lax.broadcasted_iota(jnp.int32, (SSM_CHUNK, SSM_CHUNK), 0)
    c = lax.broadcasted_iota(jnp.int32, (SSM_CHUNK, SSM_CHUNK), 1)
    tri = ((r >= c) & ((r // dec_seq) == (c // dec_seq))).astype(BF16)
    return pl.pallas_call(
        functools.partial(_ssd_sample_kernel, dec_seq),
        grid=(nblk,),
        in_specs=[tok(CONV_DIM), tok(CONV_DIM), tok(SSM_INNER), tok(LANES), st_spec,
                  _const_spec(cw.shape, 1), _const_spec(cbias.shape, 1), _const_spec(dtb.shape, 1),
                  _const_spec(a_row.shape, 1), _const_spec(dsk.shape, 1), _const_spec(nw.shape, 1),
                  _const_spec((SSM_CHUNK, SSM_CHUNK), 1), _const_spec((LANES, SSM_INNER), 1),
                  _const_spec((LANES, SSM_HEADS * LANES), 1)],
        out_specs=[tok(SSM_INNER), st_spec],
        out_shape=[jax.ShapeDtypeStruct((n_tok, SSM_INNER), BF16),
                   jax.ShapeDtypeStruct(state_t.shape, F32)],
        compiler_params=_params("arbitrary"),
        name="ssd_sample",
    )(xbc, prev, z, dt, state_t, cw, cbias, dtb, a_row, dsk, nw, tri, _expand_matrix(SSM_P), _expand_matrix(LANES))


def _merge_kernel(o_ref, y_ref, ga_ref, gs_ref, x_ref, g1_ref, sh_ref, sc_ref, nw_ref, wa_ref, ws_ref, wo_ref,
                  x1_ref, h2_ref):
    merged = (jax.nn.sigmoid(ga_ref[...]) * _dot(o_ref[...].astype(BF16), wa_ref[...])
              + jax.nn.sigmoid(gs_ref[...]) * _dot(y_ref[...], ws_ref[...]))
    mix = _dot(merged.astype(BF16), wo_ref[...])
    x = x_ref[...]
    g, r, d = x.shape
    x1 = x + g1_ref[...] * mix.reshape(g, r, d)
    x1_ref[...] = x1
    y = x1 * lax.rsqrt(jnp.mean(x1 * x1, axis=-1, keepdims=True) + EPS) * nw_ref[...]
    h2 = y * (1.0 + sc_ref[...]) + sh_ref[...]
    h2_ref[...] = h2.reshape(g * r, d).astype(h2_ref.dtype)


def _mod_map(gpb, blocks_per_mod, col):
    if gpb == 1:
        return lambda i: (i // blocks_per_mod, 0, col)
    return lambda i: (i, 0, col)


def _merge(o_attn, y_ssm, ga, gs, x3, mod3, norm2_w, wa, ws, wo, groups_per_block, blocks_per_mod):
    ng, r, d = x3.shape
    gpb = groups_per_block
    nblk = ng // gpb
    tm = gpb * r
    t = ng * r
    tok = lambda w: pl.BlockSpec((tm, w), lambda i: (i, 0))
    xspec = pl.BlockSpec((gpb, r, d), lambda i: (i, 0, 0))
    mspec = lambda col: pl.BlockSpec((gpb, 1, d), _mod_map(gpb, blocks_per_mod, col))
    return pl.pallas_call(
        _merge_kernel,
        grid=(nblk,),
        in_specs=[tok(SB_WIDTH), tok(SSM_INNER), tok(d), tok(d), xspec, mspec(2), mspec(3), mspec(4),
                  pl.BlockSpec((1, 1, d), lambda i: (0, 0, 0)),
                  _const_spec(wa.shape, 1), _const_spec(ws.shape, 1), _const_spec(wo.shape, 1)],
        out_specs=[xspec, tok(d)],
        out_shape=[jax.ShapeDtypeStruct((ng, r, d), F32), jax.ShapeDtypeStruct((t, d), BF16)],
        compiler_params=_params("arbitrary"),
        name="merge",
    )(o_attn, y_ssm, ga, gs, x3, mod3, mod3, mod3, norm2_w.reshape(1, 1, d), wa, ws, wo)


def _batcher_pairs(n):
    pairs = []
    p = 1
    while p < n:
        k = p
        while k >= 1:
            for j in range(k % p, n - k, 2 * k):
                for i in range(min(k, n - j - k)):
                    if (i + j) // (2 * p) == (i + j + k) // (2 * p):
                        pairs.append((i + j, i + j + k))
            k //= 2
        p *= 2
    return pairs


_SORT16 = _batcher_pairs(PEER_TOPK)


def _sort_desc(vals):
    vals = list(vals)
    for i, j in _SORT16:
        hi = jnp.maximum(vals[i], vals[j])
        lo = jnp.minimum(vals[i], vals[j])
        vals[i], vals[j] = hi, lo
    return vals


def _merge_top(a, b, sort_result=True):
    n = PEER_TOPK
    t = [jnp.maximum(a[k], b[n - 1 - k]) for k in range(n)]
    if not sort_result:
        return t
    d = n // 2
    while d >= 1:
        for k in range(n):
            if (k // d) % 2 == 0:
                hi = jnp.maximum(t[k], t[k + d])
                lo = jnp.minimum(t[k], t[k + d])
                t[k], t[k + d] = hi, lo
        d //= 2
    return t


def _top16_desc(vals, need_sorted=True):
    groups = [_sort_desc(vals[i:i + PEER_TOPK]) for i in range(0, len(vals), PEER_TOPK)]
    while len(groups) > 1:
        nxt = []
        for i in range(0, len(groups) - 1, 2):
            last = len(groups) == 2
            nxt.append(_merge_top(groups[i], groups[i + 1], sort_result=need_sorted or not last))
        if len(groups) % 2:
            nxt.append(groups[-1])
        groups = nxt
    return groups[0]


ROUTE_TM = 256


def _route_kernel(h_ref, wq_ref, k1_ref, k2_ref, th_ref, p_ref, code_ref, r_ref, s1_scr, s2_scr):
    half = PEER_HEADS * PEER_HALF
    qt = _dot_nt(wq_ref[...], h_ref[...])
    s1 = _dot(k1_ref[...], qt[:half].astype(BF16))
    s2 = _dot(k2_ref[...], qt[half:].astype(BF16))
    for c in range(ROUTE_TM // LANES):
        s1_scr[c] = s1[:, c * LANES:(c + 1) * LANES]
        s2_scr[c] = s2[:, c * LANES:(c + 1) * LANES]
    staircase = [(a, b) for a in range(PEER_TOPK) for b in range(PEER_TOPK) if (a + 1) * (b + 1) <= PEER_TOPK]
    neg_inf = jnp.full((SUBLANES, LANES), -jnp.inf, F32)
    for c in range(ROUTE_TM // LANES):
        tl = pl.ds(c * LANES, LANES)
        row = lambda scr, i, c=c: scr[c, i * SUBLANES:(i + 1) * SUBLANES, :]
        top1 = _top16_desc([row(s1_scr, i) for i in range(PEER_KEYS)])
        top2 = _top16_desc([row(s2_scr, i) for i in range(PEER_KEYS)])
        cand = {(a, b): top1[a] + top2[b] for a, b in staircase}
        cvals = [cand[ab] for ab in staircase]
        cvals += [neg_inf] * (-len(cvals) % PEER_TOPK)
        tau = _top16_desc(cvals, need_sorted=False)
        thr = tau[0]
        for v in tau[1:]:
            thr = jnp.minimum(thr, v)
        cmax = cand[(0, 0)]
        zsum = jnp.zeros((SUBLANES, LANES), F32)
        never = jnp.full((SUBLANES, LANES), PEER_TOPK + 1.0, F32)
        need = [never] * PEER_TOPK
        for a, b in staircase:
            ok = cand[(a, b)] >= thr
            zsum = zsum + jnp.where(ok, jnp.exp(cand[(a, b)] - cmax), 0.0)
            need[a] = need[a] - jnp.where(ok, 1.0, 0.0)
        inv_z = 1.0 / zsum
        for i in range(PEER_KEYS):
            s = row(s1_scr, i)
            th = never
            for a in range(PEER_TOPK):
                th = jnp.where(s == top1[a], need[a], th)
            th_ref[i, :, tl] = th
            p_ref[i, :, tl] = jnp.exp(s - top1[0]) * inv_z
        for h in range(PEER_HEADS):
            s2h = s2_scr[c, pl.ds(h, PEER_KEYS, stride=SUBLANES), :]
            code = jnp.zeros((PEER_KEYS, LANES), F32)
            for b in range(PEER_TOPK):
                code = code + jnp.where(s2h >= top2[b][h:h + 1, :], 1.0, 0.0)
            tiles = (PEER_KEYS // BF16_ROWS, BF16_ROWS, LANES)
            code_ref[h, :, :, tl] = code.reshape(tiles).astype(BF16)
            r_ref[h, :, :, tl] = jnp.exp(s2h - top2[0][h:h + 1, :]).reshape(tiles).astype(BF16)


def _route(h2, wq_t, kbd1, kbd2):
    t = h2.shape[0]
    tm = ROUTE_TM
    half = PEER_HEADS * PEER_HALF
    kt = PEER_KEYS // BF16_ROWS
    return pl.pallas_call(
        _route_kernel,
        grid=(t // tm,),
        in_specs=[pl.BlockSpec((tm, D_MODEL), lambda i: (i, 0)),
                  _const_spec(wq_t.shape, 1), _const_spec(kbd1.shape, 1), _const_spec(kbd2.shape, 1)],
        out_specs=[pl.BlockSpec((PEER_KEYS, PEER_HEADS, tm), lambda i: (0, 0, i)),
                   pl.BlockSpec((PEER_KEYS, PEER_HEADS, tm), lambda i: (0, 0, i)),
                   pl.BlockSpec((PEER_HEADS, kt, BF16_ROWS, tm), lambda i: (0, 0, 0, i)),
                   pl.BlockSpec((PEER_HEADS, kt, BF16_ROWS, tm), lambda i: (0, 0, 0, i))],
        out_shape=[jax.ShapeDtypeStruct((PEER_KEYS, PEER_HEADS, t), F32),
                   jax.ShapeDtypeStruct((PEER_KEYS, PEER_HEADS, t), F32),
                   jax.ShapeDtypeStruct((PEER_HEADS, kt, BF16_ROWS, t), BF16),
                   jax.ShapeDtypeStruct((PEER_HEADS, kt, BF16_ROWS, t), BF16)],
        scratch_shapes=[pltpu.VMEM((tm // LANES, half, LANES), F32), pltpu.VMEM((tm // LANES, half, LANES), F32)],
        compiler_params=_params("arbitrary"),
        name="peer_route",
    )(h2, wq_t, kbd1, kbd2)


PEER_TM = 1024
PEER_KB = 8
PEER_SUB = 8
BF16_ROWS = 16


def _gelu_tanh(x):
    return 0.5 * x * (1.0 + jnp.tanh(math.sqrt(2.0 / math.pi) * (x + 0.044715 * (x * x * x))))


def _peer_kernel(h_ref, u_ref, vt_ref, th_ref, p_ref, code_ref, r_ref, x_ref, g2_ref, nw_ref, y_ref, acc_ref):
    j = pl.program_id(1)
    tm = h_ref.shape[0]
    kt = PEER_KEYS // BF16_ROWS

    @pl.when(j == 0)
    def _():
        acc_ref[...] = jnp.zeros_like(acc_ref)

    zero = jnp.zeros((), BF16)
    h = h_ref[...]
    total = None
    for sb in range(PEER_KB // PEER_SUB):
        rows = slice(sb * PEER_SUB * PEER_KEYS, (sb + 1) * PEER_SUB * PEER_KEYS)
        act = _gelu_tanh(_dot_nt(u_ref[rows, :], h))
        act = act.reshape(PEER_SUB * kt, BF16_ROWS, tm).astype(BF16)
        parts = []
        for s in range(PEER_SUB):
            ii = sb * PEER_SUB + s
            w = None
            for hd in range(PEER_HEADS):
                need = jnp.broadcast_to(th_ref[ii, hd:hd + 1, :], (BF16_ROWS, tm)).astype(BF16)
                pr = jnp.broadcast_to(p_ref[ii, hd:hd + 1, :], (BF16_ROWS, tm)).astype(BF16)
                sel = jnp.where(code_ref[hd] >= need[None], r_ref[hd], zero) * pr[None]
                w = sel if w is None else w + sel
            parts.append(w * act[s * kt:(s + 1) * kt])
        a = jnp.concatenate(parts, axis=0).reshape(PEER_SUB * PEER_KEYS, tm)
        contrib = _dot(vt_ref[:, rows], a)
        total = contrib if total is None else total + contrib
    acc_ref[...] += total

    @pl.when(j == pl.num_programs(1) - 1)
    def _():
        x1 = x_ref[...]
        g, rr, d = x1.shape
        x2 = x1 + g2_ref[...] * acc_ref[...].T.reshape(g, rr, d)
        y_ref[...] = x2 * lax.rsqrt(jnp.mean(x2 * x2, axis=-1, keepdims=True) + EPS) * nw_ref[...]


def _peer_dense(h2, u_b, vt_b, th, p, code, r, x1_3, mod3, norm_f_w, mod_per_group):
    t = h2.shape[0]
    tm = min(PEER_TM, t)
    ng, rr, d = x1_3.shape
    gt = tm // rr
    if mod_per_group:
        g2_spec = pl.BlockSpec((gt, 1, d), lambda i, j: (i, 0, 5))
    else:
        tiles_per_mod = (ng // mod3.shape[0]) // gt
        g2_spec = pl.BlockSpec((1, 1, d), lambda i, j: (i // tiles_per_mod, 0, 5))
    x_spec = pl.BlockSpec((gt, rr, d), lambda i, j: (i, 0, 0))
    eb = PEER_KB * PEER_KEYS
    nj = PEER_KEYS // PEER_KB
    kt = PEER_KEYS // BF16_ROWS
    mask_spec = pl.BlockSpec((PEER_HEADS, kt, BF16_ROWS, tm), lambda i, j: (0, 0, 0, i))
    return pl.pallas_call(
        _peer_kernel,
        grid=(t // tm, nj),
        in_specs=[pl.BlockSpec((tm, D_MODEL), lambda i, j: (i, 0)),
                  pl.BlockSpec((eb, D_MODEL), lambda i, j: (j, 0)),
                  pl.BlockSpec((D_MODEL, eb), lambda i, j: (0, j)),
                  pl.BlockSpec((PEER_KB, PEER_HEADS, tm), lambda i, j: (j, 0, i)),
                  pl.BlockSpec((PEER_KB, PEER_HEADS, tm), lambda i, j: (j, 0, i)),
                  mask_spec, mask_spec, x_spec, g2_spec, pl.BlockSpec((1, 1, d), lambda i, j: (0, 0, 0))],
        out_specs=x_spec,
        out_shape=jax.ShapeDtypeStruct((ng, rr, d), F32),
        scratch_shapes=[pltpu.VMEM((D_MODEL, tm), F32)],
        compiler_params=_params("arbitrary", "arbitrary"),
        name="peer_dense",
    )(h2, u_b, vt_b, th, p, code, r, x1_3, mod3, norm_f_w.reshape(1, 1, d))


def _pack_w_in(w_in):
    q_end = 3 * SB_WIDTH + SSM_INNER + CONV_DIM
    dt_w = jnp.pad(w_in[:, q_end:q_end + SSM_HEADS], ((0, 0), (0, LANES - SSM_HEADS)))
    return jnp.concatenate([w_in[:, :q_end], dt_w, w_in[:, q_end + SSM_HEADS:]], axis=1).astype(BF16)


def _pack_peer(peer_w_query, peer_keys1, peer_keys2):
    d = peer_w_query.shape[0]
    wq = peer_w_query.reshape(d, PEER_HEADS, 2, PEER_HALF)
    wq_t = jnp.transpose(wq, (2, 1, 3, 0)).reshape(2 * PEER_HEADS * PEER_HALF, d).astype(BF16)
    eye = jnp.eye(PEER_HEADS, dtype=F32)

    def block_diag(keys):
        return jnp.einsum("kd,hg->khgd", keys, eye).reshape(PEER_KEYS * PEER_HEADS, PEER_HEADS * PEER_HALF).astype(BF16)

    return wq_t, block_diag(peer_keys1), block_diag(peer_keys2)


def _layer_tail(o_attn, y_ssm, proj, x3, mod3, lw, gpb, bpm):
    x1_3, h2 = _merge(o_attn, y_ssm, proj["ga"], proj["gs"], x3, mod3, lw["norm2_w"], lw["wa"], lw["ws"], lw["wo"],
                      gpb, bpm)
    th, p, code, r = _route(h2, lw["wq_t"], lw["kbd1"], lw["kbd2"])
    return _peer_dense(h2, lw["u_b"], lw["vt_b"], th, p, code, r, x1_3, mod3, lw["norm_f_w"], mod_per_group=gpb > 1)


def _proj_dict(outs):
    names = ("q", "k", "v", "kb", "vb", "z", "xbc", "dt", "ga", "gs")
    return dict(zip(names, outs))


INPROJ_TM = 256


def kernel(x_prompt, x_sample, cache_k, cache_v, state_conv, state_ssm, page_table, c_prompt, c_sample, norm1_w,
           norm2_w, w_ada, b_ada, w_in, sb_bias, conv_w, conv_b, dt_bias, a_log, d_skip, ssm_norm_w, w_br_attn,
           w_br_ssm, w_out, peer_w_query, peer_keys1, peer_keys2, peer_u, peer_v, norm_f_w):
    batch, seq, d = x_prompt.shape
    n_seq, dec_seq, _ = x_sample.shape
    assert w_ada.shape[0] == 1, "single layer"
    lw = {"norm2_w": norm2_w[0], "wa": w_br_attn[0].astype(BF16), "ws": w_br_ssm[0].astype(BF16),
          "wo": w_out[0].astype(BF16), "norm_f_w": norm_f_w,
          "u_b": peer_u[0].astype(BF16), "vt_b": peer_v[0].T.astype(BF16)}
    lw["wq_t"], lw["kbd1"], lw["kbd2"] = _pack_peer(peer_w_query[0], peer_keys1[0], peer_keys2[0])
    w_packed = _pack_w_in(w_in[0])
    consts = _ssd_consts(conv_w[0], conv_b[0], dt_bias[0], a_log[0], d_skip[0], ssm_norm_w[0])

    mod = _adaln(jnp.concatenate([c_prompt, c_sample], axis=0), w_ada[0], b_ada[0])
    mod_p = mod[:batch].reshape(batch, 1, 6 * d)
    mod_s = mod[batch:].reshape(n_seq, 1, 6 * d)

    tm = INPROJ_TM
    bpm = seq // tm
    xp3 = x_prompt.reshape(batch * bpm, tm, d)
    pp = _proj_dict(_inproj(xp3, mod_p, norm1_w[0], w_packed, 1, bpm, kv_transposed=True))
    o_attn_p = _attn_prompt(pp["q"], pp["kb"], pp["vb"], sb_bias[0], batch, seq)
    y_ssm_p, tail_p, st_p = _ssd_prompt(pp["xbc"], pp["z"], pp["dt"], consts, batch, seq)
    y_prompt = _layer_tail(o_attn_p, y_ssm_p, pp, xp3, mod_p, lw, 1, bpm).reshape(batch, seq, d)

    gpb = tm // dec_seq
    ps = _proj_dict(_inproj(x_sample, mod_s, norm1_w[0], w_packed, gpb, 1))
    o_attn_s = _attn_sample(ps["q"], ps["k"], ps["v"], cache_k[0].transpose(0, 2, 3, 1),
                            cache_v[0].transpose(0, 2, 3, 1), page_table, sb_bias[0], dec_seq)
    prev = jnp.pad(state_conv[0], ((0, 0), (dec_seq - (CONV_W - 1), 0), (0, 0))).reshape(n_seq * dec_seq, CONV_DIM)
    st_in = state_ssm[0].reshape(n_seq, SSM_PAIRS, 2 * SSM_P, SSM_N).transpose(0, 1, 3, 2)
    y_ssm_s, st_s = _ssd_sample(ps["xbc"], prev, ps["z"], ps["dt"], st_in, consts, dec_seq)
    y_sample = _layer_tail(o_attn_s, y_ssm_s, ps, x_sample, mod_s, lw, gpb, 1)

    def state_out(st):
        b = st.shape[0]
        return st.transpose(0, 1, 3, 2).reshape(1, b, SSM_HEADS, SSM_P, SSM_N)

    k_prompt = pp["k"].reshape(1, batch, SB_HEADS, SB_HEAD_DIM, seq).transpose(0, 1, 4, 2, 3)
    v_prompt = pp["v"].reshape(1, batch, SB_HEADS, SB_HEAD_DIM, seq).transpose(0, 1, 4, 2, 3)
    conv_prompt = tail_p[:, SUBLANES - (CONV_W - 1):, :][None]
    k_sample = ps["k"].reshape(1, n_seq, dec_seq, SB_HEADS, SB_HEAD_DIM)
    v_sample = ps["v"].reshape(1, n_seq, dec_seq, SB_HEADS, SB_HEAD_DIM)
    conv_sample = ps["xbc"].reshape(n_seq, dec_seq, CONV_DIM)[:, dec_seq - (CONV_W - 1):, :][None]
    return (y_prompt, y_sample, k_prompt, v_prompt, conv_prompt, state_out(st_p), k_sample, v_sample, conv_sample,
            state_out(st_s))
```

```python
import functools
import math

import jax
import jax.numpy as jnp
from jax import lax
from jax.experimental import pallas as pl
from jax.experimental.pallas import tpu as pltpu

F32 = jnp.float32
BF16 = jnp.bfloat16

D_MODEL = 1024
EPS = 1e-6
SB_HEADS = 8
SB_HEAD_DIM = 64
SB_WIDTH = SB_HEADS * SB_HEAD_DIM
PAGE = 128
SSM_HEADS = 16
SSM_P = 64
SSM_GROUPS = 2
SSM_N = 128
SSM_INNER = SSM_HEADS * SSM_P
SSM_PAIRS = SSM_HEADS // 2
CONV_W = 4
CONV_DIM = SSM_INNER + 2 * SSM_GROUPS * SSM_N
SSM_CHUNK = 128
PEER_HEADS = 8
PEER_KEYS = 128
PEER_HALF = 128
PEER_TOPK = 16

LANES = 128
SUBLANES = 8
VMEM_LIMIT = 56 * 1024 * 1024

_PROJ_GROUPS = (("q", SB_WIDTH), ("k", SB_WIDTH), ("v", SB_WIDTH), ("z", SSM_INNER), ("xbc", CONV_DIM),
                ("dt", LANES), ("ga", D_MODEL), ("gs", D_MODEL))
_PROJ_COLS = sum(w for _, w in _PROJ_GROUPS)


def _params(*sem):
    return pltpu.CompilerParams(dimension_semantics=sem, vmem_limit_bytes=VMEM_LIMIT)


def _dot(a, b):
    return jnp.dot(a, b, preferred_element_type=F32)


def _dot_nt(a, b):
    return lax.dot_general(a, b, (((1,), (1,)), ((), ())), preferred_element_type=F32)


def _split3(x):
    hi = x.astype(BF16)
    r = x - hi.astype(F32)
    mid = r.astype(BF16)
    lo = (r - mid.astype(F32)).astype(BF16)
    return hi, mid, lo


def _dot3_l(x, m):
    hi, mid, lo = _split3(x)
    return _dot(hi, m) + _dot(mid, m) + _dot(lo, m)


def _dot3_r(m, x):
    hi, mid, lo = _split3(x)
    return _dot(m, hi) + _dot(m, mid) + _dot(m, lo)


def _softplus(x):
    return jnp.maximum(x, 0.0) + jnp.log1p(jnp.exp(-jnp.abs(x)))


def _silu(x):
    return x * jax.nn.sigmoid(x)


def _adaln_kernel(c_ref, w_ref, b_ref, o_ref):
    s = _silu(c_ref[...]).astype(BF16)
    o_ref[...] = _dot(s, w_ref[...].astype(BF16)) + b_ref[...]


def _adaln(c_all, w_ada, b_ada):
    m, n = c_all.shape[0], w_ada.shape[1]
    tn = 1024
    return pl.pallas_call(
        _adaln_kernel,
        grid=(n // tn,),
        in_specs=[pl.BlockSpec((m, D_MODEL), lambda j: (0, 0)),
                  pl.BlockSpec((D_MODEL, tn), lambda j: (0, j)),
                  pl.BlockSpec((1, tn), lambda j: (0, j))],
        out_specs=pl.BlockSpec((m, tn), lambda j: (0, j)),
        out_shape=jax.ShapeDtypeStruct((m, n), F32),
        compiler_params=_params("arbitrary"),
        name="adaln",
    )(c_all, w_ada, b_ada.reshape(1, n))


def _inproj_kernel(kv_transposed, x_ref, sh_ref, sc_ref, nw_ref, w_ref, q_ref, k_ref, v_ref, kb_ref, vb_ref, z_ref,
                   xbc_ref, dt_ref, ga_ref, gs_ref):
    x = x_ref[...]
    g, r, d = x.shape
    y = x * lax.rsqrt(jnp.mean(x * x, axis=-1, keepdims=True) + EPS) * nw_ref[...]
    h = y * (1.0 + sc_ref[...]) + sh_ref[...]
    hb = h.reshape(g * r, d).astype(BF16)
    outs = {"q": q_ref, "k": k_ref, "v": v_ref, "z": z_ref, "xbc": xbc_ref, "dt": dt_ref, "ga": ga_ref, "gs": gs_ref}
    col = 0
    for name, width in _PROJ_GROUPS:
        res = _dot(hb, w_ref[:, col:col + width])
        col += width
        if name == "q":
            res = res * (SB_HEAD_DIM ** -0.5)
        if name in ("k", "v") and kv_transposed:
            outs[name][0] = res.T
        else:
            outs[name][...] = res.astype(outs[name].dtype)
        if name == "k":
            kb_ref[...] = res.astype(BF16)
        if name == "v":
            vb_ref[...] = res.astype(BF16)


def _inproj(x3, mod3, norm_w, w_packed, groups_per_block, blocks_per_mod, kv_transposed=False):
    ng, r, d = x3.shape
    gpb = groups_per_block
    nblk = ng // gpb
    tm = gpb * r
    t = ng * r
    if gpb == 1:
        mod_map = lambda col: (lambda i: (i // blocks_per_mod, 0, col))
    else:
        mod_map = lambda col: (lambda i: (i, 0, col))
    widths = dict(_PROJ_GROUPS)
    out_shapes = [
        jax.ShapeDtypeStruct((t, widths["q"]), F32), jax.ShapeDtypeStruct((t, widths["k"]), F32),
        jax.ShapeDtypeStruct((t, widths["v"]), F32), jax.ShapeDtypeStruct((t, widths["k"]), BF16),
        jax.ShapeDtypeStruct((t, widths["v"]), BF16), jax.ShapeDtypeStruct((t, widths["z"]), F32),
        jax.ShapeDtypeStruct((t, widths["xbc"]), F32), jax.ShapeDtypeStruct((t, widths["dt"]), F32),
        jax.ShapeDtypeStruct((t, widths["ga"]), F32), jax.ShapeDtypeStruct((t, widths["gs"]), F32)]
    out_specs = [pl.BlockSpec((tm, s.shape[1]), lambda i: (i, 0)) for s in out_shapes]
    if kv_transposed:
        assert gpb == 1
        for idx in (1, 2):
            out_shapes[idx] = jax.ShapeDtypeStruct((ng // blocks_per_mod, SB_WIDTH, blocks_per_mod * r), F32)
            out_specs[idx] = pl.BlockSpec((1, SB_WIDTH, tm), lambda i: (i // blocks_per_mod, 0, i % blocks_per_mod))
    return pl.pallas_call(
        functools.partial(_inproj_kernel, kv_transposed),
        grid=(nblk,),
        in_specs=[pl.BlockSpec((gpb, r, d), lambda i: (i, 0, 0)),
                  pl.BlockSpec((gpb, 1, d), mod_map(0)),
                  pl.BlockSpec((gpb, 1, d), mod_map(1)),
                  pl.BlockSpec((1, 1, d), lambda i: (0, 0, 0)),
                  pl.BlockSpec((d, _PROJ_COLS), lambda i: (0, 0))],
        out_specs=out_specs,
        out_shape=out_shapes,
        compiler_params=_params("arbitrary"),
        name="inproj",
    )(x3, mod3, mod3, norm_w.reshape(1, 1, d), w_packed)


ATT_TQ = 512
ATT_TK = 512
ATT_CUM = 256
ATT_HEADS = 2


def _stick_incl(z, mask, tri):
    sp = jnp.maximum(z, 0.0) + jnp.log(1.0 + jnp.exp(-jnp.abs(z)))
    if mask is not None:
        sp = jnp.where(mask, sp, 0.0)
    hi = sp.astype(BF16)
    lo = (sp - hi.astype(F32)).astype(BF16)
    n, cols = z.shape
    wd = tri.shape[0]
    stacked = jnp.concatenate([hi, lo], axis=0)
    chunks = []
    carry = None
    for c in range(cols // wd - 1, -1, -1):
        both = _dot(stacked[:, c * wd:(c + 1) * wd], tri)
        inc = both[:n] + both[n:]
        if carry is not None:
            inc = inc + carry
        carry = inc[:, 0:1]
        chunks.append(inc)
    return chunks[0] if len(chunks) == 1 else jnp.concatenate(chunks[::-1], axis=1)


def _stick_tile(z, mask, tri, cum):
    incl = _stick_incl(z, mask, tri)
    w = jnp.exp(z - incl - cum)
    if mask is not None:
        w = jnp.where(mask, w, 0.0)
    return w, cum + incl[:, 0:1]


def _attn_prompt_kernel(bias_ref, q_ref, k_ref, v_ref, tri_ref, o_ref):
    hg = pl.program_id(1)
    qi = pl.program_id(2)
    tq, tk = ATT_TQ, ATT_TK
    ratio = tq // tk
    width = ATT_HEADS * SB_HEAD_DIM
    q = q_ref[...].astype(BF16)
    lane_head = lax.broadcasted_iota(jnp.int32, (tq, width), 1) // SB_HEAD_DIM
    rel = lax.broadcasted_iota(jnp.int32, (tq, tk), 0) - lax.broadcasted_iota(jnp.int32, (tq, tk), 1)
    tri = tri_ref[...]
    qhs = [jnp.where(lane_head == hh, q, jnp.zeros_like(q)) for hh in range(ATT_HEADS)]
    biases = [bias_ref[hg * ATT_HEADS + hh] for hh in range(ATT_HEADS)]

    def block(kb, carry, mask):
        off = pl.multiple_of(kb * tk, tk)
        kblk = k_ref[pl.ds(off, tk), :]
        vblk = v_ref[pl.ds(off, tk), :]
        out = []
        for hh in range(ATT_HEADS):
            acc, cum = carry[hh]
            z = _dot_nt(qhs[hh], kblk) + biases[hh]
            w, cum = _stick_tile(z, mask, tri, cum)
            out.append((acc + _dot(w.astype(BF16), vblk), cum))
        return tuple(out)

    carry = tuple((jnp.zeros((tq, width), F32), jnp.zeros((tq, 1), F32)) for _ in range(ATT_HEADS))
    for d in range(ratio - 1, -1, -1):
        carry = block(qi * ratio + d, carry, rel > d * tk)
    res = lax.fori_loop(0, qi * ratio, lambda step, c: block(qi * ratio - 1 - step, c, None), carry)
    o = res[0][0]
    for hh in range(1, ATT_HEADS):
        o = jnp.where(lane_head == hh, res[hh][0], o)
    o_ref[...] = o.astype(o_ref.dtype)


def _tri_incl(n):
    r = lax.broadcasted_iota(jnp.int32, (n, n), 0)
    c = lax.broadcasted_iota(jnp.int32, (n, n), 1)
    return (r >= c).astype(BF16)


def _attn_prompt(q, kb, vb, sb_bias, batch, seq):
    tq = ATT_TQ
    nq = seq // tq
    width = ATT_HEADS * SB_HEAD_DIM
    return pl.pallas_call(
        _attn_prompt_kernel,
        grid=(batch, SB_HEADS // ATT_HEADS, nq),
        in_specs=[pl.BlockSpec(memory_space=pltpu.SMEM),
                  pl.BlockSpec((tq, width), lambda b, h, i: (b * nq + i, h)),
                  pl.BlockSpec((seq, width), lambda b, h, i: (b, h)),
                  pl.BlockSpec((seq, width), lambda b, h, i: (b, h)),
                  pl.BlockSpec((ATT_CUM, ATT_CUM), lambda b, h, i: (0, 0))],
        out_specs=pl.BlockSpec((tq, width), lambda b, h, i: (b * nq + i, h)),
        out_shape=jax.ShapeDtypeStruct((batch * seq, SB_WIDTH), BF16),
        compiler_params=_params("arbitrary", "arbitrary", "arbitrary"),
        name="attn_prompt",
    )(sb_bias, q, kb, vb, _tri_incl(ATT_CUM))


def _attn_sample_kernel(n_pages, dec_seq, pt_ref, bias_ref, q_ref, kn_ref, vn_ref, tri_ref, *refs):
    k_pages = refs[:n_pages]
    v_pages = refs[n_pages:2 * n_pages]
    o_ref = refs[2 * n_pages]
    rows = SB_HEADS * dec_seq
    q = q_ref[...]
    qt = jnp.concatenate([q] * SB_HEADS, axis=0)
    row_head = lax.broadcasted_iota(jnp.int32, (rows, SB_WIDTH), 0) // dec_seq
    lane_head = lax.broadcasted_iota(jnp.int32, (rows, SB_WIDTH), 1) // SB_HEAD_DIM
    qbd = jnp.where(row_head == lane_head, qt, 0.0).astype(BF16)
    bias = bias_ref[...]
    tri = tri_ref[...]
    r_i = lax.broadcasted_iota(jnp.int32, (rows, PAGE), 0) % dec_seq
    c_i = lax.broadcasted_iota(jnp.int32, (rows, PAGE), 1)
    new_valid = c_i < r_i

    def pad_rows(a):
        return jnp.concatenate([a, jnp.zeros((PAGE - dec_seq, SB_WIDTH), a.dtype)], axis=0)

    def page_tile(ref):
        return ref[0].reshape(SB_WIDTH, PAGE).astype(BF16)

    zs = [_dot_nt(qbd, pad_rows(kn_ref[...]).astype(BF16)) + bias]
    zs += [_dot(qbd, page_tile(k_pages[p])) + bias for p in range(n_pages - 1, -1, -1)]
    masks = [new_valid] + [None] * n_pages
    incls = [_stick_incl(z, m, tri) for z, m in zip(zs, masks)]
    cum = jnp.zeros((rows, 1), F32)
    acc = None
    for b, (z, m, incl) in enumerate(zip(zs, masks, incls)):
        w = jnp.exp(z - incl - cum)
        cum = cum + incl[:, 0:1]
        if m is not None:
            w = jnp.where(m, w, 0.0)
        wb = w.astype(BF16)
        if b == 0:
            contrib = _dot(wb, pad_rows(vn_ref[...]).astype(BF16))
        else:
            contrib = _dot_nt(wb, page_tile(v_pages[n_pages - b]))
        acc = contrib if acc is None else acc + contrib
    out = jnp.zeros((dec_seq, SB_WIDTH), F32)
    lane_head8 = lax.broadcasted_iota(jnp.int32, (dec_seq, SB_WIDTH), 1) // SB_HEAD_DIM
    for h in range(SB_HEADS):
        out = out + jnp.where(lane_head8 == h, acc[h * dec_seq:(h + 1) * dec_seq, :], 0.0)
    o_ref[...] = out.astype(o_ref.dtype)


def _attn_sample(q, k_new, v_new, cache_k, cache_v, page_table, sb_bias, dec_seq):
    n_seq, n_pages = page_table.shape
    rows = SB_HEADS * dec_seq
    bias_rows = jnp.broadcast_to(jnp.repeat(sb_bias, dec_seq)[:, None], (rows, PAGE))

    def page_spec(p):
        return pl.BlockSpec((1, SB_HEADS, SB_HEAD_DIM, PAGE), lambda s, pt: (pt[s, p], 0, 0, 0))

    tok_spec = pl.BlockSpec((dec_seq, SB_WIDTH), lambda s, pt: (s, 0))
    grid_spec = pltpu.PrefetchScalarGridSpec(
        num_scalar_prefetch=1,
        grid=(n_seq,),
        in_specs=[pl.BlockSpec((rows, PAGE), lambda s, pt: (0, 0)), tok_spec, tok_spec, tok_spec,
                  pl.BlockSpec((PAGE, PAGE), lambda s, pt: (0, 0))]
                 + [page_spec(p) for p in range(n_pages)] + [page_spec(p) for p in range(n_pages)],
        out_specs=tok_spec,
    )
    return pl.pallas_call(
        functools.partial(_attn_sample_kernel, n_pages, dec_seq),
        grid_spec=grid_spec,
        out_shape=jax.ShapeDtypeStruct((n_seq * dec_seq, SB_WIDTH), F32),
        compiler_params=_params("arbitrary"),
        name="attn_sample",
    )(page_table, bias_rows, q, k_new, v_new, _tri_incl(PAGE), *([cache_k] * n_pages), *([cache_v] * n_pages))


def _expand_matrix(reps):
    r = lax.broadcasted_iota(jnp.int32, (LANES, SSM_HEADS * reps), 0)
    c = lax.broadcasted_iota(jnp.int32, (LANES, SSM_HEADS * reps), 1)
    return (c // reps == r).astype(BF16)


def _ssd_core(act, dt_raw, z, valid, tri, e1, e2, dtb, a_row, dsk, nw, state_in, c_masks):
    xs = act[:, :SSM_INNER]
    bm = act[:, SSM_INNER:SSM_INNER + SSM_GROUPS * SSM_N]
    cm = act[:, SSM_INNER + SSM_GROUPS * SSM_N:]
    dt = _softplus(dt_raw + dtb)
    dta = dt * a_row
    acs = _dot3_r(tri, dta)
    acs_t = acs.T
    acs_e2 = _dot3_l(acs, e2)
    dt_e = _dot3_l(dt, e1)
    eacs_e = _dot3_l(jnp.exp(acs), e1)
    n_slots = len(state_in)
    dend_e = []
    cdec_e = []
    for s in range(n_slots):
        if c_masks[s] is None:
            last = acs[SSM_CHUNK - 1:SSM_CHUNK, :]
        else:
            last = jnp.min(jnp.where(c_masks[s], acs, jnp.inf), axis=0, keepdims=True)
        de = jnp.exp(last - acs)
        if c_masks[s] is not None:
            de = jnp.where(c_masks[s], de, 0.0)
        dend_e.append(_dot3_l(de, e1))
        cdec_e.append(_dot3_l(jnp.broadcast_to(jnp.exp(last), (SUBLANES, LANES)), e1)[0:1, :])
    bmb = [bm[:, g * SSM_N:(g + 1) * SSM_N] for g in range(SSM_GROUPS)]
    cmb = [cm[:, g * SSM_N:(g + 1) * SSM_N].astype(BF16) for g in range(SSM_GROUPS)]
    cb = [_dot_nt(cmb[g], bmb[g].astype(BF16)) for g in range(SSM_GROUPS)]
    bt = [bmb[g].T.astype(BF16) for g in range(SSM_GROUPS)]
    lane = lax.broadcasted_iota(jnp.int32, (SSM_CHUNK, LANES), 1)
    y_pairs = []
    new_states = [[None] * SSM_PAIRS for _ in range(n_slots)]
    for k in range(SSM_PAIRS):
        g = (2 * k) // (SSM_HEADS // SSM_GROUPS)
        sl = slice(k * LANES, (k + 1) * LANES)
        xdt = xs[:, sl] * dt_e[:, sl]
        xdt_b = xdt.astype(BF16)
        parts = []
        for hh in range(2):
            h = 2 * k + hh
            seg = acs_e2[:, h * LANES:(h + 1) * LANES] - acs_t[h:h + 1, :]
            decay = jnp.where(valid, jnp.exp(jnp.where(valid, seg, 0.0)), 0.0)
            parts.append(_dot((cb[g] * decay).astype(BF16), xdt_b))
        y = jnp.where(lane < SSM_P, parts[0], parts[1])
        y_off = None
        for s in range(n_slots):
            st = state_in[s][k]
            c_s = cmb[g] if c_masks[s] is None else jnp.where(c_masks[s], cmb[g], jnp.zeros_like(cmb[g]))
            contrib = _dot(c_s, st.astype(BF16))
            y_off = contrib if y_off is None else y_off + contrib
            xdd = (xdt * dend_e[s][:, sl]).astype(BF16)
            new_states[s][k] = st * cdec_e[s][:, sl] + _dot(bt[g], xdd)
        y = y + y_off * eacs_e[:, sl] + dsk[:, sl] * xs[:, sl]
        y_pairs.append(y)
    y = jnp.concatenate(y_pairs, axis=1) * _silu(z)
    gsz = SSM_INNER // SSM_GROUPS
    outs = []
    for g in range(SSM_GROUPS):
        yg = y[:, g * gsz:(g + 1) * gsz]
        outs.append(yg * lax.rsqrt(jnp.mean(yg * yg, axis=-1, keepdims=True) + EPS) * nw[:, g * gsz:(g + 1) * gsz])
    return jnp.concatenate(outs, axis=1), new_states


def _ssd_prompt_kernel(xbc_ref, z_ref, dt_ref, cw_ref, cbias_ref, dtb_ref, a_ref, dsk_ref, nw_ref, tri_ref, e1_ref,
                       e2_ref, y_ref, tail_ref, st_ref, xs_scr, st_scr):
    c = pl.program_id(1)

    @pl.when(c == 0)
    def _():
        xs_scr[0:SUBLANES, :] = jnp.zeros((SUBLANES, CONV_DIM), F32)
        st_scr[...] = jnp.zeros_like(st_scr)

    xs_scr[SUBLANES:SUBLANES + SSM_CHUNK, :] = xbc_ref[...]
    conv = cbias_ref[...]
    for i in range(CONV_W):
        off = SUBLANES - (CONV_W - 1) + i
        conv = conv + xs_scr[off:off + SSM_CHUNK, :] * cw_ref[i:i + 1, :]
    tail = xs_scr[SSM_CHUNK:SSM_CHUNK + SUBLANES, :]
    xs_scr[0:SUBLANES, :] = tail
    tail_ref[0] = tail
    act = _silu(conv)
    r = lax.broadcasted_iota(jnp.int32, (SSM_CHUNK, SSM_CHUNK), 0)
    cc = lax.broadcasted_iota(jnp.int32, (SSM_CHUNK, SSM_CHUNK), 1)
    valid = r >= cc
    state_in = [[st_scr[k] for k in range(SSM_PAIRS)]]
    y, new_states = _ssd_core(act, dt_ref[...], z_ref[...], valid, tri_ref[...], e1_ref[...], e2_ref[...],
                              dtb_ref[...], a_ref[...], dsk_ref[...], nw_ref[...], state_in, [None])
    for k in range(SSM_PAIRS):
        st_scr[k] = new_states[0][k]
        st_ref[0, k] = new_states[0][k]
    y_ref[...] = y.astype(y_ref.dtype)


def _ssd_consts(conv_w, conv_b, dt_bias, a_log, d_skip, ssm_norm_w):
    pad = LANES - SSM_HEADS
    cw = jnp.concatenate([conv_w, jnp.zeros((SUBLANES - CONV_W, CONV_DIM), F32)], axis=0)
    dtb = jnp.pad(dt_bias, (0, pad)).reshape(1, LANES)
    a_row = jnp.pad(-jnp.exp(a_log), (0, pad)).reshape(1, LANES)
    dsk = jnp.repeat(d_skip, SSM_P).reshape(1, SSM_INNER)
    return cw, conv_b.reshape(1, CONV_DIM), dtb, a_row, dsk, ssm_norm_w.reshape(1, SSM_INNER)


def _const_spec(shape, ngrid):
    zeros = (0,) * len(shape)
    if ngrid == 1:
        return pl.BlockSpec(shape, lambda i: zeros)
    return pl.BlockSpec(shape, lambda i, j: zeros)


def _ssd_prompt(xbc, z, dt, consts, batch, seq):
    nc = seq // SSM_CHUNK
    cw, cbias, dtb, a_row, dsk, nw = consts
    tok = lambda w: pl.BlockSpec((SSM_CHUNK, w), lambda b, c: (b * nc + c, 0))
    tri = _tri_incl(SSM_CHUNK)
    return pl.pallas_call(
        _ssd_prompt_kernel,
        grid=(batch, nc),
        in_specs=[tok(CONV_DIM), tok(SSM_INNER), tok(LANES),
                  _const_spec(cw.shape, 2), _const_spec(cbias.shape, 2), _const_spec(dtb.shape, 2),
                  _const_spec(a_row.shape, 2), _const_spec(dsk.shape, 2), _const_spec(nw.shape, 2),
                  _const_spec((SSM_CHUNK, SSM_CHUNK), 2), _const_spec((LANES, SSM_INNER), 2),
                  _const_spec((LANES, SSM_HEADS * LANES), 2)],
        out_specs=[tok(SSM_INNER),
                   pl.BlockSpec((1, SUBLANES, CONV_DIM), lambda b, c: (b, 0, 0)),
                   pl.BlockSpec((1, SSM_PAIRS, SSM_N, LANES), lambda b, c: (b, 0, 0, 0))],
        out_shape=[jax.ShapeDtypeStruct((batch * seq, SSM_INNER), BF16),
                   jax.ShapeDtypeStruct((batch, SUBLANES, CONV_DIM), F32),
                   jax.ShapeDtypeStruct((batch, SSM_PAIRS, SSM_N, LANES), F32)],
        scratch_shapes=[pltpu.VMEM((SSM_CHUNK + SUBLANES, CONV_DIM), F32),
                        pltpu.VMEM((SSM_PAIRS, SSM_N, LANES), F32)],
        compiler_params=_params("arbitrary", "arbitrary"),
        name="ssd_prompt",
    )(xbc, z, dt, cw, cbias, dtb, a_row, dsk, nw, tri, _expand_matrix(SSM_P), _expand_matrix(LANES))


def _ssd_sample_kernel(dec_seq, xbc_ref, prev_ref, z_ref, dt_ref, st_in_ref, cw_ref, cbias_ref, dtb_ref, a_ref,
                       dsk_ref, nw_ref, tri_ref, e1_ref, e2_ref, y_ref, st_ref):
    n_slots = SSM_CHUNK // dec_seq
    cur = xbc_ref[...]
    prev = prev_ref[...]
    pos = lax.broadcasted_iota(jnp.int32, (SSM_CHUNK, CONV_DIM), 0) % dec_seq
    conv = cbias_ref[...] + cur * cw_ref[CONV_W - 1:CONV_W, :]
    for k in range(1, CONV_W):
        shifted = jnp.where(pos < k, pltpu.roll(prev, SSM_CHUNK - dec_seq + k, axis=0), pltpu.roll(cur, k, axis=0))
        conv = conv + shifted * cw_ref[CONV_W - 1 - k:CONV_W - k, :]
    act = _silu(conv)
    r = lax.broadcasted_iota(jnp.int32, (SSM_CHUNK, SSM_CHUNK), 0)
    cc = lax.broadcasted_iota(jnp.int32, (SSM_CHUNK, SSM_CHUNK), 1)
    valid = (r >= cc) & ((r // dec_seq) == (cc // dec_seq))
    row_seq = lax.broadcasted_iota(jnp.int32, (SSM_CHUNK, LANES), 0) // dec_seq
    c_masks = [row_seq == s for s in range(n_slots)]
    state_in = [[st_in_ref[s, k] for k in range(SSM_PAIRS)] for s in range(n_slots)]
    y, new_states = _ssd_core(act, dt_ref[...], z_ref[...], valid, tri_ref[...], e1_ref[...], e2_ref[...],
                              dtb_ref[...], a_ref[...], dsk_ref[...], nw_ref[...], state_in, c_masks)
    for s in range(n_slots):
        for k in range(SSM_PAIRS):
            st_ref[s, k] = new_states[s][k]
    y_ref[...] = y.astype(y_ref.dtype)


def _ssd_sample(xbc, prev, z, dt, state_t, consts, dec_seq):
    n_tok = xbc.shape[0]
    n_slots = SSM_CHUNK // dec_seq
    nblk = n_tok // SSM_CHUNK
    cw, cbias, dtb, a_row, dsk, nw = consts
    tok = lambda w: pl.BlockSpec((SSM_CHUNK, w), lambda i: (i, 0))
    st_spec = pl.BlockSpec((n_slots, SSM_PAIRS, SSM_N, LANES), lambda i: (i, 0, 0, 0))
    r = lax.broadcasted_iota(jnp.int32, (SSM_CHUNK, SSM_CHUNK), 0)
    c = lax.broadcasted_iota(jnp.int32, (SSM_CHUNK, SSM_CHUNK), 1)
    tri = ((r >= c) & ((r // dec_seq) == (c // dec_seq))).astype(BF16)
    return pl.pallas_call(
        functools.partial(_ssd_sample_kernel, dec_seq),
        grid=(nblk,),
        in_specs=[tok(CONV_DIM), tok(CONV_DIM), tok(SSM_INNER), tok(LANES), st_spec,
                  _const_spec(cw.shape, 1), _const_spec(cbias.shape, 1), _const_spec(dtb.shape, 1),
                  _const_spec(a_row.shape, 1), _const_spec(dsk.shape, 1), _const_spec(nw.shape, 1),
                  _const_spec((SSM_CHUNK, SSM_CHUNK), 1), _const_spec((LANES, SSM_INNER), 1),
                  _const_spec((LANES, SSM_HEADS * LANES), 1)],
        out_specs=[tok(SSM_INNER), st_spec],
        out_shape=[jax.ShapeDtypeStruct((n_tok, SSM_INNER), BF16),
                   jax.ShapeDtypeStruct(state_t.shape, F32)],
        compiler_params=_params("arbitrary"),
        name="ssd_sample",
    )(xbc, prev, z, dt, state_t, cw, cbias, dtb, a_row, dsk, nw, tri, _expand_matrix(SSM_P), _expand_matrix(LANES))


def _merge_kernel(o_ref, y_ref, ga_ref, gs_ref, x_ref, g1_ref, sh_ref, sc_ref, nw_ref, wa_ref, ws_ref, wo_ref,
                  x1_ref, h2_ref):
    merged = (jax.nn.sigmoid(ga_ref[...]) * _dot(o_ref[...].astype(BF16), wa_ref[...])
              + jax.nn.sigmoid(gs_ref[...]) * _dot(y_ref[...], ws_ref[...]))
    mix = _dot(merged.astype(BF16), wo_ref[...])
    x = x_ref[...]
    g, r, d = x.shape
    x1 = x + g1_ref[...] * mix.reshape(g, r, d)
    x1_ref[...] = x1
    y = x1 * lax.rsqrt(jnp.mean(x1 * x1, axis=-1, keepdims=True) + EPS) * nw_ref[...]
    h2 = y * (1.0 + sc_ref[...]) + sh_ref[...]
    h2_ref[...] = h2.reshape(g * r, d).astype(h2_ref.dtype)


def _mod_map(gpb, blocks_per_mod, col):
    if gpb == 1:
        return lambda i: (i // blocks_per_mod, 0, col)
    return lambda i: (i, 0, col)


def _merge(o_attn, y_ssm, ga, gs, x3, mod3, norm2_w, wa, ws, wo, groups_per_block, blocks_per_mod):
    ng, r, d = x3.shape
    gpb = groups_per_block
    nblk = ng // gpb
    tm = gpb * r
    t = ng * r
    tok = lambda w: pl.BlockSpec((tm, w), lambda i: (i, 0))
    xspec = pl.BlockSpec((gpb, r, d), lambda i: (i, 0, 0))
    mspec = lambda col: pl.BlockSpec((gpb, 1, d), _mod_map(gpb, blocks_per_mod, col))
    return pl.pallas_call(
        _merge_kernel,
        grid=(nblk,),
        in_specs=[tok(SB_WIDTH), tok(SSM_INNER), tok(d), tok(d), xspec, mspec(2), mspec(3), mspec(4),
                  pl.BlockSpec((1, 1, d), lambda i: (0, 0, 0)),
                  _const_spec(wa.shape, 1), _const_spec(ws.shape, 1), _const_spec(wo.shape, 1)],
        out_specs=[xspec, tok(d)],
        out_shape=[jax.ShapeDtypeStruct((ng, r, d), F32), jax.ShapeDtypeStruct((t, d), BF16)],
        compiler_params=_params("arbitrary"),
        name="merge",
    )(o_attn, y_ssm, ga, gs, x3, mod3, mod3, mod3, norm2_w.reshape(1, 1, d), wa, ws, wo)


def _batcher_pairs(n):
    pairs = []
    p = 1
    while p < n:
        k = p
        while k >= 1:
            for j in range(k % p, n - k, 2 * k):
                for i in range(min(k, n - j - k)):
                    if (i + j) // (2 * p) == (i + j + k) // (2 * p):
                        pairs.append((i + j, i + j + k))
            k //= 2
        p *= 2
    return pairs


_SORT16 = _batcher_pairs(PEER_TOPK)


def _sort_desc(vals):
    vals = list(vals)
    for i, j in _SORT16:
        hi = jnp.maximum(vals[i], vals[j])
        lo = jnp.minimum(vals[i], vals[j])
        vals[i], vals[j] = hi, lo
    return vals


def _merge_top(a, b, sort_result=True):
    n = PEER_TOPK
    t = [jnp.maximum(a[k], b[n - 1 - k]) for k in range(n)]
    if not sort_result:
        return t
    d = n // 2
    while d >= 1:
        for k in range(n):
            if (k // d) % 2 == 0:
                hi = jnp.maximum(t[k], t[k + d])
                lo = jnp.minimum(t[k], t[k + d])
                t[k], t[k + d] = hi, lo
        d //= 2
    return t


def _top16_desc(vals, need_sorted=True):
    groups = [_sort_desc(vals[i:i + PEER_TOPK]) for i in range(0, len(vals), PEER_TOPK)]
    while len(groups) > 1:
        nxt = []
        for i in range(0, len(groups) - 1, 2):
            last = len(groups) == 2
            nxt.append(_merge_top(groups[i], groups[i + 1], sort_result=need_sorted or not last))
        if len(groups) % 2:
            nxt.append(groups[-1])
        groups = nxt
    return groups[0]


ROUTE_TM = 512


def _route_kernel(h_ref, wq_ref, k1_ref, k2_ref, th_ref, p_ref, code_ref, r_ref, s1_scr, s2_scr):
    half = PEER_HEADS * PEER_HALF
    qt = _dot_nt(wq_ref[...], h_ref[...])
    s1 = _dot(k1_ref[...], qt[:half].astype(BF16))
    s2 = _dot(k2_ref[...], qt[half:].astype(BF16))
    for c in range(ROUTE_TM // LANES):
        s1_scr[c] = s1[:, c * LANES:(c + 1) * LANES]
        s2_scr[c] = s2[:, c * LANES:(c + 1) * LANES]
    staircase = [(a, b) for a in range(PEER_TOPK) for b in range(PEER_TOPK) if (a + 1) * (b + 1) <= PEER_TOPK]
    neg_inf = jnp.full((SUBLANES, LANES), -jnp.inf, F32)
    for c in range(ROUTE_TM // LANES):
        tl = pl.ds(c * LANES, LANES)
        row = lambda scr, i, c=c: scr[c, i * SUBLANES:(i + 1) * SUBLANES, :]
        top1 = _top16_desc([row(s1_scr, i) for i in range(PEER_KEYS)])
        top2 = _top16_desc([row(s2_scr, i) for i in range(PEER_KEYS)])
        cand = {(a, b): top1[a] + top2[b] for a, b in staircase}
        cvals = [cand[ab] for ab in staircase]
        cvals += [neg_inf] * (-len(cvals) % PEER_TOPK)
        tau = _top16_desc(cvals, need_sorted=False)
        thr = tau[0]
        for v in tau[1:]:
            thr = jnp.minimum(thr, v)
        cmax = cand[(0, 0)]
        zsum = jnp.zeros((SUBLANES, LANES), F32)
        never = jnp.full((SUBLANES, LANES), PEER_TOPK + 1.0, F32)
        need = [never] * PEER_TOPK
        for a, b in staircase:
            ok = cand[(a, b)] >= thr
            zsum = zsum + jnp.where(ok, jnp.exp(cand[(a, b)] - cmax), 0.0)
            need[a] = need[a] - jnp.where(ok, 1.0, 0.0)
        inv_z = 1.0 / zsum
        for i in range(PEER_KEYS):
            s = row(s1_scr, i)
            th = never
            for a in range(PEER_TOPK):
                th = jnp.where(s == top1[a], need[a], th)
            th_ref[i, :, tl] = th
            p_ref[i, :, tl] = jnp.exp(s - top1[0]) * inv_z
        for h in range(PEER_HEADS):
            s2h = s2_scr[c, pl.ds(h, PEER_KEYS, stride=SUBLANES), :]
            code = jnp.zeros((PEER_KEYS, LANES), F32)
            for b in range(PEER_TOPK):
                code = code + jnp.where(s2h >= top2[b][h:h + 1, :], 1.0, 0.0)
            tiles = (PEER_KEYS // BF16_ROWS, BF16_ROWS, LANES)
            code_ref[h, :, :, tl] = code.reshape(tiles).astype(BF16)
            r_ref[h, :, :, tl] = jnp.exp(s2h - top2[0][h:h + 1, :]).reshape(tiles).astype(BF16)


def _route(h2, wq_t, kbd1, kbd2):
    t = h2.shape[0]
    tm = ROUTE_TM
    half = PEER_HEADS * PEER_HALF
    kt = PEER_KEYS // BF16_ROWS
    return pl.pallas_call(
        _route_kernel,
        grid=(t // tm,),
        in_specs=[pl.BlockSpec((tm, D_MODEL), lambda i: (i, 0)),
                  _const_spec(wq_t.shape, 1), _const_spec(kbd1.shape, 1), _const_spec(kbd2.shape, 1)],
        out_specs=[pl.BlockSpec((PEER_KEYS, PEER_HEADS, tm), lambda i: (0, 0, i)),
                   pl.BlockSpec((PEER_KEYS, PEER_HEADS, tm), lambda i: (0, 0, i)),
                   pl.BlockSpec((PEER_HEADS, kt, BF16_ROWS, tm), lambda i: (0, 0, 0, i)),
                   pl.BlockSpec((PEER_HEADS, kt, BF16_ROWS, tm), lambda i: (0, 0, 0, i))],
        out_shape=[jax.ShapeDtypeStruct((PEER_KEYS, PEER_HEADS, t), F32),
                   jax.ShapeDtypeStruct((PEER_KEYS, PEER_HEADS, t), F32),
                   jax.ShapeDtypeStruct((PEER_HEADS, kt, BF16_ROWS, t), BF16),
                   jax.ShapeDtypeStruct((PEER_HEADS, kt, BF16_ROWS, t), BF16)],
        scratch_shapes=[pltpu.VMEM((tm // LANES, half, LANES), F32), pltpu.VMEM((tm // LANES, half, LANES), F32)],
        compiler_params=_params("arbitrary"),
        name="peer_route",
    )(h2, wq_t, kbd1, kbd2)


PEER_TM = 1024
PEER_KB = 8
PEER_SUB = 8
BF16_ROWS = 16


def _gelu_tanh(x):
    return 0.5 * x * (1.0 + jnp.tanh(math.sqrt(2.0 / math.pi) * (x + 0.044715 * (x * x * x))))


def _peer_kernel(h_ref, u_ref, vt_ref, th_ref, p_ref, code_ref, r_ref, x_ref, g2_ref, nw_ref, y_ref, acc_ref):
    j = pl.program_id(1)
    tm = h_ref.shape[0]
    kt = PEER_KEYS // BF16_ROWS

    @pl.when(j == 0)
    def _():
        acc_ref[...] = jnp.zeros_like(acc_ref)

    zero = jnp.zeros((), BF16)
    h = h_ref[...]
    total = None
    for sb in range(PEER_KB // PEER_SUB):
        rows = slice(sb * PEER_SUB * PEER_KEYS, (sb + 1) * PEER_SUB * PEER_KEYS)
        act = _gelu_tanh(_dot_nt(u_ref[rows, :], h))
        act = act.reshape(PEER_SUB * kt, BF16_ROWS, tm).astype(BF16)
        parts = []
        for s in range(PEER_SUB):
            ii = sb * PEER_SUB + s
            w = None
            for hd in range(PEER_HEADS):
                need = jnp.broadcast_to(th_ref[ii, hd:hd + 1, :], (BF16_ROWS, tm)).astype(BF16)
                pr = jnp.broadcast_to(p_ref[ii, hd:hd + 1, :], (BF16_ROWS, tm)).astype(BF16)
                sel = jnp.where(code_ref[hd] >= need[None], r_ref[hd], zero) * pr[None]
                w = sel if w is None else w + sel
            parts.append(w * act[s * kt:(s + 1) * kt])
        a = jnp.concatenate(parts, axis=0).reshape(PEER_SUB * PEER_KEYS, tm)
        contrib = _dot(vt_ref[:, rows], a)
        total = contrib if total is None else total + contrib
    acc_ref[...] += total

    @pl.when(j == pl.num_programs(1) - 1)
    def _():
        x1 = x_ref[...]
        g, rr, d = x1.shape
        x2 = x1 + g2_ref[...] * acc_ref[...].T.reshape(g, rr, d)
        y_ref[...] = x2 * lax.rsqrt(jnp.mean(x2 * x2, axis=-1, keepdims=True) + EPS) * nw_ref[...]


def _peer_dense(h2, u_b, vt_b, th, p, code, r, x1_3, mod3, norm_f_w, mod_per_group):
    t = h2.shape[0]
    tm = min(PEER_TM, t)
    ng, rr, d = x1_3.shape
    gt = tm // rr
    if mod_per_group:
        g2_spec = pl.BlockSpec((gt, 1, d), lambda i, j: (i, 0, 5))
    else:
        tiles_per_mod = (ng // mod3.shape[0]) // gt
        g2_spec = pl.BlockSpec((1, 1, d), lambda i, j: (i // tiles_per_mod, 0, 5))
    x_spec = pl.BlockSpec((gt, rr, d), lambda i, j: (i, 0, 0))
    eb = PEER_KB * PEER_KEYS
    nj = PEER_KEYS // PEER_KB
    kt = PEER_KEYS // BF16_ROWS
    mask_spec = pl.BlockSpec((PEER_HEADS, kt, BF16_ROWS, tm), lambda i, j: (0, 0, 0, i))
    return pl.pallas_call(
        _peer_kernel,
        grid=(t // tm, nj),
        in_specs=[pl.BlockSpec((tm, D_MODEL), lambda i, j: (i, 0)),
                  pl.BlockSpec((eb, D_MODEL), lambda i, j: (j, 0)),
                  pl.BlockSpec((D_MODEL, eb), lambda i, j: (0, j)),
                  pl.BlockSpec((PEER_KB, PEER_HEADS, tm), lambda i, j: (j, 0, i)),
                  pl.BlockSpec((PEER_KB, PEER_HEADS, tm), lambda i, j: (j, 0, i)),
                  mask_spec, mask_spec, x_spec, g2_spec, pl.BlockSpec((1, 1, d), lambda i, j: (0, 0, 0))],
        out_specs=x_spec,
        out_shape=jax.ShapeDtypeStruct((ng, rr, d), F32),
        scratch_shapes=[pltpu.VMEM((D_MODEL, tm), F32)],
        compiler_params=_params("arbitrary", "arbitrary"),
        name="peer_dense",
    )(h2, u_b, vt_b, th, p, code, r, x1_3, mod3, norm_f_w.reshape(1, 1, d))


def _pack_w_in(w_in):
    q_end = 3 * SB_WIDTH + SSM_INNER + CONV_DIM
    dt_w = jnp.pad(w_in[:, q_end:q_end + SSM_HEADS], ((0, 0), (0, LANES - SSM_HEADS)))
    return jnp.concatenate([w_in[:, :q_end], dt_w, w_in[:, q_end + SSM_HEADS:]], axis=1).astype(BF16)


def _pack_peer(peer_w_query, peer_keys1, peer_keys2):
    d = peer_w_query.shape[0]
    wq = peer_w_query.reshape(d, PEER_HEADS, 2, PEER_HALF)
    wq_t = jnp.transpose(wq, (2, 1, 3, 0)).reshape(2 * PEER_HEADS * PEER_HALF, d).astype(BF16)
    eye = jnp.eye(PEER_HEADS, dtype=F32)

    def block_diag(keys):
        return jnp.einsum("kd,hg->khgd", keys, eye).reshape(PEER_KEYS * PEER_HEADS, PEER_HEADS * PEER_HALF).astype(BF16)

    return wq_t, block_diag(peer_keys1), block_diag(peer_keys2)


def _layer_tail(o_attn, y_ssm, proj, x3, mod3, lw, gpb, bpm):
    x1_3, h2 = _merge(o_attn, y_ssm, proj["ga"], proj["gs"], x3, mod3, lw["norm2_w"], lw["wa"], lw["ws"], lw["wo"],
                      gpb, bpm)
    th, p, code, r = _route(h2, lw["wq_t"], lw["kbd1"], lw["kbd2"])
    return _peer_dense(h2, lw["u_b"], lw["vt_b"], th, p, code, r, x1_3, mod3, lw["norm_f_w"], mod_per_group=gpb > 1)


def _proj_dict(outs):
    names = ("q", "k", "v", "kb", "vb", "z", "xbc", "dt", "ga", "gs")
    return dict(zip(names, outs))


INPROJ_TM = 256


def kernel(x_prompt, x_sample, cache_k, cache_v, state_conv, state_ssm, page_table, c_prompt, c_sample, norm1_w,
           norm2_w, w_ada, b_ada, w_in, sb_bias, conv_w, conv_b, dt_bias, a_log, d_skip, ssm_norm_w, w_br_attn,
           w_br_ssm, w_out, peer_w_query, peer_keys1, peer_keys2, peer_u, peer_v, norm_f_w):
    batch, seq, d = x_prompt.shape
    n_seq, dec_seq, _ = x_sample.shape
    assert w_ada.shape[0] == 1, "single layer"
    lw = {"norm2_w": norm2_w[0], "wa": w_br_attn[0].astype(BF16), "ws": w_br_ssm[0].astype(BF16),
          "wo": w_out[0].astype(BF16), "norm_f_w": norm_f_w,
          "u_b": peer_u[0].astype(BF16), "vt_b": peer_v[0].T.astype(BF16)}
    lw["wq_t"], lw["kbd1"], lw["kbd2"] = _pack_peer(peer_w_query[0], peer_keys1[0], peer_keys2[0])
    w_packed = _pack_w_in(w_in[0])
    consts = _ssd_consts(conv_w[0], conv_b[0], dt_bias[0], a_log[0], d_skip[0], ssm_norm_w[0])

    mod = _adaln(jnp.concatenate([c_prompt, c_sample], axis=0), w_ada[0], b_ada[0])
    mod_p = mod[:batch].reshape(batch, 1, 6 * d)
    mod_s = mod[batch:].reshape(n_seq, 1, 6 * d)

    tm = INPROJ_TM
    bpm = seq // tm
    xp3 = x_prompt.reshape(batch * bpm, tm, d)
    pp = _proj_dict(_inproj(xp3, mod_p, norm1_w[0], w_packed, 1, bpm, kv_transposed=True))
    o_attn_p = _attn_prompt(pp["q"], pp["kb"], pp["vb"], sb_bias[0], batch, seq)
    y_ssm_p, tail_p, st_p = _ssd_prompt(pp["xbc"], pp["z"], pp["dt"], consts, batch, seq)
    y_prompt = _layer_tail(o_attn_p, y_ssm_p, pp, xp3, mod_p, lw, 1, bpm).reshape(batch, seq, d)

    gpb = tm // dec_seq
    ps = _proj_dict(_inproj(x_sample, mod_s, norm1_w[0], w_packed, gpb, 1))
    o_attn_s = _attn_sample(ps["q"], ps["k"], ps["v"], cache_k[0].transpose(0, 2, 3, 1),
                            cache_v[0].transpose(0, 2, 3, 1), page_table, sb_bias[0], dec_seq)
    prev = jnp.pad(state_conv[0], ((0, 0), (dec_seq - (CONV_W - 1), 0), (0, 0))).reshape(n_seq * dec_seq, CONV_DIM)
    st_in = state_ssm[0].reshape(n_seq, SSM_PAIRS, 2 * SSM_P, SSM_N).transpose(0, 1, 3, 2)
    y_ssm_s, st_s = _ssd_sample(ps["xbc"], prev, ps["z"], ps["dt"], st_in, consts, dec_seq)
    y_sample = _layer_tail(o_attn_s, y_ssm_s, ps, x_sample, mod_s, lw, gpb, 1)

    def state_out(st):
        b = st.shape[0]
        return st.transpose(0, 1, 3, 2).reshape(1, b, SSM_HEADS, SSM_P, SSM_N)

    k_prompt = pp["k"].reshape(1, batch, SB_HEADS, SB_HEAD_DIM, seq).transpose(0, 1, 4, 2, 3)
    v_prompt = pp["v"].reshape(1, batch, SB_HEADS, SB_HEAD_DIM, seq).transpose(0, 1, 4, 2, 3)
    conv_prompt = tail_p[:, SUBLANES - (CONV_W - 1):, :][None]
    k_sample = ps["k"].reshape(1, n_seq, dec_seq, SB_HEADS, SB_HEAD_DIM)
    v_sample = ps["v"].reshape(1, n_seq, dec_seq, SB_HEADS, SB_HEAD_DIM)
    conv_sample = ps["xbc"].reshape(n_seq, dec_seq, CONV_DIM)[:, dec_seq - (CONV_W - 1):, :][None]
    return (y_prompt, y_sample, k_prompt, v_prompt, conv_prompt, state_out(st_p), k_sample, v_sample, conv_sample,
            state_out(st_s))
```

```python
import functools
import math

import jax
import jax.numpy as jnp
from jax import lax
from jax.experimental import pallas as pl
from jax.experimental.pallas import tpu as pltpu

F32 = jnp.float32
BF16 = jnp.bfloat16

D_MODEL = 1024
EPS = 1e-6
SB_HEADS = 8
SB_HEAD_DIM = 64
SB_WIDTH = SB_HEADS * SB_HEAD_DIM
PAGE = 128
SSM_HEADS = 16
SSM_P = 64
SSM_GROUPS = 2
SSM_N = 128
SSM_INNER = SSM_HEADS * SSM_P
SSM_PAIRS = SSM_HEADS // 2
CONV_W = 4
CONV_DIM = SSM_INNER + 2 * SSM_GROUPS * SSM_N
SSM_CHUNK = 128
PEER_HEADS = 8
PEER_KEYS = 128
PEER_HALF = 128
PEER_TOPK = 16

LANES = 128
SUBLANES = 8
VMEM_LIMIT = 56 * 1024 * 1024

_PROJ_GROUPS = (("q", SB_WIDTH), ("k", SB_WIDTH), ("v", SB_WIDTH), ("z", SSM_INNER), ("xbc", CONV_DIM),
                ("dt", LANES), ("ga", D_MODEL), ("gs", D_MODEL))
_PROJ_COLS = sum(w for _, w in _PROJ_GROUPS)


def _params(*sem):
    return pltpu.CompilerParams(dimension_semantics=sem, vmem_limit_bytes=VMEM_LIMIT)


def _dot(a, b):
    return jnp.dot(a, b, preferred_element_type=F32)


def _dot_nt(a, b):
    return lax.dot_general(a, b, (((1,), (1,)), ((), ())), preferred_element_type=F32)


def _split3(x):
    hi = x.astype(BF16)
    r = x - hi.astype(F32)
    mid = r.astype(BF16)
    lo = (r - mid.astype(F32)).astype(BF16)
    return hi, mid, lo


def _dot3_l(x, m):
    hi, mid, lo = _split3(x)
    return _dot(hi, m) + _dot(mid, m) + _dot(lo, m)


def _dot3_r(m, x):
    hi, mid, lo = _split3(x)
    return _dot(m, hi) + _dot(m, mid) + _dot(m, lo)


def _softplus(x):
    return jnp.maximum(x, 0.0) + jnp.log1p(jnp.exp(-jnp.abs(x)))


def _silu(x):
    return x * jax.nn.sigmoid(x)


def _adaln_kernel(c_ref, w_ref, b_ref, o_ref):
    s = _silu(c_ref[...]).astype(BF16)
    o_ref[...] = _dot(s, w_ref[...].astype(BF16)) + b_ref[...]


def _adaln(c_all, w_ada, b_ada):
    m, n = c_all.shape[0], w_ada.shape[1]
    tn = 1024
    return pl.pallas_call(
        _adaln_kernel,
        grid=(n // tn,),
        in_specs=[pl.BlockSpec((m, D_MODEL), lambda j: (0, 0)),
                  pl.BlockSpec((D_MODEL, tn), lambda j: (0, j)),
                  pl.BlockSpec((1, tn), lambda j: (0, j))],
        out_specs=pl.BlockSpec((m, tn), lambda j: (0, j)),
        out_shape=jax.ShapeDtypeStruct((m, n), F32),
        compiler_params=_params("arbitrary"),
        name="adaln",
    )(c_all, w_ada, b_ada.reshape(1, n))


def _inproj_kernel(kv_transposed, x_ref, sh_ref, sc_ref, nw_ref, w_ref, q_ref, k_ref, v_ref, kb_ref, vb_ref, z_ref,
                   xbc_ref, dt_ref, ga_ref, gs_ref):
    x = x_ref[...]
    g, r, d = x.shape
    y = x * lax.rsqrt(jnp.mean(x * x, axis=-1, keepdims=True) + EPS) * nw_ref[...]
    h = y * (1.0 + sc_ref[...]) + sh_ref[...]
    hb = h.reshape(g * r, d).astype(BF16)
    outs = {"q": q_ref, "k": k_ref, "v": v_ref, "z": z_ref, "xbc": xbc_ref, "dt": dt_ref, "ga": ga_ref, "gs": gs_ref}
    col = 0
    for name, width in _PROJ_GROUPS:
        res = _dot(hb, w_ref[:, col:col + width])
        col += width
        if name == "q":
            res = res * (SB_HEAD_DIM ** -0.5)
        if name in ("k", "v") and kv_transposed:
            outs[name][0] = res.T
        else:
            outs[name][...] = res.astype(outs[name].dtype)
        if name == "k":
            kb_ref[...] = res.astype(BF16)
        if name == "v":
            vb_ref[...] = res.astype(BF16)


def _inproj(x3, mod3, norm_w, w_packed, groups_per_block, blocks_per_mod, kv_transposed=False):
    ng, r, d = x3.shape
    gpb = groups_per_block
    nblk = ng // gpb
    tm = gpb * r
    t = ng * r
    if gpb == 1:
        mod_map = lambda col: (lambda i: (i // blocks_per_mod, 0, col))
    else:
        mod_map = lambda col: (lambda i: (i, 0, col))
    widths = dict(_PROJ_GROUPS)
    out_shapes = [
        jax.ShapeDtypeStruct((t, widths["q"]), F32), jax.ShapeDtypeStruct((t, widths["k"]), F32),
        jax.ShapeDtypeStruct((t, widths["v"]), F32), jax.ShapeDtypeStruct((t, widths["k"]), BF16),
        jax.ShapeDtypeStruct((t, widths["v"]), BF16), jax.ShapeDtypeStruct((t, widths["z"]), F32),
        jax.ShapeDtypeStruct((t, widths["xbc"]), F32), jax.ShapeDtypeStruct((t, widths["dt"]), F32),
        jax.ShapeDtypeStruct((t, widths["ga"]), F32), jax.ShapeDtypeStruct((t, widths["gs"]), F32)]
    out_specs = [pl.BlockSpec((tm, s.shape[1]), lambda i: (i, 0)) for s in out_shapes]
    if kv_transposed:
        assert gpb == 1
        for idx in (1, 2):
            out_shapes[idx] = jax.ShapeDtypeStruct((ng // blocks_per_mod, SB_WIDTH, blocks_per_mod * r), F32)
            out_specs[idx] = pl.BlockSpec((1, SB_WIDTH, tm), lambda i: (i // blocks_per_mod, 0, i % blocks_per_mod))
    return pl.pallas_call(
        functools.partial(_inproj_kernel, kv_transposed),
        grid=(nblk,),
        in_specs=[pl.BlockSpec((gpb, r, d), lambda i: (i, 0, 0)),
                  pl.BlockSpec((gpb, 1, d), mod_map(0)),
                  pl.BlockSpec((gpb, 1, d), mod_map(1)),
                  pl.BlockSpec((1, 1, d), lambda i: (0, 0, 0)),
                  pl.BlockSpec((d, _PROJ_COLS), lambda i: (0, 0))],
        out_specs=out_specs,
        out_shape=out_shapes,
        compiler_params=_params("arbitrary"),
        name="inproj",
    )(x3, mod3, mod3, norm_w.reshape(1, 1, d), w_packed)


ATT_TQ = 512
ATT_TK = 512
ATT_CUM = 256
ATT_HEADS = 2


def _stick_incl(z, mask, tri):
    sp = jnp.maximum(z, 0.0) + jnp.log(1.0 + jnp.exp(-jnp.abs(z)))
    if mask is not None:
        sp = jnp.where(mask, sp, 0.0)
    hi = sp.astype(BF16)
    lo = (sp - hi.astype(F32)).astype(BF16)
    n, cols = z.shape
    wd = tri.shape[0]
    stacked = jnp.concatenate([hi, lo], axis=0)
    chunks = []
    carry = None
    for c in range(cols // wd - 1, -1, -1):
        both = _dot(stacked[:, c * wd:(c + 1) * wd], tri)
        inc = both[:n] + both[n:]
        if carry is not None:
            inc = inc + carry
        carry = inc[:, 0:1]
        chunks.append(inc)
    return chunks[0] if len(chunks) == 1 else jnp.concatenate(chunks[::-1], axis=1)


def _stick_tile(z, mask, tri, cum):
    incl = _stick_incl(z, mask, tri)
    w = jnp.exp(z - incl - cum)
    if mask is not None:
        w = jnp.where(mask, w, 0.0)
    return w, cum + incl[:, 0:1]


def _attn_prompt_kernel(bias_ref, q_ref, k_ref, v_ref, tri_ref, o_ref):
    hg = pl.program_id(1)
    qi = pl.program_id(2)
    tq, tk = ATT_TQ, ATT_TK
    ratio = tq // tk
    width = ATT_HEADS * SB_HEAD_DIM
    q = q_ref[...].astype(BF16)
    lane_head = lax.broadcasted_iota(jnp.int32, (tq, width), 1) // SB_HEAD_DIM
    rel = lax.broadcasted_iota(jnp.int32, (tq, tk), 0) - lax.broadcasted_iota(jnp.int32, (tq, tk), 1)
    tri = tri_ref[...]
    qhs = [jnp.where(lane_head == hh, q, jnp.zeros_like(q)) for hh in range(ATT_HEADS)]
    biases = [bias_ref[hg * ATT_HEADS + hh] for hh in range(ATT_HEADS)]

    def block(kb, carry, mask):
        off = pl.multiple_of(kb * tk, tk)
        kblk = k_ref[pl.ds(off, tk), :]
        vblk = v_ref[pl.ds(off, tk), :]
        out = []
        for hh in range(ATT_HEADS):
            acc, cum = carry[hh]
            z = _dot_nt(qhs[hh], kblk) + biases[hh]
            w, cum = _stick_tile(z, mask, tri, cum)
            out.append((acc + _dot(w.astype(BF16), vblk), cum))
        return tuple(out)

    carry = tuple((jnp.zeros((tq, width), F32), jnp.zeros((tq, 1), F32)) for _ in range(ATT_HEADS))
    for d in range(ratio - 1, -1, -1):
        carry = block(qi * ratio + d, carry, rel > d * tk)
    res = lax.fori_loop(0, qi * ratio, lambda step, c: block(qi * ratio - 1 - step, c, None), carry)
    o = res[0][0]
    for hh in range(1, ATT_HEADS):
        o = jnp.where(lane_head == hh, res[hh][0], o)
    o_ref[...] = o.astype(o_ref.dtype)


def _tri_incl(n):
    r = lax.broadcasted_iota(jnp.int32, (n, n), 0)
    c = lax.broadcasted_iota(jnp.int32, (n, n), 1)
    return (r >= c).astype(BF16)


def _attn_prompt(q, kb, vb, sb_bias, batch, seq):
    tq = ATT_TQ
    nq = seq // tq
    width = ATT_HEADS * SB_HEAD_DIM
    return pl.pallas_call(
        _attn_prompt_kernel,
        grid=(batch, SB_HEADS // ATT_HEADS, nq),
        in_specs=[pl.BlockSpec(memory_space=pltpu.SMEM),
                  pl.BlockSpec((tq, width), lambda b, h, i: (b * nq + i, h)),
                  pl.BlockSpec((seq, width), lambda b, h, i: (b, h)),
                  pl.BlockSpec((seq, width), lambda b, h, i: (b, h)),
                  pl.BlockSpec((ATT_CUM, ATT_CUM), lambda b, h, i: (0, 0))],
        out_specs=pl.BlockSpec((tq, width), lambda b, h, i: (b * nq + i, h)),
        out_shape=jax.ShapeDtypeStruct((batch * seq, SB_WIDTH), BF16),
        compiler_params=_params("arbitrary", "arbitrary", "arbitrary"),
        name="attn_prompt",
    )(sb_bias, q, kb, vb, _tri_incl(ATT_CUM))


SAMPLE_SEQS = 2


def _attn_sample_kernel(n_pages, dec_seq, pt_ref, bias_ref, q_ref, kn_ref, vn_ref, tri_ref, *refs):
    n_in = SAMPLE_SEQS * n_pages
    k_pages = refs[:n_in]
    v_pages = refs[n_in:2 * n_in]
    o_ref = refs[2 * n_in]
    rows = SB_HEADS * dec_seq
    row_head = lax.broadcasted_iota(jnp.int32, (rows, SB_WIDTH), 0) // dec_seq
    lane_head = lax.broadcasted_iota(jnp.int32, (rows, SB_WIDTH), 1) // SB_HEAD_DIM
    bias = bias_ref[...]
    tri = tri_ref[...]
    r_i = lax.broadcasted_iota(jnp.int32, (rows, PAGE), 0) % dec_seq
    c_i = lax.broadcasted_iota(jnp.int32, (rows, PAGE), 1)
    new_valid = c_i < r_i
    lane_head8 = lax.broadcasted_iota(jnp.int32, (dec_seq, SB_WIDTH), 1) // SB_HEAD_DIM

    def pad_rows(a):
        return jnp.concatenate([a, jnp.zeros((PAGE - dec_seq, SB_WIDTH), a.dtype)], axis=0)

    def page_tile(ref):
        return ref[0].reshape(SB_WIDTH, PAGE).astype(BF16)

    for sq in range(SAMPLE_SEQS):
        tok = slice(sq * dec_seq, (sq + 1) * dec_seq)
        kp = k_pages[sq * n_pages:(sq + 1) * n_pages]
        vp = v_pages[sq * n_pages:(sq + 1) * n_pages]
        qt = jnp.concatenate([q_ref[tok, :]] * SB_HEADS, axis=0)
        qbd = jnp.where(row_head == lane_head, qt, 0.0).astype(BF16)
        zs = [_dot_nt(qbd, pad_rows(kn_ref[tok, :]).astype(BF16)) + bias]
        zs += [_dot(qbd, page_tile(kp[p])) + bias for p in range(n_pages - 1, -1, -1)]
        masks = [new_valid] + [None] * n_pages
        incls = [_stick_incl(z, m, tri) for z, m in zip(zs, masks)]
        cum = jnp.zeros((rows, 1), F32)
        acc = None
        for b, (z, m, incl) in enumerate(zip(zs, masks, incls)):
            w = jnp.exp(z - incl - cum)
            cum = cum + incl[:, 0:1]
            if m is not None:
                w = jnp.where(m, w, 0.0)
            wb = w.astype(BF16)
            if b == 0:
                contrib = _dot(wb, pad_rows(vn_ref[tok, :]).astype(BF16))
            else:
                contrib = _dot_nt(wb, page_tile(vp[n_pages - b]))
            acc = contrib if acc is None else acc + contrib
        out = jnp.zeros((dec_seq, SB_WIDTH), F32)
        for h in range(SB_HEADS):
            out = out + jnp.where(lane_head8 == h, acc[h * dec_seq:(h + 1) * dec_seq, :], 0.0)
        o_ref[tok, :] = out.astype(o_ref.dtype)


def _attn_sample(q, k_new, v_new, cache_k, cache_v, page_table, sb_bias, dec_seq):
    n_seq, n_pages = page_table.shape
    sps = SAMPLE_SEQS
    rows = SB_HEADS * dec_seq
    bias_rows = jnp.broadcast_to(jnp.repeat(sb_bias, dec_seq)[:, None], (rows, PAGE))

    def page_spec(sq, p):
        return pl.BlockSpec((1, SB_HEADS, SB_HEAD_DIM, PAGE), lambda s, pt: (pt[s * sps + sq, p], 0, 0, 0))

    page_specs = [page_spec(sq, p) for sq in range(sps) for p in range(n_pages)]
    tok_spec = pl.BlockSpec((sps * dec_seq, SB_WIDTH), lambda s, pt: (s, 0))
    grid_spec = pltpu.PrefetchScalarGridSpec(
        num_scalar_prefetch=1,
        grid=(n_seq // sps,),
        in_specs=[pl.BlockSpec((rows, PAGE), lambda s, pt: (0, 0)), tok_spec, tok_spec, tok_spec,
                  pl.BlockSpec((PAGE, PAGE), lambda s, pt: (0, 0))] + page_specs + page_specs,
        out_specs=tok_spec,
    )
    n_in = sps * n_pages
    return pl.pallas_call(
        functools.partial(_attn_sample_kernel, n_pages, dec_seq),
        grid_spec=grid_spec,
        out_shape=jax.ShapeDtypeStruct((n_seq * dec_seq, SB_WIDTH), F32),
        compiler_params=_params("arbitrary"),
        name="attn_sample",
    )(page_table, bias_rows, q, k_new, v_new, _tri_incl(PAGE), *([cache_k] * n_in), *([cache_v] * n_in))


def _expand_matrix(reps):
    r = lax.broadcasted_iota(jnp.int32, (LANES, SSM_HEADS * reps), 0)
    c = lax.broadcasted_iota(jnp.int32, (LANES, SSM_HEADS * reps), 1)
    return (c // reps == r).astype(BF16)


def _ssd_core(act, dt_raw, z, valid, tri, e1, e2, dtb, a_row, dsk, nw, state_in, c_masks):
    xs = act[:, :SSM_INNER]
    bm = act[:, SSM_INNER:SSM_INNER + SSM_GROUPS * SSM_N]
    cm = act[:, SSM_INNER + SSM_GROUPS * SSM_N:]
    dt = _softplus(dt_raw + dtb)
    dta = dt * a_row
    acs = _dot3_r(tri, dta)
    acs_t = acs.T
    acs_e2 = _dot3_l(acs, e2)
    dt_e = _dot3_l(dt, e1)
    eacs_e = _dot3_l(jnp.exp(acs), e1)
    n_slots = len(state_in)
    dend_e = []
    cdec_e = []
    for s in range(n_slots):
        if c_masks[s] is None:
            last = acs[SSM_CHUNK - 1:SSM_CHUNK, :]
        else:
            last = jnp.min(jnp.where(c_masks[s], acs, jnp.inf), axis=0, keepdims=True)
        de = jnp.exp(last - acs)
        if c_masks[s] is not None:
            de = jnp.where(c_masks[s], de, 0.0)
        dend_e.append(_dot3_l(de, e1))
        cdec_e.append(_dot3_l(jnp.broadcast_to(jnp.exp(last), (SUBLANES, LANES)), e1)[0:1, :])
    bmb = [bm[:, g * SSM_N:(g + 1) * SSM_N] for g in range(SSM_GROUPS)]
    cmb = [cm[:, g * SSM_N:(g + 1) * SSM_N].astype(BF16) for g in range(SSM_GROUPS)]
    cb = [_dot_nt(cmb[g], bmb[g].astype(BF16)) for g in range(SSM_GROUPS)]
    bt = [bmb[g].T.astype(BF16) for g in range(SSM_GROUPS)]
    lane = lax.broadcasted_iota(jnp.int32, (SSM_CHUNK, LANES), 1)
    y_pairs = []
    new_states = [[None] * SSM_PAIRS for _ in range(n_slots)]
    for k in range(SSM_PAIRS):
        g = (2 * k) // (SSM_HEADS // SSM_GROUPS)
        sl = slice(k * LANES, (k + 1) * LANES)
        xdt = xs[:, sl] * dt_e[:, sl]
        xdt_b = xdt.astype(BF16)
        parts = []
        for hh in range(2):
            h = 2 * k + hh
            seg = acs_e2[:, h * LANES:(h + 1) * LANES] - acs_t[h:h + 1, :]
            decay = jnp.where(valid, jnp.exp(jnp.where(valid, seg, 0.0)), 0.0)
            parts.append(_dot((cb[g] * decay).astype(BF16), xdt_b))
        y = jnp.where(lane < SSM_P, parts[0], parts[1])
        y_off = None
        for s in range(n_slots):
            st = state_in[s][k]
            c_s = cmb[g] if c_masks[s] is None else jnp.where(c_masks[s], cmb[g], jnp.zeros_like(cmb[g]))
            contrib = _dot(c_s, st.astype(BF16))
            y_off = contrib if y_off is None else y_off + contrib
            xdd = (xdt * dend_e[s][:, sl]).astype(BF16)
            new_states[s][k] = st * cdec_e[s][:, sl] + _dot(bt[g], xdd)
        y = y + y_off * eacs_e[:, sl] + dsk[:, sl] * xs[:, sl]
        y_pairs.append(y)
    y = jnp.concatenate(y_pairs, axis=1) * _silu(z)
    gsz = SSM_INNER // SSM_GROUPS
    outs = []
    for g in range(SSM_GROUPS):
        yg = y[:, g * gsz:(g + 1) * gsz]
        outs.append(yg * lax.rsqrt(jnp.mean(yg * yg, axis=-1, keepdims=True) + EPS) * nw[:, g * gsz:(g + 1) * gsz])
    return jnp.concatenate(outs, axis=1), new_states


def _ssd_prompt_kernel(xbc_ref, z_ref, dt_ref, cw_ref, cbias_ref, dtb_ref, a_ref, dsk_ref, nw_ref, tri_ref, e1_ref,
                       e2_ref, y_ref, tail_ref, st_ref, xs_scr, st_scr):
    c = pl.program_id(1)

    @pl.when(c == 0)
    def _():
        xs_scr[0:SUBLANES, :] = jnp.zeros((SUBLANES, CONV_DIM), F32)
        st_scr[...] = jnp.zeros_like(st_scr)

    xs_scr[SUBLANES:SUBLANES + SSM_CHUNK, :] = xbc_ref[...]
    conv = cbias_ref[...]
    for i in range(CONV_W):
        off = SUBLANES - (CONV_W - 1) + i
        conv = conv + xs_scr[off:off + SSM_CHUNK, :] * cw_ref[i:i + 1, :]
    tail = xs_scr[SSM_CHUNK:SSM_CHUNK + SUBLANES, :]
    xs_scr[0:SUBLANES, :] = tail
    tail_ref[0] = tail
    act = _silu(conv)
    r = lax.broadcasted_iota(jnp.int32, (SSM_CHUNK, SSM_CHUNK), 0)
    cc = lax.broadcasted_iota(jnp.int32, (SSM_CHUNK, SSM_CHUNK), 1)
    valid = r >= cc
    state_in = [[st_scr[k] for k in range(SSM_PAIRS)]]
    y, new_states = _ssd_core(act, dt_ref[...], z_ref[...], valid, tri_ref[...], e1_ref[...], e2_ref[...],
                              dtb_ref[...], a_ref[...], dsk_ref[...], nw_ref[...], state_in, [None])
    for k in range(SSM_PAIRS):
        st_scr[k] = new_states[0][k]
        st_ref[0, k] = new_states[0][k]
    y_ref[...] = y.astype(y_ref.dtype)


def _ssd_consts(conv_w, conv_b, dt_bias, a_log, d_skip, ssm_norm_w):
    pad = LANES - SSM_HEADS
    cw = jnp.concatenate([conv_w, jnp.zeros((SUBLANES - CONV_W, CONV_DIM), F32)], axis=0)
    dtb = jnp.pad(dt_bias, (0, pad)).reshape(1, LANES)
    a_row = jnp.pad(-jnp.exp(a_log), (0, pad)).reshape(1, LANES)
    dsk = jnp.repeat(d_skip, SSM_P).reshape(1, SSM_INNER)
    return cw, conv_b.reshape(1, CONV_DIM), dtb, a_row, dsk, ssm_norm_w.reshape(1, SSM_INNER)


def _const_spec(shape, ngrid):
    zeros = (0,) * len(shape)
    if ngrid == 1:
        return pl.BlockSpec(shape, lambda i: zeros)
    return pl.BlockSpec(shape, lambda i, j: zeros)


def _ssd_prompt(xbc, z, dt, consts, batch, seq):
    nc = seq // SSM_CHUNK
    cw, cbias, dtb, a_row, dsk, nw = consts
    tok = lambda w: pl.BlockSpec((SSM_CHUNK, w), lambda b, c: (b * nc + c, 0))
    tri = _tri_incl(SSM_CHUNK)
    return pl.pallas_call(
        _ssd_prompt_kernel,
        grid=(batch, nc),
        in_specs=[tok(CONV_DIM), tok(SSM_INNER), tok(LANES),
                  _const_spec(cw.shape, 2), _const_spec(cbias.shape, 2), _const_spec(dtb.shape, 2),
                  _const_spec(a_row.shape, 2), _const_spec(dsk.shape, 2), _const_spec(nw.shape, 2),
                  _const_spec((SSM_CHUNK, SSM_CHUNK), 2), _const_spec((LANES, SSM_INNER), 2),
                  _const_spec((LANES, SSM_HEADS * LANES), 2)],
        out_specs=[tok(SSM_INNER),
                   pl.BlockSpec((1, SUBLANES, CONV_DIM), lambda b, c: (b, 0, 0)),
                   pl.BlockSpec((1, SSM_PAIRS, SSM_N, LANES), lambda b, c: (b, 0, 0, 0))],
        out_shape=[jax.ShapeDtypeStruct((batch * seq, SSM_INNER), BF16),
                   jax.ShapeDtypeStruct((batch, SUBLANES, CONV_DIM), F32),
                   jax.ShapeDtypeStruct((batch, SSM_PAIRS, SSM_N, LANES), F32)],
        scratch_shapes=[pltpu.VMEM((SSM_CHUNK + SUBLANES, CONV_DIM), F32),
                        pltpu.VMEM((SSM_PAIRS, SSM_N, LANES), F32)],
        compiler_params=_params("arbitrary", "arbitrary"),
        name="ssd_prompt",
    )(xbc, z, dt, cw, cbias, dtb, a_row, dsk, nw, tri, _expand_matrix(SSM_P), _expand_matrix(LANES))


def _ssd_sample_kernel(dec_seq, xbc_ref, prev_ref, z_ref, dt_ref, st_in_ref, cw_ref, cbias_ref, dtb_ref, a_ref,
                       dsk_ref, nw_ref, tri_ref, e1_ref, e2_ref, y_ref, st_ref):
    n_slots = SSM_CHUNK // dec_seq
    cur = xbc_ref[...]
    prev = prev_ref[...]
    pos = lax.broadcasted_iota(jnp.int32, (SSM_CHUNK, CONV_DIM), 0) % dec_seq
    conv = cbias_ref[...] + cur * cw_ref[CONV_W - 1:CONV_W, :]
    for k in range(1, CONV_W):
        shifted = jnp.where(pos < k, pltpu.roll(prev, SSM_CHUNK - dec_seq + k, axis=0), pltpu.roll(cur, k, axis=0))
        conv = conv + shifted * cw_ref[CONV_W - 1 - k:CONV_W - k, :]
    act = _silu(conv)
    r = lax.broadcasted_iota(jnp.int32, (SSM_CHUNK, SSM_CHUNK), 0)
    cc = lax.broadcasted_iota(jnp.int32, (SSM_CHUNK, SSM_CHUNK), 1)
    valid = (r >= cc) & ((r // dec_seq) == (cc // dec_seq))
    row_seq = lax.broadcasted_iota(jnp.int32, (SSM_CHUNK, LANES), 0) // dec_seq
    c_masks = [row_seq == s for s in range(n_slots)]
    state_in = [[st_in_ref[s, k] for k in range(SSM_PAIRS)] for s in range(n_slots)]
    y, new_states = _ssd_core(act, dt_ref[...], z_ref[...], valid, tri_ref[...], e1_ref[...], e2_ref[...],
                              dtb_ref[...], a_ref[...], dsk_ref[...], nw_ref[...], state_in, c_masks)
    for s in range(n_slots):
        for k in range(SSM_PAIRS):
            st_ref[s, k] = new_states[s][k]
    y_ref[...] = y.astype(y_ref.dtype)


def _ssd_sample(xbc, prev, z, dt, state_t, consts, dec_seq):
    n_tok = xbc.shape[0]
    n_slots = SSM_CHUNK // dec_seq
    nblk = n_tok // SSM_CHUNK
    cw, cbias, dtb, a_row, dsk, nw = consts
    tok = lambda w: pl.BlockSpec((SSM_CHUNK, w), lambda i: (i, 0))
    st_spec = pl.BlockSpec((n_slots, SSM_PAIRS, SSM_N, LANES), lambda i: (i, 0, 0, 0))
    r = lax.broadcasted_iota(jnp.int32, (SSM_CHUNK, SSM_CHUNK), 0)
    c = lax.broadcasted_iota(jnp.int32, (SSM_CHUNK, SSM_CHUNK), 1)
    tri = ((r >= c) & ((r // dec_seq) == (c // dec_seq))).astype(BF16)
    return pl.pallas_call(
        functools.partial(_ssd_sample_kernel, dec_seq),
        grid=(nblk,),
        in_specs=[tok(CONV_DIM), tok(CONV_DIM), tok(SSM_INNER), tok(LANES), st_spec,
                  _const_spec(cw.shape, 1), _const_spec(cbias.shape, 1), _const_spec(dtb.shape, 1),
                  _const_spec(a_row.shape, 1), _const_spec(dsk.shape, 1), _const_spec(nw.shape, 1),
                  _const_spec((SSM_CHUNK, SSM_CHUNK), 1), _const_spec((LANES, SSM_INNER), 1),
                  _const_spec((LANES, SSM_HEADS * LANES), 1)],
        out_specs=[tok(SSM_INNER), st_spec],
        out_shape=[jax.ShapeDtypeStruct((n_tok, SSM_INNER), BF16),
                   jax.ShapeDtypeStruct(state_t.shape, F32)],
        compiler_params=_params("arbitrary"),
        name="ssd_sample",
    )(xbc, prev, z, dt, state_t, cw, cbias, dtb, a_row, dsk, nw, tri, _expand_matrix(SSM_P), _expand_matrix(LANES))


def _merge_kernel(o_ref, y_ref, ga_ref, gs_ref, x_ref, g1_ref, sh_ref, sc_ref, nw_ref, wa_ref, ws_ref, wo_ref,
                  x1_ref, h2_ref):
    merged = (jax.nn.sigmoid(ga_ref[...]) * _dot(o_ref[...].astype(BF16), wa_ref[...])
              + jax.nn.sigmoid(gs_ref[...]) * _dot(y_ref[...], ws_ref[...]))
    mix = _dot(merged.astype(BF16), wo_ref[...])
    x = x_ref[...]
    g, r, d = x.shape
    x1 = x + g1_ref[...] * mix.reshape(g, r, d)
    x1_ref[...] = x1
    y = x1 * lax.rsqrt(jnp.mean(x1 * x1, axis=-1, keepdims=True) + EPS) * nw_ref[...]
    h2 = y * (1.0 + sc_ref[...]) + sh_ref[...]
    h2_ref[...] = h2.reshape(g * r, d).astype(h2_ref.dtype)


def _mod_map(gpb, blocks_per_mod, col):
    if gpb == 1:
        return lambda i: (i // blocks_per_mod, 0, col)
    return lambda i: (i, 0, col)


def _merge(o_attn, y_ssm, ga, gs, x3, mod3, norm2_w, wa, ws, wo, groups_per_block, blocks_per_mod):
    ng, r, d = x3.shape
    gpb = groups_per_block
    nblk = ng // gpb
    tm = gpb * r
    t = ng * r
    tok = lambda w: pl.BlockSpec((tm, w), lambda i: (i, 0))
    xspec = pl.BlockSpec((gpb, r, d), lambda i: (i, 0, 0))
    mspec = lambda col: pl.BlockSpec((gpb, 1, d), _mod_map(gpb, blocks_per_mod, col))
    return pl.pallas_call(
        _merge_kernel,
        grid=(nblk,),
        in_specs=[tok(SB_WIDTH), tok(SSM_INNER), tok(d), tok(d), xspec, mspec(2), mspec(3), mspec(4),
                  pl.BlockSpec((1, 1, d), lambda i: (0, 0, 0)),
                  _const_spec(wa.shape, 1), _const_spec(ws.shape, 1), _const_spec(wo.shape, 1)],
        out_specs=[xspec, tok(d)],
        out_shape=[jax.ShapeDtypeStruct((ng, r, d), F32), jax.ShapeDtypeStruct((t, d), BF16)],
        compiler_params=_params("arbitrary"),
        name="merge",
    )(o_attn, y_ssm, ga, gs, x3, mod3, mod3, mod3, norm2_w.reshape(1, 1, d), wa, ws, wo)


def _batcher_pairs(n):
    pairs = []
    p = 1
    while p < n:
        k = p
        while k >= 1:
            for j in range(k % p, n - k, 2 * k):
                for i in range(min(k, n - j - k)):
                    if (i + j) // (2 * p) == (i + j + k) // (2 * p):
                        pairs.append((i + j, i + j + k))
            k //= 2
        p *= 2
    return pairs


_SORT16 = _batcher_pairs(PEER_TOPK)


def _sort_desc(vals):
    vals = list(vals)
    for i, j in _SORT16:
        hi = jnp.maximum(vals[i], vals[j])
        lo = jnp.minimum(vals[i], vals[j])
        vals[i], vals[j] = hi, lo
    return vals


def _merge_top(a, b, sort_result=True):
    n = PEER_TOPK
    t = [jnp.maximum(a[k], b[n - 1 - k]) for k in range(n)]
    if not sort_result:
        return t
    d = n // 2
    while d >= 1:
        for k in range(n):
            if (k // d) % 2 == 0:
                hi = jnp.maximum(t[k], t[k + d])
                lo = jnp.minimum(t[k], t[k + d])
                t[k], t[k + d] = hi, lo
        d //= 2
    return t


def _top16_desc(vals, need_sorted=True):
    groups = [_sort_desc(vals[i:i + PEER_TOPK]) for i in range(0, len(vals), PEER_TOPK)]
    while len(groups) > 1:
        nxt = []
        for i in range(0, len(groups) - 1, 2):
            last = len(groups) == 2
            nxt.append(_merge_top(groups[i], groups[i + 1], sort_result=need_sorted or not last))
        if len(groups) % 2:
            nxt.append(groups[-1])
        groups = nxt
    return groups[0]


ROUTE_TM = 512


def _route_kernel(h_ref, wq_ref, k1_ref, k2_ref, th_ref, p_ref, code_ref, r_ref, s1_scr, s2_scr):
    half = PEER_HEADS * PEER_HALF
    qt = _dot_nt(wq_ref[...], h_ref[...])
    s1 = _dot(k1_ref[...], qt[:half].astype(BF16))
    s2 = _dot(k2_ref[...], qt[half:].astype(BF16))
    for c in range(ROUTE_TM // LANES):
        s1_scr[c] = s1[:, c * LANES:(c + 1) * LANES]
        s2_scr[c] = s2[:, c * LANES:(c + 1) * LANES]
    staircase = [(a, b) for a in range(PEER_TOPK) for b in range(PEER_TOPK) if (a + 1) * (b + 1) <= PEER_TOPK]
    neg_inf = jnp.full((SUBLANES, LANES), -jnp.inf, F32)
    for c in range(ROUTE_TM // LANES):
        tl = pl.ds(c * LANES, LANES)
        row = lambda scr, i, c=c: scr[c, i * SUBLANES:(i + 1) * SUBLANES, :]
        top1 = _top16_desc([row(s1_scr, i) for i in range(PEER_KEYS)])
        top2 = _top16_desc([row(s2_scr, i) for i in range(PEER_KEYS)])
        cand = {(a, b): top1[a] + top2[b] for a, b in staircase}
        cvals = [cand[ab] for ab in staircase]
        cvals += [neg_inf] * (-len(cvals) % PEER_TOPK)
        tau = _top16_desc(cvals, need_sorted=False)
        thr = tau[0]
        for v in tau[1:]:
            thr = jnp.minimum(thr, v)
        cmax = cand[(0, 0)]
        zsum = jnp.zeros((SUBLANES, LANES), F32)
        never = jnp.full((SUBLANES, LANES), PEER_TOPK + 1.0, F32)
        need = [never] * PEER_TOPK
        for a, b in staircase:
            ok = cand[(a, b)] >= thr
            zsum = zsum + jnp.where(ok, jnp.exp(cand[(a, b)] - cmax), 0.0)
            need[a] = need[a] - jnp.where(ok, 1.0, 0.0)
        inv_z = 1.0 / zsum
        for i in range(PEER_KEYS):
            s = row(s1_scr, i)
            th = never
            for a in range(PEER_TOPK):
                th = jnp.where(s == top1[a], need[a], th)
            th_ref[i, :, tl] = th
            p_ref[i, :, tl] = jnp.exp(s - top1[0]) * inv_z
        for h in range(PEER_HEADS):
            s2h = s2_scr[c, pl.ds(h, PEER_KEYS, stride=SUBLANES), :]
            code = jnp.zeros((PEER_KEYS, LANES), F32)
            for b in range(PEER_TOPK):
                code = code + jnp.where(s2h >= top2[b][h:h + 1, :], 1.0, 0.0)
            tiles = (PEER_KEYS // BF16_ROWS, BF16_ROWS, LANES)
            code_ref[h, :, :, tl] = code.reshape(tiles).astype(BF16)
            r_ref[h, :, :, tl] = jnp.exp(s2h - top2[0][h:h + 1, :]).reshape(tiles).astype(BF16)


def _route(h2, wq_t, kbd1, kbd2):
    t = h2.shape[0]
    tm = ROUTE_TM
    half = PEER_HEADS * PEER_HALF
    kt = PEER_KEYS // BF16_ROWS
    return pl.pallas_call(
        _route_kernel,
        grid=(t // tm,),
        in_specs=[pl.BlockSpec((tm, D_MODEL), lambda i: (i, 0)),
                  _const_spec(wq_t.shape, 1), _const_spec(kbd1.shape, 1), _const_spec(kbd2.shape, 1)],
        out_specs=[pl.BlockSpec((PEER_KEYS, PEER_HEADS, tm), lambda i: (0, 0, i)),
                   pl.BlockSpec((PEER_KEYS, PEER_HEADS, tm), lambda i: (0, 0, i)),
                   pl.BlockSpec((PEER_HEADS, kt, BF16_ROWS, tm), lambda i: (0, 0, 0, i)),
                   pl.BlockSpec((PEER_HEADS, kt, BF16_ROWS, tm), lambda i: (0, 0, 0, i))],
        out_shape=[jax.ShapeDtypeStruct((PEER_KEYS, PEER_HEADS, t), F32),
                   jax.ShapeDtypeStruct((PEER_KEYS, PEER_HEADS, t), F32),
                   jax.ShapeDtypeStruct((PEER_HEADS, kt, BF16_ROWS, t), BF16),
                   jax.ShapeDtypeStruct((PEER_HEADS, kt, BF16_ROWS, t), BF16)],
        scratch_shapes=[pltpu.VMEM((tm // LANES, half, LANES), F32), pltpu.VMEM((tm // LANES, half, LANES), F32)],
        compiler_params=_params("arbitrary"),
        name="peer_route",
    )(h2, wq_t, kbd1, kbd2)


PEER_TM = 1024
PEER_KB = 8
PEER_SUB = 8
BF16_ROWS = 16


def _gelu_tanh(x):
    return 0.5 * x * (1.0 + jnp.tanh(math.sqrt(2.0 / math.pi) * (x + 0.044715 * (x * x * x))))


def _peer_kernel(h_ref, u_ref, vt_ref, th_ref, p_ref, code_ref, r_ref, x_ref, g2_ref, nw_ref, y_ref, acc_ref):
    j = pl.program_id(1)
    tm = h_ref.shape[0]
    kt = PEER_KEYS // BF16_ROWS

    @pl.when(j == 0)
    def _():
        acc_ref[...] = jnp.zeros_like(acc_ref)

    zero = jnp.zeros((), BF16)
    h = h_ref[...]
    total = None
    for sb in range(PEER_KB // PEER_SUB):
        rows = slice(sb * PEER_SUB * PEER_KEYS, (sb + 1) * PEER_SUB * PEER_KEYS)
        act = _gelu_tanh(_dot_nt(u_ref[rows, :], h))
        act = act.reshape(PEER_SUB * kt, BF16_ROWS, tm).astype(BF16)
        parts = []
        for s in range(PEER_SUB):
            ii = sb * PEER_SUB + s
            w = None
            for hd in range(PEER_HEADS):
                need = jnp.broadcast_to(th_ref[ii, hd:hd + 1, :], (BF16_ROWS, tm)).astype(BF16)
                pr = jnp.broadcast_to(p_ref[ii, hd:hd + 1, :], (BF16_ROWS, tm)).astype(BF16)
                sel = jnp.where(code_ref[hd] >= need[None], r_ref[hd], zero) * pr[None]
                w = sel if w is None else w + sel
            parts.append(w * act[s * kt:(s + 1) * kt])
        a = jnp.concatenate(parts, axis=0).reshape(PEER_SUB * PEER_KEYS, tm)
        contrib = _dot(vt_ref[:, rows], a)
        total = contrib if total is None else total + contrib
    acc_ref[...] += total

    @pl.when(j == pl.num_programs(1) - 1)
    def _():
        x1 = x_ref[...]
        g, rr, d = x1.shape
        x2 = x1 + g2_ref[...] * acc_ref[...].T.reshape(g, rr, d)
        y_ref[...] = x2 * lax.rsqrt(jnp.mean(x2 * x2, axis=-1, keepdims=True) + EPS) * nw_ref[...]


def _peer_dense(h2, u_b, vt_b, th, p, code, r, x1_3, mod3, norm_f_w, mod_per_group):
    t = h2.shape[0]
    tm = min(PEER_TM, t)
    ng, rr, d = x1_3.shape
    gt = tm // rr
    if mod_per_group:
        g2_spec = pl.BlockSpec((gt, 1, d), lambda i, j: (i, 0, 5))
    else:
        tiles_per_mod = (ng // mod3.shape[0]) // gt
        g2_spec = pl.BlockSpec((1, 1, d), lambda i, j: (i // tiles_per_mod, 0, 5))
    x_spec = pl.BlockSpec((gt, rr, d), lambda i, j: (i, 0, 0))
    eb = PEER_KB * PEER_KEYS
    nj = PEER_KEYS // PEER_KB
    kt = PEER_KEYS // BF16_ROWS
    mask_spec = pl.BlockSpec((PEER_HEADS, kt, BF16_ROWS, tm), lambda i, j: (0, 0, 0, i))
    return pl.pallas_call(
        _peer_kernel,
        grid=(t // tm, nj),
        in_specs=[pl.BlockSpec((tm, D_MODEL), lambda i, j: (i, 0)),
                  pl.BlockSpec((eb, D_MODEL), lambda i, j: (j, 0)),
                  pl.BlockSpec((D_MODEL, eb), lambda i, j: (0, j)),
                  pl.BlockSpec((PEER_KB, PEER_HEADS, tm), lambda i, j: (j, 0, i)),
                  pl.BlockSpec((PEER_KB, PEER_HEADS, tm), lambda i, j: (j, 0, i)),
                  mask_spec, mask_spec, x_spec, g2_spec, pl.BlockSpec((1, 1, d), lambda i, j: (0, 0, 0))],
        out_specs=x_spec,
        out_shape=jax.ShapeDtypeStruct((ng, rr, d), F32),
        scratch_shapes=[pltpu.VMEM((D_MODEL, tm), F32)],
        compiler_params=_params("arbitrary", "arbitrary"),
        name="peer_dense",
    )(h2, u_b, vt_b, th, p, code, r, x1_3, mod3, norm_f_w.reshape(1, 1, d))


def _pack_w_in(w_in):
    q_end = 3 * SB_WIDTH + SSM_INNER + CONV_DIM
    dt_w = jnp.pad(w_in[:, q_end:q_end + SSM_HEADS], ((0, 0), (0, LANES - SSM_HEADS)))
    return jnp.concatenate([w_in[:, :q_end], dt_w, w_in[:, q_end + SSM_HEADS:]], axis=1).astype(BF16)


def _pack_peer(peer_w_query, peer_keys1, peer_keys2):
    d = peer_w_query.shape[0]
    wq = peer_w_query.reshape(d, PEER_HEADS, 2, PEER_HALF)
    wq_t = jnp.transpose(wq, (2, 1, 3, 0)).reshape(2 * PEER_HEADS * PEER_HALF, d).astype(BF16)
    eye = jnp.eye(PEER_HEADS, dtype=F32)

    def block_diag(keys):
        return jnp.einsum("kd,hg->khgd", keys, eye).reshape(PEER_KEYS * PEER_HEADS, PEER_HEADS * PEER_HALF).astype(BF16)

    return wq_t, block_diag(peer_keys1), block_diag(peer_keys2)


def _layer_tail(o_attn, y_ssm, proj, x3, mod3, lw, gpb, bpm):
    x1_3, h2 = _merge(o_attn, y_ssm, proj["ga"], proj["gs"], x3, mod3, lw["norm2_w"], lw["wa"], lw["ws"], lw["wo"],
                      gpb, bpm)
    th, p, code, r = _route(h2, lw["wq_t"], lw["kbd1"], lw["kbd2"])
    return _peer_dense(h2, lw["u_b"], lw["vt_b"], th, p, code, r, x1_3, mod3, lw["norm_f_w"], mod_per_group=gpb > 1)


def _proj_dict(outs):
    names = ("q", "k", "v", "kb", "vb", "z", "xbc", "dt", "ga", "gs")
    return dict(zip(names, outs))


INPROJ_TM = 256


def kernel(x_prompt, x_sample, cache_k, cache_v, state_conv, state_ssm, page_table, c_prompt, c_sample, norm1_w,
           norm2_w, w_ada, b_ada, w_in, sb_bias, conv_w, conv_b, dt_bias, a_log, d_skip, ssm_norm_w, w_br_attn,
           w_br_ssm, w_out, peer_w_query, peer_keys1, peer_keys2, peer_u, peer_v, norm_f_w):
    batch, seq, d = x_prompt.shape
    n_seq, dec_seq, _ = x_sample.shape
    assert w_ada.shape[0] == 1, "single layer"
    lw = {"norm2_w": norm2_w[0], "wa": w_br_attn[0].astype(BF16), "ws": w_br_ssm[0].astype(BF16),
          "wo": w_out[0].astype(BF16), "norm_f_w": norm_f_w,
          "u_b": peer_u[0].astype(BF16), "vt_b": peer_v[0].T.astype(BF16)}
    lw["wq_t"], lw["kbd1"], lw["kbd2"] = _pack_peer(peer_w_query[0], peer_keys1[0], peer_keys2[0])
    w_packed = _pack_w_in(w_in[0])
    consts = _ssd_consts(conv_w[0], conv_b[0], dt_bias[0], a_log[0], d_skip[0], ssm_norm_w[0])

    mod = _adaln(jnp.concatenate([c_prompt, c_sample], axis=0), w_ada[0], b_ada[0])
    mod_p = mod[:batch].reshape(batch, 1, 6 * d)
    mod_s = mod[batch:].reshape(n_seq, 1, 6 * d)

    tm = INPROJ_TM
    bpm = seq // tm
    xp3 = x_prompt.reshape(batch * bpm, tm, d)
    pp = _proj_dict(_inproj(xp3, mod_p, norm1_w[0], w_packed, 1, bpm, kv_transposed=True))
    o_attn_p = _attn_prompt(pp["q"], pp["kb"], pp["vb"], sb_bias[0], batch, seq)
    y_ssm_p, tail_p, st_p = _ssd_prompt(pp["xbc"], pp["z"], pp["dt"], consts, batch, seq)
    y_prompt = _layer_tail(o_attn_p, y_ssm_p, pp, xp3, mod_p, lw, 1, bpm).reshape(batch, seq, d)

    gpb = tm // dec_seq
    ps = _proj_dict(_inproj(x_sample, mod_s, norm1_w[0], w_packed, gpb, 1))
    o_attn_s = _attn_sample(ps["q"], ps["k"], ps["v"], cache_k[0].transpose(0, 2, 3, 1),
                            cache_v[0].transpose(0, 2, 3, 1), page_table, sb_bias[0], dec_seq)
    prev = jnp.pad(state_conv[0], ((0, 0), (dec_seq - (CONV_W - 1), 0), (0, 0))).reshape(n_seq * dec_seq, CONV_DIM)
    st_in = state_ssm[0].reshape(n_seq, SSM_PAIRS, 2 * SSM_P, SSM_N).transpose(0, 1, 3, 2)
    y_ssm_s, st_s = _ssd_sample(ps["xbc"], prev, ps["z"], ps["dt"], st_in, consts, dec_seq)
    y_sample = _layer_tail(o_attn_s, y_ssm_s, ps, x_sample, mod_s, lw, gpb, 1)

    def state_out(st):
        b = st.shape[0]
        return st.transpose(0, 1, 3, 2).reshape(1, b, SSM_HEADS, SSM_P, SSM_N)

    k_prompt = pp["k"].reshape(1, batch, SB_HEADS, SB_HEAD_DIM, seq).transpose(0, 1, 4, 2, 3)
    v_prompt = pp["v"].reshape(1, batch, SB_HEADS, SB_HEAD_DIM, seq).transpose(0, 1, 4, 2, 3)
    conv_prompt = tail_p[:, SUBLANES - (CONV_W - 1):, :][None]
    k_sample = ps["k"].reshape(1, n_seq, dec_seq, SB_HEADS, SB_HEAD_DIM)
    v_sample = ps["v"].reshape(1, n_seq, dec_seq, SB_HEADS, SB_HEAD_DIM)
    conv_sample = ps["xbc"].reshape(n_seq, dec_seq, CONV_DIM)[:, dec_seq - (CONV_W - 1):, :][None]
    return (y_prompt, y_sample, k_prompt, v_prompt, conv_prompt, state_out(st_p), k_sample, v_sample, conv_sample,
            state_out(st_s))
```
